```python
import math
import jax, jax.numpy as jnp
from jax import lax
import numpy as np

D_MODEL = 1024
BATCH = 4
SEQ = 4096
DEPTH = 1
DEC_BATCH = 32
DEC_SEQ = 64
PAST_LEN = 4096

CHUNK = 64
N_HEADS = 16
KV_HEADS = 4
HEAD_DIM = 64
GROUP = N_HEADS // KV_HEADS
WINDOW = 128
N_WIN_CHUNKS = WINDOW // CHUNK
BAND = WINDOW + CHUNK
CONV_W = 31
D_CONV = D_MODEL
D_FF = ((8 * D_MODEL // 3 + 255) // 256) * 256
NUM_BUCKETS = 32
MAX_DISTANCE = 128
EPS = 1e-6
NEG = -1e30
Q_W = N_HEADS * HEAD_DIM
KV_W = KV_HEADS * HEAD_DIM
IN_W = Q_W + 2 * KV_W + 2 * D_CONV + 2 * D_MODEL

kernel_name = "chunk_causal_hybrid_swa_conformer_step"


def _rmsnorm(x, g):
    x32 = x.astype(jnp.float32)
    y = x32 * lax.rsqrt(jnp.mean(x32 * x32, axis=-1, keepdims=True) + EPS)
    return (y * g.astype(jnp.float32)).astype(x.dtype)


def _layernorm(x, g, b):
    x32 = x.astype(jnp.float32)
    mu = jnp.mean(x32, axis=-1, keepdims=True)
    var = jnp.mean(jnp.square(x32 - mu), axis=-1, keepdims=True)
    y = (x32 - mu) * lax.rsqrt(var + EPS)
    return (y * g.astype(jnp.float32) + b.astype(jnp.float32)).astype(x.dtype)


def _rel_bucket(rel):
    nb = NUM_BUCKETS // 2
    max_exact = nb // 2
    ret = (rel > 0).astype(jnp.int32) * nb
    n = jnp.abs(rel)
    nf = jnp.maximum(n, 1).astype(jnp.float32)
    large = max_exact + (jnp.log(nf / max_exact) / math.log(MAX_DISTANCE / max_exact)
                         * (nb - max_exact)).astype(jnp.int32)
    large = jnp.minimum(large, nb - 1)
    return ret + jnp.where(n < max_exact, n, large)


def _grouped_attention(qb, kb, vb, rel, valid, rel_table, sink):
    s = jnp.einsum('bnqkgd,bnjkd->bnkgqj', qb, kb,
                   preferred_element_type=jnp.float32) * (HEAD_DIM ** -0.5)
    bias = rel_table[_rel_bucket(rel)]
    bias = jnp.transpose(bias, (2, 0, 1)).reshape(KV_HEADS, GROUP, rel.shape[0], rel.shape[1])
    s = s + bias.astype(jnp.float32)
    s = jnp.where(valid[None, :, None, None, None, :], s, NEG)
    sk = sink.astype(jnp.float32).reshape(KV_HEADS, GROUP)[None, None, :, :, None, None]
    m = jnp.maximum(jnp.max(s, axis=-1, keepdims=True), sk)
    p = jnp.exp(s - m)
    p = p / (jnp.sum(p, axis=-1, keepdims=True) + jnp.exp(sk - m))
    o = jnp.einsum('bnkgqj,bnjkd->bnqkgd', p.astype(vb.dtype), vb)
    b, n, q = o.shape[:3]
    return o.reshape(b, n * q, Q_W)


def _swa_prompt(q, k, v, rel_table, sink):
    b, t = q.shape[:2]
    nc = t // CHUNK
    pad = jnp.zeros((b, WINDOW, KV_HEADS, HEAD_DIM), k.dtype)
    kc = jnp.concatenate([pad, k], axis=1).reshape(b, nc + N_WIN_CHUNKS, CHUNK, KV_HEADS, HEAD_DIM)
    vc = jnp.concatenate([pad.astype(v.dtype), v], axis=1).reshape(b, nc + N_WIN_CHUNKS, CHUNK, KV_HEADS, HEAD_DIM)
    kb = jnp.concatenate([kc[:, i:i + nc] for i in range(N_WIN_CHUNKS + 1)], axis=2)
    vb = jnp.concatenate([vc[:, i:i + nc] for i in range(N_WIN_CHUNKS + 1)], axis=2)
    qb = q.reshape(b, nc, CHUNK, KV_HEADS, GROUP, HEAD_DIM)
    kj = jnp.arange(BAND, dtype=jnp.int32)
    rel = kj[None, :] - WINDOW - jnp.arange(CHUNK, dtype=jnp.int32)[:, None]
    valid = (jnp.arange(nc, dtype=jnp.int32)[:, None] * CHUNK - WINDOW + kj[None, :]) >= 0
    return _grouped_attention(qb, kb, vb, rel, valid, rel_table, sink)


def _swa_sample(q, k_all, v_all, rel_table, sink):
    b, ds = q.shape[:2]
    nk = k_all.shape[1]
    qb = q.reshape(b, 1, ds, KV_HEADS, GROUP, HEAD_DIM)
    rel = jnp.arange(nk, dtype=jnp.int32)[None, :] - WINDOW - jnp.arange(ds, dtype=jnp.int32)[:, None]
    valid = jnp.ones((1, nk), dtype=bool)
    return _grouped_attention(qb, k_all[:, None], v_all[:, None], rel, valid, rel_table, sink)


def _conv_module(glu_in, prev, dw_w, dw_b, ln_g, ln_b, w_conv_out):
    a, g = jnp.split(glu_in, 2, axis=-1)
    z = a * jax.nn.sigmoid(g)
    zp = jnp.concatenate([prev.astype(z.dtype), z], axis=1)
    y = lax.conv_general_dilated(zp, dw_w[:, None, :].astype(z.dtype), (1,), 'VALID',
                                 dimension_numbers=('NWC', 'WIO', 'NWC'),
                                 feature_group_count=D_CONV) + dw_b
    y = _layernorm(y, ln_g, ln_b)
    y = y * jax.nn.sigmoid(y)
    return y @ w_conv_out, zp[:, -(CONV_W - 1):]


def _trunk_layer(x, c, k_cache, v_cache, conv_cache, rel_table, w_ada, b_ada, norm1_g, norm2_g,
                 w_in, sink, w_attn_out, dw_w, dw_b, conv_ln_g, conv_ln_b, w_conv_out, w_out,
                 w_ffn_up, w_ffn_down):
    b = x.shape[0]
    mod = jax.nn.silu(c) @ w_ada + b_ada
    sh1, sc1, gt1, sh2, sc2, gt2 = [m[:, None, :] for m in jnp.split(mod, 6, axis=-1)]

    h = _rmsnorm(x, norm1_g) * (1 + sc1) + sh1
    proj = h @ w_in
    o1 = Q_W
    o2 = o1 + KV_W
    o3 = o2 + KV_W
    o4 = o3 + 2 * D_CONV
    q = proj[..., :o1]
    k = proj[..., o1:o2].reshape(b, -1, KV_HEADS, HEAD_DIM)
    v = proj[..., o2:o3].reshape(b, -1, KV_HEADS, HEAD_DIM)
    glu_in = proj[..., o3:o4]
    g_a, g_b = jnp.split(jax.nn.sigmoid(proj[..., o4:]), 2, axis=-1)

    if k_cache is None:
        att = _swa_prompt(q, k, v, rel_table, sink)
        new_k = k[:, -WINDOW:]
        new_v = v[:, -WINDOW:]
        prev = jnp.zeros((b, CONV_W - 1, D_CONV), x.dtype)
    else:
        k_all = jnp.concatenate([k_cache.astype(k.dtype), k], axis=1)
        v_all = jnp.concatenate([v_cache.astype(v.dtype), v], axis=1)
        att = _swa_sample(q, k_all, v_all, rel_table, sink)
        new_k = k_all[:, -WINDOW:]
        new_v = v_all[:, -WINDOW:]
        prev = conv_cache
    y_a = att @ w_attn_out
    y_b, new_conv = _conv_module(glu_in, prev, dw_w, dw_b, conv_ln_g, conv_ln_b, w_conv_out)
    x = x + gt1 * ((g_a * y_a + g_b * y_b) @ w_out)

    h2 = _rmsnorm(x, norm2_g) * (1 + sc2) + sh2
    gate, up = jnp.split(h2 @ w_ffn_up, 2, axis=-1)
    x = x + gt2 * ((jax.nn.silu(gate) * up) @ w_ffn_down)
    return x, new_k, new_v, new_conv


def setup_inputs(seed: int = 0) -> dict:
    key = jax.random.key(seed)
    ks = jax.random.split(key, 32)
    f32 = jnp.float32
    nrm = lambda k, s, sc: jax.random.normal(k, s, f32) * sc
    return {
        "x_prompt": nrm(ks[0], (BATCH, SEQ, D_MODEL), 1.0),
        "x_sample": nrm(ks[1], (DEC_BATCH, DEC_SEQ, D_MODEL), 1.0),
        "cache_k": nrm(ks[2], (DEPTH, DEC_BATCH, WINDOW, KV_HEADS, HEAD_DIM), 1.0),
        "cache_v": nrm(ks[3], (DEPTH, DEC_BATCH, WINDOW, KV_HEADS, HEAD_DIM), 1.0),
        "cache_conv": nrm(ks[4], (DEPTH, DEC_BATCH, CONV_W - 1, D_CONV), 0.5),
        "c_prompt": nrm(ks[5], (BATCH, D_MODEL), 1.0),
        "c_sample": nrm(ks[6], (DEC_BATCH, D_MODEL), 1.0),
        "rel_table": nrm(ks[7], (NUM_BUCKETS, N_HEADS), 0.5),
        "w_ada": nrm(ks[8], (DEPTH, D_MODEL, 6 * D_MODEL), 0.5 * D_MODEL ** -0.5),
        "b_ada": nrm(ks[9], (DEPTH, 6 * D_MODEL), 0.01),
        "norm1_g": 1.0 + nrm(ks[10], (DEPTH, D_MODEL), 0.02),
        "norm2_g": 1.0 + nrm(ks[11], (DEPTH, D_MODEL), 0.02),
        "w_in": nrm(ks[12], (DEPTH, D_MODEL, IN_W), D_MODEL ** -0.5),
        "sink": nrm(ks[13], (DEPTH, N_HEADS), 0.5),
        "w_attn_out": nrm(ks[14], (DEPTH, Q_W, D_MODEL), Q_W ** -0.5),
        "dw_w": nrm(ks[15], (DEPTH, CONV_W, D_CONV), CONV_W ** -0.5),
        "dw_b": nrm(ks[16], (DEPTH, D_CONV), 0.01),
        "conv_ln_g": 1.0 + nrm(ks[17], (DEPTH, D_CONV), 0.02),
        "conv_ln_b": nrm(ks[18], (DEPTH, D_CONV), 0.01),
        "w_conv_out": nrm(ks[19], (DEPTH, D_CONV, D_MODEL), D_CONV ** -0.5),
        "w_out": nrm(ks[20], (DEPTH, D_MODEL, D_MODEL), D_MODEL ** -0.5),
        "w_ffn_up": nrm(ks[21], (DEPTH, D_MODEL, 2 * D_FF), D_MODEL ** -0.5),
        "w_ffn_down": nrm(ks[22], (DEPTH, D_FF, D_MODEL), D_FF ** -0.5),
        "final_g": 1.0 + nrm(ks[23], (D_MODEL,), 0.02),
    }


def reference(x_prompt, x_sample, cache_k, cache_v, cache_conv, c_prompt, c_sample, rel_table,
              w_ada, b_ada, norm1_g, norm2_g, w_in, sink, w_attn_out, dw_w, dw_b, conv_ln_g,
              conv_ln_b, w_conv_out, w_out, w_ffn_up, w_ffn_down, final_g):
    yp = x_prompt
    ys = x_sample
    kp, vp, cp, ks, vs, cs = [], [], [], [], [], []
    for l in range(DEPTH):
        params = (rel_table, w_ada[l], b_ada[l], norm1_g[l], norm2_g[l], w_in[l], sink[l],
                  w_attn_out[l], dw_w[l], dw_b[l], conv_ln_g[l], conv_ln_b[l], w_conv_out[l],
                  w_out[l], w_ffn_up[l], w_ffn_down[l])
        yp, nk, nv, nc = _trunk_layer(yp, c_prompt, None, None, None, *params)
        kp.append(nk); vp.append(nv); cp.append(nc)
        ys, nk, nv, nc = _trunk_layer(ys, c_sample, cache_k[l], cache_v[l], cache_conv[l], *params)
        ks_ = nk
        ks.append(ks_); vs.append(nv); cs.append(nc)
    y_prompt = _rmsnorm(yp, final_g)
    y_sample = _rmsnorm(ys, final_g)
    new_k_prompt = jnp.stack(kp)
    new_v_prompt = jnp.stack(vp)
    new_conv_prompt = jnp.stack(cp)
    new_k_sample = jnp.stack(ks)
    new_v_sample = jnp.stack(vs)
    new_conv_sample = jnp.stack(cs)
    return (y_prompt, y_sample, new_k_prompt, new_v_prompt, new_conv_prompt,
            new_k_sample, new_v_sample, new_conv_sample)
```

```python
import functools
import math

import jax
import jax.numpy as jnp
from jax import lax
from jax.experimental import pallas as pl
from jax.experimental.pallas import tpu as pltpu

f32 = jnp.float32
bf16 = jnp.bfloat16

D_MODEL = 1024
N_HEADS = 16
KV_HEADS = 4
HEAD_DIM = 64
GROUP = N_HEADS // KV_HEADS
WINDOW = 128
CHUNK = 64
BAND = WINDOW + CHUNK
CONV_W = 31
D_FF = 2816
NUM_BUCKETS = 32
MAX_DISTANCE = 128
EPS = 1e-6
NEG = -1e30
Q_W = N_HEADS * HEAD_DIM
KV_W = KV_HEADS * HEAD_DIM
IN_W = Q_W + 2 * KV_W + 4 * D_MODEL
O_K = Q_W
O_V = O_K + KV_W
O_GLU_A = O_V + KV_W
O_GLU_G = O_GLU_A + D_MODEL
O_GA = O_GLU_G + D_MODEL
O_GB = O_GA + D_MODEL

LANES = 128
N_SLABS = D_MODEL // LANES
HIST = 32
ROW_BLK = 32
COL_BLK = 256
FF_BLK = 256
VMEM_LIMIT = 56 * 1024 * 1024


def _sigmoid(x):
    return 1.0 / (1.0 + jnp.exp(-x))


def _rms(x, g):
    ms = jnp.mean(x * x, axis=-1, keepdims=True)
    return x * lax.rsqrt(ms + EPS) * g


def _for_rows(n_rows, blk, body):
    def step(i, c):
        body(pl.multiple_of(i * blk, blk))
        return c
    lax.fori_loop(0, n_rows // blk, step, 0)


def _mm(a, w_ref, c0, width):
    return jnp.dot(a, w_ref[:, c0:c0 + width], preferred_element_type=f32)


def _rel_bucket(rel):
    nb = NUM_BUCKETS // 2
    max_exact = nb // 2
    ret = (rel > 0).astype(jnp.int32) * nb
    n = jnp.abs(rel)
    nf = jnp.maximum(n, 1).astype(f32)
    large = max_exact + (jnp.log(nf / max_exact) / math.log(MAX_DISTANCE / max_exact)
                         * (nb - max_exact)).astype(jnp.int32)
    large = jnp.minimum(large, nb - 1)
    return ret + jnp.where(n < max_exact, n, large)


def _bias_body(idx_ref, tab_ref, o_ref):
    idx = idx_ref[...]
    for h in range(N_HEADS):
        acc = jnp.zeros((CHUNK, BAND), f32)
        for b in range(NUM_BUCKETS):
            acc = jnp.where(idx == b, tab_ref[b, h], acc)
        o_ref[h * CHUNK:(h + 1) * CHUNK, :] = acc


def _bias_table(rel_table):
    kj = jnp.arange(BAND, dtype=jnp.int32)
    rel = kj[None, :] - WINDOW - jnp.arange(CHUNK, dtype=jnp.int32)[:, None]
    idx = _rel_bucket(rel)
    return pl.pallas_call(
        _bias_body,
        in_specs=[pl.BlockSpec(memory_space=pltpu.VMEM), pl.BlockSpec(memory_space=pltpu.SMEM)],
        out_specs=pl.BlockSpec(memory_space=pltpu.VMEM),
        out_shape=jax.ShapeDtypeStruct((N_HEADS * CHUNK, BAND), f32),
        name="rel_bias",
    )(idx, rel_table)


def _mod_body(c_ref, w_ref, b_ref, o_ref):
    c = c_ref[...]
    s = c * _sigmoid(c)
    o_ref[...] = jnp.dot(s.astype(bf16), w_ref[...].astype(bf16), preferred_element_type=f32) + b_ref[...]


def _modulation(c_all, w_ada, b_ada):
    rows = c_all.shape[0]
    blk = 512
    return pl.pallas_call(
        _mod_body,
        grid=(6 * D_MODEL // blk,),
        in_specs=[pl.BlockSpec((rows, D_MODEL), lambda j: (0, 0)),
                  pl.BlockSpec((D_MODEL, blk), lambda j: (0, j)),
                  pl.BlockSpec((1, blk), lambda j: (0, j))],
        out_specs=pl.BlockSpec((rows, blk), lambda j: (0, j)),
        out_shape=jax.ShapeDtypeStruct((rows, 6 * D_MODEL), f32),
        name="adaln_mod",
    )(c_all, w_ada, b_ada)


def _attn_chunk(q_c, kband, vband, bias_ref, sink_ref, valid, att_ref, row0):
    for k in range(KV_HEADS):
        qk = jnp.concatenate(
            [q_c[:, (GROUP * k + g) * HEAD_DIM:(GROUP * k + g + 1) * HEAD_DIM] for g in range(GROUP)], axis=0)
        kk = kband[:, k * HEAD_DIM:(k + 1) * HEAD_DIM]
        s = lax.dot_general(qk, kk, (((1,), (1,)), ((), ())), preferred_element_type=f32)
        s = s + bias_ref[k * GROUP * CHUNK:(k + 1) * GROUP * CHUNK, :]
        if valid is not None:
            s = jnp.where(valid, s, NEG)
        sk = jnp.concatenate(
            [jnp.full((CHUNK, 1), sink_ref[GROUP * k + g], f32) for g in range(GROUP)], axis=0)
        m = jnp.maximum(jnp.max(s, axis=-1, keepdims=True), sk)
        p = jnp.exp(s - m)
        den = jnp.sum(p, axis=-1, keepdims=True) + jnp.exp(sk - m)
        p = p / den
        o = jnp.dot(p.astype(bf16), vband[:, k * HEAD_DIM:(k + 1) * HEAD_DIM], preferred_element_type=f32)
        for g in range(GROUP):
            h = GROUP * k + g
            att_ref[pl.ds(row0, CHUNK), h * HEAD_DIM:(h + 1) * HEAD_DIM] = (
                o[g * CHUNK:(g + 1) * CHUNK].astype(att_ref.dtype))


def _conv_seq(zb_ref, s, dww_ref, cb_ref, out_row0, n_rows):
    for j in range(N_SLABS):
        def blk(i, c):
            r0 = pl.multiple_of(i * 32, 32)
            accs = [jnp.zeros((8, LANES), f32) for _ in range(4)]
            for k in range(CONV_W):
                wk = dww_ref[k:k + 1, j * LANES:(j + 1) * LANES]
                for a in range(4):
                    start = r0 + (a // 2) * 16 + (a % 2) + (HIST - (CONV_W - 1)) + k
                    accs[a] = accs[a] + zb_ref[s, j, pl.ds(start, 8, stride=2), :] * wk
            for a in range(4):
                start = out_row0 + r0 + (a // 2) * 16 + (a % 2)
                cb_ref[j, pl.ds(start, 8, stride=2), :] = accs[a]
            return c
        lax.fori_loop(0, n_rows // 32, blk, 0)


def _branch_body(*refs, is_prompt, nb, L, n_tiles):
    if is_prompt:
        (x_ref, mod_ref, bias_ref, sink_ref, g1_ref, win_ref, wao_ref, wco_ref, wo_ref,
         dww_ref, dwb_ref, lng_ref, lnb_ref,
         x1_ref, nk_ref, nv_ref, nc_ref,
         h_s, q_s, kb_s, vb_s, att_s, zb_s, cb_s, cbf_s, acc_s) = refs
        ck_ref = cv_ref = cc_ref = None
    else:
        (x_ref, mod_ref, ck_ref, cv_ref, cc_ref, bias_ref, sink_ref, g1_ref, win_ref, wao_ref, wco_ref, wo_ref,
         dww_ref, dwb_ref, lng_ref, lnb_ref,
         x1_ref, nk_ref, nv_ref, nc_ref,
         h_s, q_s, kb_s, vb_s, att_s, zb_s, cb_s, cbf_s, acc_s) = refs
    T = nb * L

    if is_prompt:
        t = pl.program_id(1)

        @pl.when(t == 0)
        def _():
            kb_s[0:WINDOW, :] = jnp.zeros((WINDOW, KV_W), bf16)
            vb_s[0:WINDOW, :] = jnp.zeros((WINDOW, KV_W), bf16)
            zb_s[0, :, 0:HIST, :] = jnp.zeros((N_SLABS, HIST, LANES), f32)

    for s in range(nb):
        def norm_rows(r0, s=s):
            x = x_ref[s, pl.ds(r0, ROW_BLK), :]
            hh = _rms(x, g1_ref[...]) * (1.0 + mod_ref[s, 1:2, :]) + mod_ref[s, 0:1, :]
            h_s[pl.ds(s * L + r0, ROW_BLK), :] = hh.astype(bf16)
        _for_rows(L, ROW_BLK, norm_rows)

    hv = h_s[...]

    for c in range(Q_W // COL_BLK):
        qf = _mm(hv, win_ref, c * COL_BLK, COL_BLK)
        q_s[:, c * COL_BLK:(c + 1) * COL_BLK] = (qf * (HEAD_DIM ** -0.5)).astype(bf16)
    kf = _mm(hv, win_ref, O_K, KV_W)
    vf = _mm(hv, win_ref, O_V, KV_W)
    if is_prompt:
        kb_s[WINDOW:WINDOW + T, :] = kf.astype(bf16)
        vb_s[WINDOW:WINDOW + T, :] = vf.astype(bf16)

        @pl.when(t == n_tiles - 1)
        def _():
            nk_ref[0] = kf[T - WINDOW:T]
            nv_ref[0] = vf[T - WINDOW:T]
    else:
        for s in range(nb):
            kb_s[s, 0:WINDOW, :] = ck_ref[s].astype(bf16)
            vb_s[s, 0:WINDOW, :] = cv_ref[s].astype(bf16)
            kb_s[s, WINDOW:BAND, :] = kf[s * L:(s + 1) * L].astype(bf16)
            vb_s[s, WINDOW:BAND, :] = vf[s * L:(s + 1) * L].astype(bf16)
            nk_ref[s, 0:WINDOW - L, :] = ck_ref[s, L:WINDOW, :]
            nv_ref[s, 0:WINDOW - L, :] = cv_ref[s, L:WINDOW, :]
            nk_ref[s, WINDOW - L:WINDOW, :] = kf[s * L:(s + 1) * L]
            nv_ref[s, WINDOW - L:WINDOW, :] = vf[s * L:(s + 1) * L]

    if is_prompt:
        def attn_step(c, carry):
            row0 = pl.multiple_of(c * CHUNK, CHUNK)
            pos = t * T + c * CHUNK - WINDOW + lax.broadcasted_iota(jnp.int32, (1, BAND), 1)
            q_c = q_s[pl.ds(row0, CHUNK), :]
            kband = kb_s[pl.ds(row0, BAND), :]
            vband = vb_s[pl.ds(row0, BAND), :]
            _attn_chunk(q_c, kband, vband, bias_ref, sink_ref, pos >= 0, att_s, row0)
            return carry
        lax.fori_loop(0, T // CHUNK, attn_step, 0)
    else:
        def attn_step(s, carry):
            row0 = pl.multiple_of(s * CHUNK, CHUNK)
            q_c = q_s[pl.ds(row0, CHUNK), :]
            _attn_chunk(q_c, kb_s[s], vb_s[s], bias_ref, sink_ref, None, att_s, row0)
            return carry
        lax.fori_loop(0, nb, attn_step, 0)

    av = att_s[...]
    for c in range(D_MODEL // COL_BLK):
        ya = _mm(av, wao_ref, c * COL_BLK, COL_BLK)
        ga = _sigmoid(_mm(hv, win_ref, O_GA + c * COL_BLK, COL_BLK))
        acc_s[:, c * COL_BLK:(c + 1) * COL_BLK] = ga * ya

    if not is_prompt:
        for s in range(nb):
            for j in range(N_SLABS):
                zb_s[s, j, HIST - (CONV_W - 1):HIST, :] = cc_ref[s, :, j * LANES:(j + 1) * LANES]
    for j in range(N_SLABS):
        za = _mm(hv, win_ref, O_GLU_A + j * LANES, LANES)
        zg = _mm(hv, win_ref, O_GLU_G + j * LANES, LANES)
        z = za * _sigmoid(zg)
        for s in range(nb):
            zb_s[s, j, HIST:HIST + L, :] = z[s * L:(s + 1) * L]
            if is_prompt:
                @pl.when(t == n_tiles - 1)
                def _():
                    nc_ref[0, :, j * LANES:(j + 1) * LANES] = z[L - (CONV_W - 1):L]
            else:
                nc_ref[s, :, j * LANES:(j + 1) * LANES] = z[s * L + L - (CONV_W - 1):(s + 1) * L]

    for s in range(nb):
        _conv_seq(zb_s, s, dww_ref, cb_s, s * L, L)

    def ln_rows(r0):
        ys = [cb_s[j, pl.ds(r0, ROW_BLK), :] + dwb_ref[:, j * LANES:(j + 1) * LANES] for j in range(N_SLABS)]
        tot = ys[0]
        for j in range(1, N_SLABS):
            tot = tot + ys[j]
        mu = jnp.sum(tot, axis=-1, keepdims=True) * (1.0 / D_MODEL)
        ds = [y - mu for y in ys]
        sq = ds[0] * ds[0]
        for j in range(1, N_SLABS):
            sq = sq + ds[j] * ds[j]
        var = jnp.sum(sq, axis=-1, keepdims=True) * (1.0 / D_MODEL)
        rstd = lax.rsqrt(var + EPS)
        for j in range(N_SLABS):
            y = ds[j] * rstd * lng_ref[:, j * LANES:(j + 1) * LANES] + lnb_ref[:, j * LANES:(j + 1) * LANES]
            cbf_s[pl.ds(r0, ROW_BLK), j * LANES:(j + 1) * LANES] = (y * _sigmoid(y)).astype(bf16)
    _for_rows(T, ROW_BLK, ln_rows)

    cv = cbf_s[...]
    for c in range(D_MODEL // COL_BLK):
        yb = _mm(cv, wco_ref, c * COL_BLK, COL_BLK)
        gb = _sigmoid(_mm(hv, win_ref, O_GB + c * COL_BLK, COL_BLK))
        acc_s[:, c * COL_BLK:(c + 1) * COL_BLK] += gb * yb

    mv = acc_s[...].astype(bf16)
    for c in range(D_MODEL // COL_BLK):
        m = _mm(mv, wo_ref, c * COL_BLK, COL_BLK)
        for s in range(nb):
            cols = slice(c * COL_BLK, (c + 1) * COL_BLK)
            x1_ref[s, :, cols] = x_ref[s, :, cols] + mod_ref[s, 2:3, cols] * m[s * L:(s + 1) * L]

    if is_prompt:
        kb_s[0:WINDOW, :] = kb_s[T:T + WINDOW, :]
        vb_s[0:WINDOW, :] = vb_s[T:T + WINDOW, :]
        zb_s[0, :, 0:HIST, :] = zb_s[0, :, L:L + HIST, :]


def _const_spec(shape):
    nd = len(shape)
    return pl.BlockSpec(shape, lambda *_: (0,) * nd, pipeline_mode=pl.Buffered(1))


def _branch_weights_specs():
    return [
        _const_spec((N_HEADS * CHUNK, BAND)),
        pl.BlockSpec(memory_space=pltpu.SMEM),
        _const_spec((1, D_MODEL)),
        _const_spec((D_MODEL, IN_W)),
        _const_spec((Q_W, D_MODEL)),
        _const_spec((D_MODEL, D_MODEL)),
        _const_spec((D_MODEL, D_MODEL)),
        _const_spec((CONV_W, D_MODEL)),
        _const_spec((1, D_MODEL)),
        _const_spec((1, D_MODEL)),
        _const_spec((1, D_MODEL)),
    ]


def _branch_scratch(nb, L, is_prompt):
    T = nb * L
    kv_shape = (WINDOW + T, KV_W) if is_prompt else (nb, BAND, KV_W)
    return [
        pltpu.VMEM((T, D_MODEL), bf16),
        pltpu.VMEM((T, Q_W), bf16),
        pltpu.VMEM(kv_shape, bf16),
        pltpu.VMEM(kv_shape, bf16),
        pltpu.VMEM((T, Q_W), bf16),
        pltpu.VMEM((nb, N_SLABS, HIST + L, LANES), f32),
        pltpu.VMEM((N_SLABS, T, LANES), f32),
        pltpu.VMEM((T, D_MODEL), bf16),
        pltpu.VMEM((T, D_MODEL), f32),
    ]


def _branch_prompt(x, mod, weights, tq):
    B, S, _ = x.shape
    n_tiles = S // tq
    body = functools.partial(_branch_body, is_prompt=True, nb=1, L=tq, n_tiles=n_tiles)
    return pl.pallas_call(
        body,
        grid=(B, n_tiles),
        in_specs=[pl.BlockSpec((1, tq, D_MODEL), lambda b, t: (b, t, 0)),
                  pl.BlockSpec((1, 6, D_MODEL), lambda b, t: (b, 0, 0))] + _branch_weights_specs(),
        out_specs=[pl.BlockSpec((1, tq, D_MODEL), lambda b, t: (b, t, 0)),
                   pl.BlockSpec((1, WINDOW, KV_W), lambda b, t: (b, 0, 0)),
                   pl.BlockSpec((1, WINDOW, KV_W), lambda b, t: (b, 0, 0)),
                   pl.BlockSpec((1, CONV_W - 1, D_MODEL), lambda b, t: (b, 0, 0))],
        out_shape=[jax.ShapeDtypeStruct((B, S, D_MODEL), f32),
                   jax.ShapeDtypeStruct((B, WINDOW, KV_W), f32),
                   jax.ShapeDtypeStruct((B, WINDOW, KV_W), f32),
                   jax.ShapeDtypeStruct((B, CONV_W - 1, D_MODEL), f32)],
        scratch_shapes=_branch_scratch(1, tq, True),
        compiler_params=pltpu.CompilerParams(
            dimension_semantics=("arbitrary", "arbitrary"), vmem_limit_bytes=VMEM_LIMIT),
        name="branch_prompt",
    )(x, mod, *weights)


def _branch_sample(x, mod, ck, cv, cc, weights, nb):
    B, L, _ = x.shape
    body = functools.partial(_branch_body, is_prompt=False, nb=nb, L=L, n_tiles=B // nb)
    seq3 = lambda i: (i, 0, 0)
    return pl.pallas_call(
        body,
        grid=(B // nb,),
        in_specs=[pl.BlockSpec((nb, L, D_MODEL), seq3),
                  pl.BlockSpec((nb, 6, D_MODEL), seq3),
                  pl.BlockSpec((nb, WINDOW, KV_W), seq3),
                  pl.BlockSpec((nb, WINDOW, KV_W), seq3),
                  pl.BlockSpec((nb, CONV_W - 1, D_MODEL), seq3)] + _branch_weights_specs(),
        out_specs=[pl.BlockSpec((nb, L, D_MODEL), seq3),
                   pl.BlockSpec((nb, WINDOW, KV_W), seq3),
                   pl.BlockSpec((nb, WINDOW, KV_W), seq3),
                   pl.BlockSpec((nb, CONV_W - 1, D_MODEL), seq3)],
        out_shape=[jax.ShapeDtypeStruct((B, L, D_MODEL), f32),
                   jax.ShapeDtypeStruct((B, WINDOW, KV_W), f32),
                   jax.ShapeDtypeStruct((B, WINDOW, KV_W), f32),
                   jax.ShapeDtypeStruct((B, CONV_W - 1, D_MODEL), f32)],
        scratch_shapes=_branch_scratch(nb, L, False),
        compiler_params=pltpu.CompilerParams(
            dimension_semantics=("arbitrary",), vmem_limit_bytes=VMEM_LIMIT),
        name="branch_sample",
    )(x, mod, ck, cv, cc, *weights)


def _ffn_body(x1_ref, mod_ref, g2_ref, fg_ref, wup_ref, wdn_ref, y_ref, h_s, act_s, *, nb, L):
    T = nb * L
    for s in range(nb):
        def norm_rows(r0, s=s):
            x = x1_ref[s, pl.ds(r0, ROW_BLK), :]
            hh = _rms(x, g2_ref[...]) * (1.0 + mod_ref[s, 4:5, :]) + mod_ref[s, 3:4, :]
            h_s[pl.ds(s * L + r0, ROW_BLK), :] = hh.astype(bf16)
        _for_rows(L, ROW_BLK, norm_rows)
    hv = h_s[...]
    for c in range(D_FF // FF_BLK):
        gate = _mm(hv, wup_ref, c * FF_BLK, FF_BLK)
        up = _mm(hv, wup_ref, D_FF + c * FF_BLK, FF_BLK)
        act_s[:, c * FF_BLK:(c + 1) * FF_BLK] = (gate * _sigmoid(gate) * up).astype(bf16)
    av = act_s[...]
    for c in range(D_MODEL // COL_BLK):
        d = _mm(av, wdn_ref, c * COL_BLK, COL_BLK)
        for s in range(nb):
            cols = slice(c * COL_BLK, (c + 1) * COL_BLK)
            y_ref[s, :, cols] = x1_ref[s, :, cols] + mod_ref[s, 5:6, cols] * d[s * L:(s + 1) * L]
    for s in range(nb):
        def fin_rows(r0, s=s):
            x2 = y_ref[s, pl.ds(r0, ROW_BLK), :]
            y_ref[s, pl.ds(r0, ROW_BLK), :] = _rms(x2, fg_ref[...])
        _for_rows(L, ROW_BLK, fin_rows)


def _ffn(x1, mod, g2, fg, wup, wdn, nb, L, name):
    B, S, _ = x1.shape
    n_tiles = S // L
    if nb == 1:
        grid = (B, n_tiles)
        xmap = lambda b, t: (b, t, 0)
        mmap = lambda b, t: (b, 0, 0)
    else:
        grid = (B // nb,)
        xmap = mmap = lambda i: (i, 0, 0)
    body = functools.partial(_ffn_body, nb=nb, L=L)
    T = nb * L
    return pl.pallas_call(
        body,
        grid=grid,
        in_specs=[pl.BlockSpec((nb, L, D_MODEL), xmap),
                  pl.BlockSpec((nb, 6, D_MODEL), mmap),
                  _const_spec((1, D_MODEL)),
                  _const_spec((1, D_MODEL)),
                  _const_spec((D_MODEL, 2 * D_FF)),
                  _const_spec((D_FF, D_MODEL))],
        out_specs=pl.BlockSpec((nb, L, D_MODEL), xmap),
        out_shape=jax.ShapeDtypeStruct(x1.shape, f32),
        scratch_shapes=[pltpu.VMEM((T, D_MODEL), bf16), pltpu.VMEM((T, D_FF), bf16)],
        compiler_params=pltpu.CompilerParams(
            dimension_semantics=("arbitrary",) * len(grid), vmem_limit_bytes=VMEM_LIMIT),
        name=name,
    )(x1, mod, g2, fg, wup, wdn)


def kernel(x_prompt, x_sample, cache_k, cache_v, cache_conv, c_prompt, c_sample, rel_table, w_ada, b_ada, norm1_g, norm2_g, w_in, sink, w_attn_out, dw_w, dw_b, conv_ln_g, conv_ln_b, w_conv_out, w_out, w_ffn_up, w_ffn_down, final_g):
    assert w_ada.shape[0] == 1, "single-layer kernel"
    B, S, _ = x_prompt.shape
    DB, DS, _ = x_sample.shape
    TQ = 512
    NB = 8

    bias = _bias_table(rel_table)
    n_seq = B + DB
    pad = (-n_seq) % 8
    c_all = jnp.concatenate([c_prompt, c_sample, jnp.zeros((pad, D_MODEL), f32)], axis=0)
    mod = _modulation(c_all, w_ada[0], b_ada).reshape(n_seq + pad, 6, D_MODEL)
    mod_p, mod_s = mod[:B], mod[B:B + DB]

    row = lambda v: v.reshape(1, D_MODEL)
    weights = (bias, sink[0], row(norm1_g[0]), w_in[0].astype(bf16), w_attn_out[0].astype(bf16),
               w_conv_out[0].astype(bf16), w_out[0].astype(bf16), dw_w[0], row(dw_b[0]),
               row(conv_ln_g[0]), row(conv_ln_b[0]))
    wup = w_ffn_up[0].astype(bf16)
    wdn = w_ffn_down[0].astype(bf16)
    g2 = row(norm2_g[0])
    fg = row(final_g)

    x1p, nkp, nvp, ncp = _branch_prompt(x_prompt, mod_p, weights, TQ)
    ck = cache_k[0].reshape(DB, WINDOW, KV_W)
    cv = cache_v[0].reshape(DB, WINDOW, KV_W)
    x1s, nks, nvs, ncs = _branch_sample(x_sample, mod_s, ck, cv, cache_conv[0], weights, NB)

    y_p = _ffn(x1p, mod_p, g2, fg, wup, wdn, 1, TQ, "ffn_prompt")
    y_s = _ffn(x1s, mod_s, g2, fg, wup, wdn, NB, DS, "ffn_sample")

    kv5 = lambda a: a.reshape(1, a.shape[0], WINDOW, KV_HEADS, HEAD_DIM)
    return (y_p, y_s, kv5(nkp), kv5(nvp), ncp[None], kv5(nks), kv5(nvs), ncs[None])
```

```python
import functools
import math

import jax
import jax.numpy as jnp
from jax import lax
from jax.experimental import pallas as pl
from jax.experimental.pallas import tpu as pltpu

f32 = jnp.float32
bf16 = jnp.bfloat16

D_MODEL = 1024
N_HEADS = 16
KV_HEADS = 4
HEAD_DIM = 64
GROUP = N_HEADS // KV_HEADS
WINDOW = 128
CHUNK = 64
BAND = WINDOW + CHUNK
CONV_W = 31
D_FF = 2816
NUM_BUCKETS = 32
MAX_DISTANCE = 128
EPS = 1e-6
NEG = -1e30
Q_W = N_HEADS * HEAD_DIM
KV_W = KV_HEADS * HEAD_DIM
IN_W = Q_W + 2 * KV_W + 4 * D_MODEL
O_K = Q_W
O_V = O_K + KV_W
O_GLU_A = O_V + KV_W
O_GLU_G = O_GLU_A + D_MODEL
O_GA = O_GLU_G + D_MODEL
O_GB = O_GA + D_MODEL

LANES = 128
N_SLABS = D_MODEL // LANES
PAIR = 2 * CHUNK
GQ = GROUP * CHUNK
HIST = 32
ROW_BLK = 32
COL_BLK = 256
FF_BLK = 256
VMEM_LIMIT = 56 * 1024 * 1024

_NT = (((1,), (1,)), ((), ()))


def _sigmoid(x):
    return 1.0 / (1.0 + jnp.exp(-x))


def _rms(x, g):
    ms = jnp.mean(x * x, axis=-1, keepdims=True)
    return x * lax.rsqrt(ms + EPS) * g


def _for_rows(n_rows, blk, body):
    def step(i, c):
        body(pl.multiple_of(i * blk, blk))
        return c
    lax.fori_loop(0, n_rows // blk, step, 0)


def _mm(a, w_ref, c0, width):
    return jnp.dot(a, w_ref[:, c0:c0 + width], preferred_element_type=f32)


def _head_row(vals, k, lane):
    row = jnp.full((1, GQ), vals(GROUP * k + GROUP - 1), f32)
    for g in range(GROUP - 2, -1, -1):
        row = jnp.where(lane < CHUNK * (g + 1), vals(GROUP * k + g), row)
    return row


def _rel_bucket(rel):
    nb = NUM_BUCKETS // 2
    max_exact = nb // 2
    ret = (rel > 0).astype(jnp.int32) * nb
    n = jnp.abs(rel)
    nf = jnp.maximum(n, 1).astype(f32)
    large = max_exact + (jnp.log(nf / max_exact) / math.log(MAX_DISTANCE / max_exact)
                         * (nb - max_exact)).astype(jnp.int32)
    large = jnp.minimum(large, nb - 1)
    return ret + jnp.where(n < max_exact, n, large)


def _bias_body(idx_ref, tab_ref, o_ref):
    idx = idx_ref[...]
    lane = lax.broadcasted_iota(jnp.int32, (1, GQ), 1)
    for k in range(KV_HEADS):
        acc = jnp.zeros((BAND, GQ), f32)
        for b in range(NUM_BUCKETS):
            acc = jnp.where(idx == b, _head_row(lambda h: tab_ref[b, h], k, lane), acc)
        o_ref[k] = acc


def _bias_table(rel_table):
    kj = jnp.arange(BAND, dtype=jnp.int32)
    rel = kj[None, :] - WINDOW - jnp.arange(CHUNK, dtype=jnp.int32)[:, None]
    idx_t = jnp.tile(_rel_bucket(rel).T, (1, GROUP))
    return pl.pallas_call(
        _bias_body,
        in_specs=[pl.BlockSpec(memory_space=pltpu.VMEM), pl.BlockSpec(memory_space=pltpu.SMEM)],
        out_specs=pl.BlockSpec(memory_space=pltpu.VMEM),
        out_shape=jax.ShapeDtypeStruct((KV_HEADS, BAND, GQ), f32),
        name="rel_bias",
    )(idx_t, rel_table)


def _mod_body(c_ref, w_ref, b_ref, o_ref):
    c = c_ref[...]
    s = c * _sigmoid(c)
    o_ref[...] = jnp.dot(s.astype(bf16), w_ref[...].astype(bf16), preferred_element_type=f32) + b_ref[...]


def _modulation(c_all, w_ada, b_ada):
    rows = c_all.shape[0]
    blk = 512
    return pl.pallas_call(
        _mod_body,
        grid=(6 * D_MODEL // blk,),
        in_specs=[pl.BlockSpec((rows, D_MODEL), lambda j: (0, 0)),
                  pl.BlockSpec((D_MODEL, blk), lambda j: (0, j)),
                  pl.BlockSpec((1, blk), lambda j: (0, j))],
        out_specs=pl.BlockSpec((rows, blk), lambda j: (0, j)),
        out_shape=jax.ShapeDtypeStruct((rows, 6 * D_MODEL), f32),
        name="adaln_mod",
    )(c_all, w_ada, b_ada)


def _attn_pair(chunks, bias_ref, sink_rows, attT_ref, grp):
    lane_lo = lax.broadcasted_iota(jnp.int32, (CHUNK, LANES), 1) < HEAD_DIM
    zpad = jnp.zeros((2 * LANES - BAND, GQ), bf16)
    for par, (q_c, kd_bands, v_win, pad_top, valid) in enumerate(chunks):
        for k in range(KV_HEADS):
            blocks = []
            for p in range(2):
                qc = q_c[:, (2 * k + p) * LANES:(2 * k + p + 1) * LANES]
                blocks.append(jnp.where(lane_lo, qc, jnp.zeros_like(qc)))
                blocks.append(jnp.where(lane_lo, jnp.zeros_like(qc), qc))
            qm = jnp.concatenate(blocks, axis=0)
            st = lax.dot_general(kd_bands[k], qm, _NT, preferred_element_type=f32)
            st = st + bias_ref[k]
            if valid is not None:
                st = jnp.where(valid, st, NEG)
            m = jnp.maximum(jnp.max(st, axis=0, keepdims=True), sink_rows[k])
            p_ = jnp.exp(st - m)
            den = jnp.sum(p_, axis=0, keepdims=True) + jnp.exp(sink_rows[k] - m)
            pb = (p_ / den).astype(bf16)
            rhs = jnp.concatenate([zpad, pb] if pad_top else [pb, zpad], axis=0)
            ot = jnp.dot(v_win[k * HEAD_DIM:(k + 1) * HEAD_DIM, :], rhs, preferred_element_type=f32)
            for g in range(GROUP):
                h = GROUP * k + g
                attT_ref[grp, h * HEAD_DIM:(h + 1) * HEAD_DIM, par * CHUNK:(par + 1) * CHUNK] = (
                    ot[:, g * CHUNK:(g + 1) * CHUNK])


def _conv_seq(zb_ref, s, dww_ref, cb_ref, out_row0, n_rows):
    for j in range(N_SLABS):
        def blk(i, c):
            r0 = pl.multiple_of(i * 32, 32)
            accs = [jnp.zeros((8, LANES), f32) for _ in range(4)]
            for k in range(CONV_W):
                wk = dww_ref[k:k + 1, j * LANES:(j + 1) * LANES]
                for a in range(4):
                    start = r0 + (a // 2) * 16 + (a % 2) + (HIST - (CONV_W - 1)) + k
                    accs[a] = accs[a] + zb_ref[s, j, pl.ds(start, 8, stride=2), :] * wk
            for a in range(4):
                start = out_row0 + r0 + (a // 2) * 16 + (a % 2)
                cb_ref[j, pl.ds(start, 8, stride=2), :] = accs[a]
            return c
        lax.fori_loop(0, n_rows // 32, blk, 0)


def _branch_body(*refs, is_prompt, nb, L, n_tiles):
    if is_prompt:
        (x_ref, mod_ref, bias_ref, sink_ref, g1_ref, win_ref, wkd_ref, wvt_ref, wao_ref, wco_ref, wo_ref,
         dww_ref, dwb_ref, lng_ref, lnb_ref,
         x1_ref, nk_ref, nv_ref, nc_ref,
         h_s, q_s, kd_s, vt_s, attT_s, att_s, zb_s, cb_s, cbf_s, acc_s) = refs
        ck_ref = cv_ref = cc_ref = None
    else:
        (x_ref, mod_ref, ck_ref, cv_ref, cc_ref, bias_ref, sink_ref, g1_ref, win_ref, wkd_ref, wvt_ref,
         wao_ref, wco_ref, wo_ref, dww_ref, dwb_ref, lng_ref, lnb_ref,
         x1_ref, nk_ref, nv_ref, nc_ref,
         h_s, q_s, kd_s, vt_s, attT_s, att_s, zb_s, cb_s, cbf_s, acc_s) = refs
    T = nb * L
    n_grp = T // PAIR

    if is_prompt:
        t = pl.program_id(1)

        @pl.when(t == 0)
        def _():
            kd_s[:, 0:WINDOW, :] = jnp.zeros((KV_HEADS, WINDOW, LANES), bf16)
            vt_s[0] = jnp.zeros((KV_W, LANES), bf16)
            zb_s[0, :, 0:HIST, :] = jnp.zeros((N_SLABS, HIST, LANES), f32)

    for s in range(nb):
        def norm_rows(r0, s=s):
            x = x_ref[s, pl.ds(r0, ROW_BLK), :]
            hh = _rms(x, g1_ref[...]) * (1.0 + mod_ref[s, 1:2, :]) + mod_ref[s, 0:1, :]
            h_s[pl.ds(s * L + r0, ROW_BLK), :] = hh.astype(bf16)
        _for_rows(L, ROW_BLK, norm_rows)

    hv = h_s[...]

    for c in range(Q_W // COL_BLK):
        qf = _mm(hv, win_ref, c * COL_BLK, COL_BLK)
        q_s[:, c * COL_BLK:(c + 1) * COL_BLK] = (qf * (HEAD_DIM ** -0.5)).astype(bf16)
    kdf = _mm(hv, wkd_ref, 0, KV_HEADS * LANES)
    vtf = lax.dot_general(wvt_ref[...], hv, _NT, preferred_element_type=f32)
    if is_prompt:
        for k in range(KV_HEADS):
            kd_s[k, WINDOW:WINDOW + T, :] = kdf[:, k * LANES:(k + 1) * LANES].astype(bf16)
        for g in range(n_grp):
            vt_s[1 + g] = vtf[:, g * LANES:(g + 1) * LANES].astype(bf16)

        @pl.when(t == n_tiles - 1)
        def _():
            tail = h_s[T - WINDOW:T, :]
            nk_ref[0] = _mm(tail, win_ref, O_K, KV_W)
            nv_ref[0] = _mm(tail, win_ref, O_V, KV_W)
    else:
        kf = _mm(hv, win_ref, O_K, KV_W)
        vf = _mm(hv, win_ref, O_V, KV_W)
        for s in range(nb):
            ck = ck_ref[s]
            for k in range(KV_HEADS):
                ckk = ck[:, k * HEAD_DIM:(k + 1) * HEAD_DIM].astype(bf16)
                kd_s[s, k, 0:WINDOW, :] = jnp.concatenate([ckk, ckk], axis=1)
                kd_s[s, k, WINDOW:BAND, :] = kdf[s * L:(s + 1) * L, k * LANES:(k + 1) * LANES].astype(bf16)
            vt_s[s, :, 0:WINDOW] = cv_ref[s].T.astype(bf16)
            vt_s[s, :, WINDOW:BAND] = vtf[:, s * L:(s + 1) * L].astype(bf16)
            vt_s[s, :, BAND:2 * LANES] = jnp.zeros((KV_W, 2 * LANES - BAND), bf16)
            nk_ref[s, 0:WINDOW - L, :] = ck[L:WINDOW, :]
            nv_ref[s, 0:WINDOW - L, :] = cv_ref[s, L:WINDOW, :]
            nk_ref[s, WINDOW - L:WINDOW, :] = kf[s * L:(s + 1) * L]
            nv_ref[s, WINDOW - L:WINDOW, :] = vf[s * L:(s + 1) * L]

    lane_gq = lax.broadcasted_iota(jnp.int32, (1, GQ), 1)
    sink_rows = [_head_row(lambda h: sink_ref[h], k, lane_gq) for k in range(KV_HEADS)]
    if is_prompt:
        def attn_step(c2, carry):
            row0 = pl.multiple_of(c2 * PAIR, PAIR)
            v_win = jnp.concatenate([vt_s[c2], vt_s[c2 + 1]], axis=1)
            chunks = []
            for par in range(2):
                r = pl.multiple_of(row0 + par * CHUNK, CHUNK)
                pos = t * T + r - WINDOW + lax.broadcasted_iota(jnp.int32, (BAND, 1), 0)
                chunks.append((q_s[pl.ds(r, CHUNK), :],
                               [kd_s[k, pl.ds(r, BAND), :] for k in range(KV_HEADS)],
                               v_win, par == 1, pos >= 0))
            _attn_pair(chunks, bias_ref, sink_rows, attT_s, c2)
            return carry
    else:
        def attn_step(s2, carry):
            chunks = []
            for par in range(2):
                s = 2 * s2 + par
                r = pl.multiple_of(s * CHUNK, CHUNK)
                chunks.append((q_s[pl.ds(r, CHUNK), :],
                               [kd_s[s, k] for k in range(KV_HEADS)],
                               vt_s[s], False, None))
            _attn_pair(chunks, bias_ref, sink_rows, attT_s, s2)
            return carry
    lax.fori_loop(0, n_grp, attn_step, 0)
    for g in range(n_grp):
        att_s[g * PAIR:(g + 1) * PAIR, :] = attT_s[g].T.astype(bf16)

    av = att_s[...]
    for c in range(D_MODEL // COL_BLK):
        ya = _mm(av, wao_ref, c * COL_BLK, COL_BLK)
        ga = _sigmoid(_mm(hv, win_ref, O_GA + c * COL_BLK, COL_BLK))
        acc_s[:, c * COL_BLK:(c + 1) * COL_BLK] = ga * ya

    if not is_prompt:
        for s in range(nb):
            for j in range(N_SLABS):
                zb_s[s, j, HIST - (CONV_W - 1):HIST, :] = cc_ref[s, :, j * LANES:(j + 1) * LANES]
    for j in range(N_SLABS):
        za = _mm(hv, win_ref, O_GLU_A + j * LANES, LANES)
        zg = _mm(hv, win_ref, O_GLU_G + j * LANES, LANES)
        z = za * _sigmoid(zg)
        for s in range(nb):
            zb_s[s, j, HIST:HIST + L, :] = z[s * L:(s + 1) * L]
            if is_prompt:
                @pl.when(t == n_tiles - 1)
                def _():
                    nc_ref[0, :, j * LANES:(j + 1) * LANES] = z[L - (CONV_W - 1):L]
            else:
                nc_ref[s, :, j * LANES:(j + 1) * LANES] = z[s * L + L - (CONV_W - 1):(s + 1) * L]

    for s in range(nb):
        _conv_seq(zb_s, s, dww_ref, cb_s, s * L, L)

    def ln_rows(r0):
        ys = [cb_s[j, pl.ds(r0, ROW_BLK), :] + dwb_ref[:, j * LANES:(j + 1) * LANES] for j in range(N_SLABS)]
        tot = ys[0]
        for j in range(1, N_SLABS):
            tot = tot + ys[j]
        mu = jnp.sum(tot, axis=-1, keepdims=True) * (1.0 / D_MODEL)
        ds = [y - mu for y in ys]
        sq = ds[0] * ds[0]
        for j in range(1, N_SLABS):
            sq = sq + ds[j] * ds[j]
        var = jnp.sum(sq, axis=-1, keepdims=True) * (1.0 / D_MODEL)
        rstd = lax.rsqrt(var + EPS)
        for j in range(N_SLABS):
            y = ds[j] * rstd * lng_ref[:, j * LANES:(j + 1) * LANES] + lnb_ref[:, j * LANES:(j + 1) * LANES]
            cbf_s[pl.ds(r0, ROW_BLK), j * LANES:(j + 1) * LANES] = (y * _sigmoid(y)).astype(bf16)
    _for_rows(T, ROW_BLK, ln_rows)

    cv = cbf_s[...]
    for c in range(D_MODEL // COL_BLK):
        yb = _mm(cv, wco_ref, c * COL_BLK, COL_BLK)
        gb = _sigmoid(_mm(hv, win_ref, O_GB + c * COL_BLK, COL_BLK))
        acc_s[:, c * COL_BLK:(c + 1) * COL_BLK] += gb * yb

    mv = acc_s[...].astype(bf16)
    for c in range(D_MODEL // COL_BLK):
        m = _mm(mv, wo_ref, c * COL_BLK, COL_BLK)
        for s in range(nb):
            cols = slice(c * COL_BLK, (c + 1) * COL_BLK)
            x1_ref[s, :, cols] = x_ref[s, :, cols] + mod_ref[s, 2:3, cols] * m[s * L:(s + 1) * L]

    if is_prompt:
        kd_s[:, 0:WINDOW, :] = kd_s[:, T:T + WINDOW, :]
        vt_s[0] = vt_s[n_grp]
        zb_s[0, :, 0:HIST, :] = zb_s[0, :, L:L + HIST, :]


def _const_spec(shape):
    nd = len(shape)
    return pl.BlockSpec(shape, lambda *_: (0,) * nd, pipeline_mode=pl.Buffered(1))


def _branch_weights_specs():
    return [
        _const_spec((KV_HEADS, BAND, GQ)),
        pl.BlockSpec(memory_space=pltpu.SMEM),
        _const_spec((1, D_MODEL)),
        _const_spec((D_MODEL, IN_W)),
        _const_spec((D_MODEL, KV_HEADS * LANES)),
        _const_spec((KV_W, D_MODEL)),
        _const_spec((Q_W, D_MODEL)),
        _const_spec((D_MODEL, D_MODEL)),
        _const_spec((D_MODEL, D_MODEL)),
        _const_spec((CONV_W, D_MODEL)),
        _const_spec((1, D_MODEL)),
        _const_spec((1, D_MODEL)),
        _const_spec((1, D_MODEL)),
    ]


def _branch_scratch(nb, L, is_prompt):
    T = nb * L
    if is_prompt:
        kd_shape = (KV_HEADS, WINDOW + T, LANES)
        vt_shape = (1 + T // PAIR, KV_W, LANES)
    else:
        kd_shape = (nb, KV_HEADS, BAND, LANES)
        vt_shape = (nb, KV_W, 2 * LANES)
    return [
        pltpu.VMEM((T, D_MODEL), bf16),
        pltpu.VMEM((T, Q_W), bf16),
        pltpu.VMEM(kd_shape, bf16),
        pltpu.VMEM(vt_shape, bf16),
        pltpu.VMEM((T // PAIR, Q_W, PAIR), f32),
        pltpu.VMEM((T, Q_W), bf16),
        pltpu.VMEM((nb, N_SLABS, HIST + L, LANES), f32),
        pltpu.VMEM((N_SLABS, T, LANES), f32),
        pltpu.VMEM((T, D_MODEL), bf16),
        pltpu.VMEM((T, D_MODEL), f32),
    ]


def _branch_prompt(x, mod, weights, tq):
    B, S, _ = x.shape
    n_tiles = S // tq
    body = functools.partial(_branch_body, is_prompt=True, nb=1, L=tq, n_tiles=n_tiles)
    return pl.pallas_call(
        body,
        grid=(B, n_tiles),
        in_specs=[pl.BlockSpec((1, tq, D_MODEL), lambda b, t: (b, t, 0)),
                  pl.BlockSpec((1, 6, D_MODEL), lambda b, t: (b, 0, 0))] + _branch_weights_specs(),
        out_specs=[pl.BlockSpec((1, tq, D_MODEL), lambda b, t: (b, t, 0)),
                   pl.BlockSpec((1, WINDOW, KV_W), lambda b, t: (b, 0, 0)),
                   pl.BlockSpec((1, WINDOW, KV_W), lambda b, t: (b, 0, 0)),
                   pl.BlockSpec((1, CONV_W - 1, D_MODEL), lambda b, t: (b, 0, 0))],
        out_shape=[jax.ShapeDtypeStruct((B, S, D_MODEL), f32),
                   jax.ShapeDtypeStruct((B, WINDOW, KV_W), f32),
                   jax.ShapeDtypeStruct((B, WINDOW, KV_W), f32),
                   jax.ShapeDtypeStruct((B, CONV_W - 1, D_MODEL), f32)],
        scratch_shapes=_branch_scratch(1, tq, True),
        compiler_params=pltpu.CompilerParams(
            dimension_semantics=("arbitrary", "arbitrary"), vmem_limit_bytes=VMEM_LIMIT),
        name="branch_prompt",
    )(x, mod, *weights)


def _branch_sample(x, mod, ck, cv, cc, weights, nb):
    B, L, _ = x.shape
    body = functools.partial(_branch_body, is_prompt=False, nb=nb, L=L, n_tiles=B // nb)
    seq3 = lambda i: (i, 0, 0)
    return pl.pallas_call(
        body,
        grid=(B // nb,),
        in_specs=[pl.BlockSpec((nb, L, D_MODEL), seq3),
                  pl.BlockSpec((nb, 6, D_MODEL), seq3),
                  pl.BlockSpec((nb, WINDOW, KV_W), seq3),
                  pl.BlockSpec((nb, WINDOW, KV_W), seq3),
                  pl.BlockSpec((nb, CONV_W - 1, D_MODEL), seq3)] + _branch_weights_specs(),
        out_specs=[pl.BlockSpec((nb, L, D_MODEL), seq3),
                   pl.BlockSpec((nb, WINDOW, KV_W), seq3),
                   pl.BlockSpec((nb, WINDOW, KV_W), seq3),
                   pl.BlockSpec((nb, CONV_W - 1, D_MODEL), seq3)],
        out_shape=[jax.ShapeDtypeStruct((B, L, D_MODEL), f32),
                   jax.ShapeDtypeStruct((B, WINDOW, KV_W), f32),
                   jax.ShapeDtypeStruct((B, WINDOW, KV_W), f32),
                   jax.ShapeDtypeStruct((B, CONV_W - 1, D_MODEL), f32)],
        scratch_shapes=_branch_scratch(nb, L, False),
        compiler_params=pltpu.CompilerParams(
            dimension_semantics=("arbitrary",), vmem_limit_bytes=VMEM_LIMIT),
        name="branch_sample",
    )(x, mod, ck, cv, cc, *weights)


def _ffn_body(x1_ref, mod_ref, g2_ref, fg_ref, wup_ref, wdn_ref, y_ref, h_s, act_s, *, nb, L):
    T = nb * L
    for s in range(nb):
        def norm_rows(r0, s=s):
            x = x1_ref[s, pl.ds(r0, ROW_BLK), :]
            hh = _rms(x, g2_ref[...]) * (1.0 + mod_ref[s, 4:5, :]) + mod_ref[s, 3:4, :]
            h_s[pl.ds(s * L + r0, ROW_BLK), :] = hh.astype(bf16)
        _for_rows(L, ROW_BLK, norm_rows)
    hv = h_s[...]
    for c in range(D_FF // FF_BLK):
        gate = _mm(hv, wup_ref, c * FF_BLK, FF_BLK)
        up = _mm(hv, wup_ref, D_FF + c * FF_BLK, FF_BLK)
        act_s[:, c * FF_BLK:(c + 1) * FF_BLK] = (gate * _sigmoid(gate) * up).astype(bf16)
    av = act_s[...]
    for c in range(D_MODEL // COL_BLK):
        d = _mm(av, wdn_ref, c * COL_BLK, COL_BLK)
        for s in range(nb):
            cols = slice(c * COL_BLK, (c + 1) * COL_BLK)
            y_ref[s, :, cols] = x1_ref[s, :, cols] + mod_ref[s, 5:6, cols] * d[s * L:(s + 1) * L]
    for s in range(nb):
        def fin_rows(r0, s=s):
            x2 = y_ref[s, pl.ds(r0, ROW_BLK), :]
            y_ref[s, pl.ds(r0, ROW_BLK), :] = _rms(x2, fg_ref[...])
        _for_rows(L, ROW_BLK, fin_rows)


def _ffn(x1, mod, g2, fg, wup, wdn, nb, L, name):
    B, S, _ = x1.shape
    n_tiles = S // L
    if nb == 1:
        grid = (B, n_tiles)
        xmap = lambda b, t: (b, t, 0)
        mmap = lambda b, t: (b, 0, 0)
    else:
        grid = (B // nb,)
        xmap = mmap = lambda i: (i, 0, 0)
    body = functools.partial(_ffn_body, nb=nb, L=L)
    T = nb * L
    return pl.pallas_call(
        body,
        grid=grid,
        in_specs=[pl.BlockSpec((nb, L, D_MODEL), xmap),
                  pl.BlockSpec((nb, 6, D_MODEL), mmap),
                  _const_spec((1, D_MODEL)),
                  _const_spec((1, D_MODEL)),
                  _const_spec((D_MODEL, 2 * D_FF)),
                  _const_spec((D_FF, D_MODEL))],
        out_specs=pl.BlockSpec((nb, L, D_MODEL), xmap),
        out_shape=jax.ShapeDtypeStruct(x1.shape, f32),
        scratch_shapes=[pltpu.VMEM((T, D_MODEL), bf16), pltpu.VMEM((T, D_FF), bf16)],
        compiler_params=pltpu.CompilerParams(
            dimension_semantics=("arbitrary",) * len(grid), vmem_limit_bytes=VMEM_LIMIT),
        name=name,
    )(x1, mod, g2, fg, wup, wdn)


def kernel(x_prompt, x_sample, cache_k, cache_v, cache_conv, c_prompt, c_sample, rel_table, w_ada, b_ada, norm1_g, norm2_g, w_in, sink, w_attn_out, dw_w, dw_b, conv_ln_g, conv_ln_b, w_conv_out, w_out, w_ffn_up, w_ffn_down, final_g):
    assert w_ada.shape[0] == 1, "single-layer kernel"
    B, S, _ = x_prompt.shape
    DB, DS, _ = x_sample.shape
    TQ = 512
    NB = 4

    bias = _bias_table(rel_table)
    n_seq = B + DB
    pad = (-n_seq) % 8
    c_all = jnp.concatenate([c_prompt, c_sample, jnp.zeros((pad, D_MODEL), f32)], axis=0)
    mod = _modulation(c_all, w_ada[0], b_ada).reshape(n_seq + pad, 6, D_MODEL)
    mod_p, mod_s = mod[:B], mod[B:B + DB]

    row = lambda v: v.reshape(1, D_MODEL)
    w_in_b = w_in[0].astype(bf16)
    w_k = w_in_b[:, O_K:O_K + KV_W].reshape(D_MODEL, KV_HEADS, 1, HEAD_DIM)
    w_kd = jnp.broadcast_to(w_k, (D_MODEL, KV_HEADS, 2, HEAD_DIM)).reshape(D_MODEL, KV_HEADS * LANES)
    w_vt = w_in_b[:, O_V:O_V + KV_W].T
    weights = (bias, sink[0], row(norm1_g[0]), w_in_b, w_kd, w_vt, w_attn_out[0].astype(bf16),
               w_conv_out[0].astype(bf16), w_out[0].astype(bf16), dw_w[0], row(dw_b[0]),
               row(conv_ln_g[0]), row(conv_ln_b[0]))
    wup = w_ffn_up[0].astype(bf16)
    wdn = w_ffn_down[0].astype(bf16)
    g2 = row(norm2_g[0])
    fg = row(final_g)

    x1p, nkp, nvp, ncp = _branch_prompt(x_prompt, mod_p, weights, TQ)
    ck = cache_k[0].reshape(DB, WINDOW, KV_W)
    cv = cache_v[0].reshape(DB, WINDOW, KV_W)
    x1s, nks, nvs, ncs = _branch_sample(x_sample, mod_s, ck, cv, cache_conv[0], weights, NB)

    y_p = _ffn(x1p, mod_p, g2, fg, wup, wdn, 1, TQ, "ffn_prompt")
    y_s = _ffn(x1s, mod_s, g2, fg, wup, wdn, NB, DS, "ffn_sample")

    kv5 = lambda a: a.reshape(1, a.shape[0], WINDOW, KV_HEADS, HEAD_DIM)
    return (y_p, y_s, kv5(nkp), kv5(nvp), ncp[None], kv5(nks), kv5(nvs), ncs[None])
```

```python
import functools
import math

import jax
import jax.numpy as jnp
from jax import lax
from jax.experimental import pallas as pl
from jax.experimental.pallas import tpu as pltpu

f32 = jnp.float32
bf16 = jnp.bfloat16

D_MODEL = 1024
N_HEADS = 16
KV_HEADS = 4
HEAD_DIM = 64
GROUP = N_HEADS // KV_HEADS
WINDOW = 128
CHUNK = 64
BAND = WINDOW + CHUNK
CONV_W = 31
D_FF = 2816
NUM_BUCKETS = 32
MAX_DISTANCE = 128
EPS = 1e-6
NEG = -1e30
Q_W = N_HEADS * HEAD_DIM
KV_W = KV_HEADS * HEAD_DIM
IN_W = Q_W + 2 * KV_W + 4 * D_MODEL
O_K = Q_W
O_V = O_K + KV_W
O_GLU_A = O_V + KV_W
O_GLU_G = O_GLU_A + D_MODEL
O_GA = O_GLU_G + D_MODEL
O_GB = O_GA + D_MODEL

LANES = 128
N_SLABS = D_MODEL // LANES
PAIR = 2 * CHUNK
GQ = GROUP * CHUNK
HIST = 32
ROW_BLK = 32
COL_BLK = 256
FF_BLK = 256
VMEM_LIMIT = 56 * 1024 * 1024

_NT = (((1,), (1,)), ((), ()))


def _sigmoid(x):
    return 1.0 / (1.0 + jnp.exp(-x))


def _rms(x, g):
    ms = jnp.mean(x * x, axis=-1, keepdims=True)
    return x * lax.rsqrt(ms + EPS) * g


def _for_rows(n_rows, blk, body, unroll=4):
    def step(i, c):
        body(pl.multiple_of(i * blk, blk))
        return c
    lax.fori_loop(0, n_rows // blk, step, 0, unroll=unroll)


def _mm(a, w_ref, c0, width):
    return jnp.dot(a, w_ref[:, c0:c0 + width], preferred_element_type=f32)


def _head_row(vals, k, lane):
    row = jnp.full((1, GQ), vals(GROUP * k + GROUP - 1), f32)
    for g in range(GROUP - 2, -1, -1):
        row = jnp.where(lane < CHUNK * (g + 1), vals(GROUP * k + g), row)
    return row


def _rel_bucket(rel):
    nb = NUM_BUCKETS // 2
    max_exact = nb // 2
    ret = (rel > 0).astype(jnp.int32) * nb
    n = jnp.abs(rel)
    nf = jnp.maximum(n, 1).astype(f32)
    large = max_exact + (jnp.log(nf / max_exact) / math.log(MAX_DISTANCE / max_exact)
                         * (nb - max_exact)).astype(jnp.int32)
    large = jnp.minimum(large, nb - 1)
    return ret + jnp.where(n < max_exact, n, large)


def _bias_body(idx_ref, tab_ref, o_ref):
    idx = idx_ref[...]
    lane = lax.broadcasted_iota(jnp.int32, (1, GQ), 1)
    for k in range(KV_HEADS):
        acc = jnp.zeros((BAND, GQ), f32)
        for b in range(NUM_BUCKETS):
            acc = jnp.where(idx == b, _head_row(lambda h: tab_ref[b, h], k, lane), acc)
        o_ref[k] = acc


def _bias_table(rel_table):
    kj = jnp.arange(BAND, dtype=jnp.int32)
    rel = kj[None, :] - WINDOW - jnp.arange(CHUNK, dtype=jnp.int32)[:, None]
    idx_t = jnp.tile(_rel_bucket(rel).T, (1, GROUP))
    return pl.pallas_call(
        _bias_body,
        in_specs=[pl.BlockSpec(memory_space=pltpu.VMEM), pl.BlockSpec(memory_space=pltpu.SMEM)],
        out_specs=pl.BlockSpec(memory_space=pltpu.VMEM),
        out_shape=jax.ShapeDtypeStruct((KV_HEADS, BAND, GQ), f32),
        name="rel_bias",
    )(idx_t, rel_table)


def _mod_body(c_ref, w_ref, b_ref, o_ref):
    c = c_ref[...]
    s = c * _sigmoid(c)
    o_ref[...] = jnp.dot(s.astype(bf16), w_ref[...].astype(bf16), preferred_element_type=f32) + b_ref[...]


def _modulation(c_all, w_ada, b_ada):
    rows = c_all.shape[0]
    blk = 512
    return pl.pallas_call(
        _mod_body,
        grid=(6 * D_MODEL // blk,),
        in_specs=[pl.BlockSpec((rows, D_MODEL), lambda j: (0, 0)),
                  pl.BlockSpec((D_MODEL, blk), lambda j: (0, j)),
                  pl.BlockSpec((1, blk), lambda j: (0, j))],
        out_specs=pl.BlockSpec((rows, blk), lambda j: (0, j)),
        out_shape=jax.ShapeDtypeStruct((rows, 6 * D_MODEL), f32),
        name="adaln_mod",
    )(c_all, w_ada, b_ada)


def _attn_pair(chunks, bias_ref, sink_rows, st_s, pb_s, attT_ref, grp):
    lane_lo = lax.broadcasted_iota(jnp.int32, (CHUNK, LANES), 1) < HEAD_DIM
    n_pad = 2 * LANES - BAND
    for par, (q_c, kd_bands, v_win, pad_top, valid) in enumerate(chunks):
        for k in range(KV_HEADS):
            blocks = []
            for p in range(2):
                qc = q_c[:, (2 * k + p) * LANES:(2 * k + p + 1) * LANES]
                blocks.append(jnp.where(lane_lo, qc, jnp.zeros_like(qc)))
                blocks.append(jnp.where(lane_lo, jnp.zeros_like(qc), qc))
            qm = jnp.concatenate(blocks, axis=0)
            st_s[par * KV_HEADS + k] = lax.dot_general(kd_bands[k], qm, _NT, preferred_element_type=f32)
    for par, (q_c, kd_bands, v_win, pad_top, valid) in enumerate(chunks):
        for k in range(KV_HEADS):
            i = par * KV_HEADS + k
            st = st_s[i] + bias_ref[k]
            if valid is not None:
                st = jnp.where(valid, st, NEG)
            m = jnp.maximum(jnp.max(st, axis=0, keepdims=True), sink_rows[k])
            p_ = jnp.exp(st - m)
            den = jnp.sum(p_, axis=0, keepdims=True) + jnp.exp(sink_rows[k] - m)
            r0 = n_pad if pad_top else 0
            pb_s[i, r0:r0 + BAND, :] = (p_ / den).astype(bf16)
            z0 = 0 if pad_top else BAND
            pb_s[i, z0:z0 + n_pad, :] = jnp.zeros((n_pad, GQ), bf16)
    for par, (q_c, kd_bands, v_win, pad_top, valid) in enumerate(chunks):
        for k in range(KV_HEADS):
            i = par * KV_HEADS + k
            ot = jnp.dot(v_win[k * HEAD_DIM:(k + 1) * HEAD_DIM, :], pb_s[i], preferred_element_type=f32)
            for g in range(GROUP):
                h = GROUP * k + g
                attT_ref[grp, h * HEAD_DIM:(h + 1) * HEAD_DIM, par * CHUNK:(par + 1) * CHUNK] = (
                    ot[:, g * CHUNK:(g + 1) * CHUNK])


def _conv_seq(zb_ref, s, dww_ref, cb_ref, out_row0, n_rows):
    for j in range(N_SLABS):
        def blk(i, c):
            r0 = pl.multiple_of(i * 32, 32)
            accs = [jnp.zeros((8, LANES), f32) for _ in range(4)]
            for k in range(CONV_W):
                wk = dww_ref[k:k + 1, j * LANES:(j + 1) * LANES]
                for a in range(4):
                    start = r0 + (a // 2) * 16 + (a % 2) + (HIST - (CONV_W - 1)) + k
                    accs[a] = accs[a] + zb_ref[s, j, pl.ds(start, 8, stride=2), :] * wk
            for a in range(4):
                start = out_row0 + r0 + (a // 2) * 16 + (a % 2)
                cb_ref[j, pl.ds(start, 8, stride=2), :] = accs[a]
            return c
        lax.fori_loop(0, n_rows // 32, blk, 0)


def _branch_body(*refs, is_prompt, nb, L, n_tiles):
    if is_prompt:
        (x_ref, mod_ref, bias_ref, sink_ref, g1_ref, win_ref, wkd_ref, wvt_ref, wao_ref, wco_ref, wo_ref,
         dww_ref, dwb_ref, lng_ref, lnb_ref,
         x1_ref, nk_ref, nv_ref, nc_ref,
         h_s, q_s, kd_s, vt_s, st_s, pb_s, attT_s, att_s, zb_s, cb_s, cbf_s, acc_s) = refs
        ck_ref = cv_ref = cc_ref = None
    else:
        (x_ref, mod_ref, ck_ref, cv_ref, cc_ref, bias_ref, sink_ref, g1_ref, win_ref, wkd_ref, wvt_ref,
         wao_ref, wco_ref, wo_ref, dww_ref, dwb_ref, lng_ref, lnb_ref,
         x1_ref, nk_ref, nv_ref, nc_ref,
         h_s, q_s, kd_s, vt_s, st_s, pb_s, attT_s, att_s, zb_s, cb_s, cbf_s, acc_s) = refs
    T = nb * L
    n_grp = T // PAIR

    if is_prompt:
        t = pl.program_id(1)

        @pl.when(t == 0)
        def _():
            kd_s[:, 0:WINDOW, :] = jnp.zeros((KV_HEADS, WINDOW, LANES), bf16)
            vt_s[0] = jnp.zeros((KV_W, LANES), bf16)
            zb_s[0, :, 0:HIST, :] = jnp.zeros((N_SLABS, HIST, LANES), f32)

    for s in range(nb):
        def norm_rows(r0, s=s):
            x = x_ref[s, pl.ds(r0, ROW_BLK), :]
            hh = _rms(x, g1_ref[...]) * (1.0 + mod_ref[s, 1:2, :]) + mod_ref[s, 0:1, :]
            h_s[pl.ds(s * L + r0, ROW_BLK), :] = hh.astype(bf16)
        _for_rows(L, ROW_BLK, norm_rows)

    hv = h_s[...]

    for c in range(Q_W // COL_BLK):
        qf = _mm(hv, win_ref, c * COL_BLK, COL_BLK)
        q_s[:, c * COL_BLK:(c + 1) * COL_BLK] = (qf * (HEAD_DIM ** -0.5)).astype(bf16)
    kdf = _mm(hv, wkd_ref, 0, KV_HEADS * LANES)
    vtf = lax.dot_general(wvt_ref[...], hv, _NT, preferred_element_type=f32)
    if is_prompt:
        for k in range(KV_HEADS):
            kd_s[k, WINDOW:WINDOW + T, :] = kdf[:, k * LANES:(k + 1) * LANES].astype(bf16)
        for g in range(n_grp):
            vt_s[1 + g] = vtf[:, g * LANES:(g + 1) * LANES].astype(bf16)

        @pl.when(t == n_tiles - 1)
        def _():
            tail = h_s[T - WINDOW:T, :]
            nk_ref[0] = _mm(tail, win_ref, O_K, KV_W)
            nv_ref[0] = _mm(tail, win_ref, O_V, KV_W)
    else:
        kf = _mm(hv, win_ref, O_K, KV_W)
        vf = _mm(hv, win_ref, O_V, KV_W)
        for s in range(nb):
            ck = ck_ref[s]
            for k in range(KV_HEADS):
                ckk = ck[:, k * HEAD_DIM:(k + 1) * HEAD_DIM].astype(bf16)
                kd_s[s, k, 0:WINDOW, :] = jnp.concatenate([ckk, ckk], axis=1)
                kd_s[s, k, WINDOW:BAND, :] = kdf[s * L:(s + 1) * L, k * LANES:(k + 1) * LANES].astype(bf16)
            vt_s[s, :, 0:WINDOW] = cv_ref[s].T.astype(bf16)
            vt_s[s, :, WINDOW:BAND] = vtf[:, s * L:(s + 1) * L].astype(bf16)
            vt_s[s, :, BAND:2 * LANES] = jnp.zeros((KV_W, 2 * LANES - BAND), bf16)
            nk_ref[s, 0:WINDOW - L, :] = ck[L:WINDOW, :]
            nv_ref[s, 0:WINDOW - L, :] = cv_ref[s, L:WINDOW, :]
            nk_ref[s, WINDOW - L:WINDOW, :] = kf[s * L:(s + 1) * L]
            nv_ref[s, WINDOW - L:WINDOW, :] = vf[s * L:(s + 1) * L]

    lane_gq = lax.broadcasted_iota(jnp.int32, (1, GQ), 1)
    sink_rows = [_head_row(lambda h: sink_ref[h], k, lane_gq) for k in range(KV_HEADS)]
    if is_prompt:
        def attn_step(c2, carry):
            row0 = pl.multiple_of(c2 * PAIR, PAIR)
            v_win = jnp.concatenate([vt_s[c2], vt_s[c2 + 1]], axis=1)
            chunks = []
            for par in range(2):
                r = pl.multiple_of(row0 + par * CHUNK, CHUNK)
                pos = t * T + r - WINDOW + lax.broadcasted_iota(jnp.int32, (BAND, 1), 0)
                chunks.append((q_s[pl.ds(r, CHUNK), :],
                               [kd_s[k, pl.ds(r, BAND), :] for k in range(KV_HEADS)],
                               v_win, par == 1, pos >= 0))
            _attn_pair(chunks, bias_ref, sink_rows, st_s, pb_s, attT_s, c2)
            return carry
    else:
        def attn_step(s2, carry):
            chunks = []
            for par in range(2):
                s = 2 * s2 + par
                r = pl.multiple_of(s * CHUNK, CHUNK)
                chunks.append((q_s[pl.ds(r, CHUNK), :],
                               [kd_s[s, k] for k in range(KV_HEADS)],
                               vt_s[s], False, None))
            _attn_pair(chunks, bias_ref, sink_rows, st_s, pb_s, attT_s, s2)
            return carry
    lax.fori_loop(0, n_grp, attn_step, 0)
    for g in range(n_grp):
        att_s[g * PAIR:(g + 1) * PAIR, :] = attT_s[g].T.astype(bf16)

    av = att_s[...]
    for c in range(D_MODEL // COL_BLK):
        ya = _mm(av, wao_ref, c * COL_BLK, COL_BLK)
        ga = _sigmoid(_mm(hv, win_ref, O_GA + c * COL_BLK, COL_BLK))
        acc_s[:, c * COL_BLK:(c + 1) * COL_BLK] = ga * ya

    if not is_prompt:
        for s in range(nb):
            for j in range(N_SLABS):
                zb_s[s, j, HIST - (CONV_W - 1):HIST, :] = cc_ref[s, :, j * LANES:(j + 1) * LANES]
    for c in range(D_MODEL // COL_BLK):
        za = _mm(hv, win_ref, O_GLU_A + c * COL_BLK, COL_BLK)
        zg = _mm(hv, win_ref, O_GLU_G + c * COL_BLK, COL_BLK)
        zc = za * _sigmoid(zg)
        for jj in range(COL_BLK // LANES):
            j = c * (COL_BLK // LANES) + jj
            z = zc[:, jj * LANES:(jj + 1) * LANES]
            for s in range(nb):
                zb_s[s, j, HIST:HIST + L, :] = z[s * L:(s + 1) * L]
                if is_prompt:
                    @pl.when(t == n_tiles - 1)
                    def _(z=z, j=j):
                        nc_ref[0, :, j * LANES:(j + 1) * LANES] = z[L - (CONV_W - 1):L]
                else:
                    nc_ref[s, :, j * LANES:(j + 1) * LANES] = z[s * L + L - (CONV_W - 1):(s + 1) * L]

    for s in range(nb):
        _conv_seq(zb_s, s, dww_ref, cb_s, s * L, L)

    def ln_rows(r0):
        ys = [cb_s[j, pl.ds(r0, ROW_BLK), :] + dwb_ref[:, j * LANES:(j + 1) * LANES] for j in range(N_SLABS)]
        tot = ys[0]
        for j in range(1, N_SLABS):
            tot = tot + ys[j]
        mu = jnp.sum(tot, axis=-1, keepdims=True) * (1.0 / D_MODEL)
        ds = [y - mu for y in ys]
        sq = ds[0] * ds[0]
        for j in range(1, N_SLABS):
            sq = sq + ds[j] * ds[j]
        var = jnp.sum(sq, axis=-1, keepdims=True) * (1.0 / D_MODEL)
        rstd = lax.rsqrt(var + EPS)
        for j in range(N_SLABS):
            y = ds[j] * rstd * lng_ref[:, j * LANES:(j + 1) * LANES] + lnb_ref[:, j * LANES:(j + 1) * LANES]
            cbf_s[pl.ds(r0, ROW_BLK), j * LANES:(j + 1) * LANES] = (y * _sigmoid(y)).astype(bf16)
    _for_rows(T, ROW_BLK, ln_rows)

    cv = cbf_s[...]
    for c in range(D_MODEL // COL_BLK):
        yb = _mm(cv, wco_ref, c * COL_BLK, COL_BLK)
        gb = _sigmoid(_mm(hv, win_ref, O_GB + c * COL_BLK, COL_BLK))
        acc_s[:, c * COL_BLK:(c + 1) * COL_BLK] += gb * yb

    mv = acc_s[...].astype(bf16)
    for c in range(D_MODEL // COL_BLK):
        m = _mm(mv, wo_ref, c * COL_BLK, COL_BLK)
        for s in range(nb):
            cols = slice(c * COL_BLK, (c + 1) * COL_BLK)
            x1_ref[s, :, cols] = x_ref[s, :, cols] + mod_ref[s, 2:3, cols] * m[s * L:(s + 1) * L]

    if is_prompt:
        kd_s[:, 0:WINDOW, :] = kd_s[:, T:T + WINDOW, :]
        vt_s[0] = vt_s[n_grp]
        zb_s[0, :, 0:HIST, :] = zb_s[0, :, L:L + HIST, :]


def _const_spec(shape):
    nd = len(shape)
    return pl.BlockSpec(shape, lambda *_: (0,) * nd, pipeline_mode=pl.Buffered(1))


def _branch_weights_specs():
    return [
        _const_spec((KV_HEADS, BAND, GQ)),
        pl.BlockSpec(memory_space=pltpu.SMEM),
        _const_spec((1, D_MODEL)),
        _const_spec((D_MODEL, IN_W)),
        _const_spec((D_MODEL, KV_HEADS * LANES)),
        _const_spec((KV_W, D_MODEL)),
        _const_spec((Q_W, D_MODEL)),
        _const_spec((D_MODEL, D_MODEL)),
        _const_spec((D_MODEL, D_MODEL)),
        _const_spec((CONV_W, D_MODEL)),
        _const_spec((1, D_MODEL)),
        _const_spec((1, D_MODEL)),
        _const_spec((1, D_MODEL)),
    ]


def _branch_scratch(nb, L, is_prompt):
    T = nb * L
    if is_prompt:
        kd_shape = (KV_HEADS, WINDOW + T, LANES)
        vt_shape = (1 + T // PAIR, KV_W, LANES)
    else:
        kd_shape = (nb, KV_HEADS, BAND, LANES)
        vt_shape = (nb, KV_W, 2 * LANES)
    return [
        pltpu.VMEM((T, D_MODEL), bf16),
        pltpu.VMEM((T, Q_W), bf16),
        pltpu.VMEM(kd_shape, bf16),
        pltpu.VMEM(vt_shape, bf16),
        pltpu.VMEM((2 * KV_HEADS, BAND, GQ), f32),
        pltpu.VMEM((2 * KV_HEADS, 2 * LANES, GQ), bf16),
        pltpu.VMEM((T // PAIR, Q_W, PAIR), f32),
        pltpu.VMEM((T, Q_W), bf16),
        pltpu.VMEM((nb, N_SLABS, HIST + L, LANES), f32),
        pltpu.VMEM((N_SLABS, T, LANES), f32),
        pltpu.VMEM((T, D_MODEL), bf16),
        pltpu.VMEM((T, D_MODEL), f32),
    ]


def _branch_prompt(x, mod, weights, tq):
    B, S, _ = x.shape
    n_tiles = S // tq
    body = functools.partial(_branch_body, is_prompt=True, nb=1, L=tq, n_tiles=n_tiles)
    return pl.pallas_call(
        body,
        grid=(B, n_tiles),
        in_specs=[pl.BlockSpec((1, tq, D_MODEL), lambda b, t: (b, t, 0)),
                  pl.BlockSpec((1, 6, D_MODEL), lambda b, t: (b, 0, 0))] + _branch_weights_specs(),
        out_specs=[pl.BlockSpec((1, tq, D_MODEL), lambda b, t: (b, t, 0)),
                   pl.BlockSpec((1, WINDOW, KV_W), lambda b, t: (b, 0, 0)),
                   pl.BlockSpec((1, WINDOW, KV_W), lambda b, t: (b, 0, 0)),
                   pl.BlockSpec((1, CONV_W - 1, D_MODEL), lambda b, t: (b, 0, 0))],
        out_shape=[jax.ShapeDtypeStruct((B, S, D_MODEL), f32),
                   jax.ShapeDtypeStruct((B, WINDOW, KV_W), f32),
                   jax.ShapeDtypeStruct((B, WINDOW, KV_W), f32),
                   jax.ShapeDtypeStruct((B, CONV_W - 1, D_MODEL), f32)],
        scratch_shapes=_branch_scratch(1, tq, True),
        compiler_params=pltpu.CompilerParams(
            dimension_semantics=("arbitrary", "arbitrary"), vmem_limit_bytes=VMEM_LIMIT),
        name="branch_prompt",
    )(x, mod, *weights)


def _branch_sample(x, mod, ck, cv, cc, weights, nb):
    B, L, _ = x.shape
    body = functools.partial(_branch_body, is_prompt=False, nb=nb, L=L, n_tiles=B // nb)
    seq3 = lambda i: (i, 0, 0)
    return pl.pallas_call(
        body,
        grid=(B // nb,),
        in_specs=[pl.BlockSpec((nb, L, D_MODEL), seq3),
                  pl.BlockSpec((nb, 6, D_MODEL), seq3),
                  pl.BlockSpec((nb, WINDOW, KV_W), seq3),
                  pl.BlockSpec((nb, WINDOW, KV_W), seq3),
                  pl.BlockSpec((nb, CONV_W - 1, D_MODEL), seq3)] + _branch_weights_specs(),
        out_specs=[pl.BlockSpec((nb, L, D_MODEL), seq3),
                   pl.BlockSpec((nb, WINDOW, KV_W), seq3),
                   pl.BlockSpec((nb, WINDOW, KV_W), seq3),
                   pl.BlockSpec((nb, CONV_W - 1, D_MODEL), seq3)],
        out_shape=[jax.ShapeDtypeStruct((B, L, D_MODEL), f32),
                   jax.ShapeDtypeStruct((B, WINDOW, KV_W), f32),
                   jax.ShapeDtypeStruct((B, WINDOW, KV_W), f32),
                   jax.ShapeDtypeStruct((B, CONV_W - 1, D_MODEL), f32)],
        scratch_shapes=_branch_scratch(nb, L, False),
        compiler_params=pltpu.CompilerParams(
            dimension_semantics=("arbitrary",), vmem_limit_bytes=VMEM_LIMIT),
        name="branch_sample",
    )(x, mod, ck, cv, cc, *weights)


def _ffn_body(x1_ref, mod_ref, g2_ref, fg_ref, wup_ref, wdn_ref, y_ref, h_s, act_s, *, nb, L):
    T = nb * L
    for s in range(nb):
        def norm_rows(r0, s=s):
            x = x1_ref[s, pl.ds(r0, ROW_BLK), :]
            hh = _rms(x, g2_ref[...]) * (1.0 + mod_ref[s, 4:5, :]) + mod_ref[s, 3:4, :]
            h_s[pl.ds(s * L + r0, ROW_BLK), :] = hh.astype(bf16)
        _for_rows(L, ROW_BLK, norm_rows)
    hv = h_s[...]
    for c in range(D_FF // FF_BLK):
        gate = _mm(hv, wup_ref, c * FF_BLK, FF_BLK)
        up = _mm(hv, wup_ref, D_FF + c * FF_BLK, FF_BLK)
        act_s[:, c * FF_BLK:(c + 1) * FF_BLK] = (gate * _sigmoid(gate) * up).astype(bf16)
    av = act_s[...]
    for c in range(D_MODEL // COL_BLK):
        d = _mm(av, wdn_ref, c * COL_BLK, COL_BLK)
        for s in range(nb):
            cols = slice(c * COL_BLK, (c + 1) * COL_BLK)
            y_ref[s, :, cols] = x1_ref[s, :, cols] + mod_ref[s, 5:6, cols] * d[s * L:(s + 1) * L]
    for s in range(nb):
        def fin_rows(r0, s=s):
            x2 = y_ref[s, pl.ds(r0, ROW_BLK), :]
            y_ref[s, pl.ds(r0, ROW_BLK), :] = _rms(x2, fg_ref[...])
        _for_rows(L, ROW_BLK, fin_rows)


def _ffn(x1, mod, g2, fg, wup, wdn, nb, L, name):
    B, S, _ = x1.shape
    n_tiles = S // L
    if nb == 1:
        grid = (B, n_tiles)
        xmap = lambda b, t: (b, t, 0)
        mmap = lambda b, t: (b, 0, 0)
    else:
        grid = (B // nb,)
        xmap = mmap = lambda i: (i, 0, 0)
    body = functools.partial(_ffn_body, nb=nb, L=L)
    T = nb * L
    return pl.pallas_call(
        body,
        grid=grid,
        in_specs=[pl.BlockSpec((nb, L, D_MODEL), xmap),
                  pl.BlockSpec((nb, 6, D_MODEL), mmap),
                  _const_spec((1, D_MODEL)),
                  _const_spec((1, D_MODEL)),
                  _const_spec((D_MODEL, 2 * D_FF)),
                  _const_spec((D_FF, D_MODEL))],
        out_specs=pl.BlockSpec((nb, L, D_MODEL), xmap),
        out_shape=jax.ShapeDtypeStruct(x1.shape, f32),
        scratch_shapes=[pltpu.VMEM((T, D_MODEL), bf16), pltpu.VMEM((T, D_FF), bf16)],
        compiler_params=pltpu.CompilerParams(
            dimension_semantics=("arbitrary",) * len(grid), vmem_limit_bytes=VMEM_LIMIT),
        name=name,
    )(x1, mod, g2, fg, wup, wdn)


def kernel(x_prompt, x_sample, cache_k, cache_v, cache_conv, c_prompt, c_sample, rel_table, w_ada, b_ada, norm1_g, norm2_g, w_in, sink, w_attn_out, dw_w, dw_b, conv_ln_g, conv_ln_b, w_conv_out, w_out, w_ffn_up, w_ffn_down, final_g):
    assert w_ada.shape[0] == 1, "single-layer kernel"
    B, S, _ = x_prompt.shape
    DB, DS, _ = x_sample.shape
    TQ = 512
    NB = 4

    bias = _bias_table(rel_table)
    n_seq = B + DB
    pad = (-n_seq) % 8
    c_all = jnp.concatenate([c_prompt, c_sample, jnp.zeros((pad, D_MODEL), f32)], axis=0)
    mod = _modulation(c_all, w_ada[0], b_ada).reshape(n_seq + pad, 6, D_MODEL)
    mod_p, mod_s = mod[:B], mod[B:B + DB]

    row = lambda v: v.reshape(1, D_MODEL)
    w_in_b = w_in[0].astype(bf16)
    w_k = w_in_b[:, O_K:O_K + KV_W].reshape(D_MODEL, KV_HEADS, 1, HEAD_DIM)
    w_kd = jnp.broadcast_to(w_k, (D_MODEL, KV_HEADS, 2, HEAD_DIM)).reshape(D_MODEL, KV_HEADS * LANES)
    w_vt = w_in_b[:, O_V:O_V + KV_W].T
    weights = (bias, sink[0], row(norm1_g[0]), w_in_b, w_kd, w_vt, w_attn_out[0].astype(bf16),
               w_conv_out[0].astype(bf16), w_out[0].astype(bf16), dw_w[0], row(dw_b[0]),
               row(conv_ln_g[0]), row(conv_ln_b[0]))
    wup = w_ffn_up[0].astype(bf16)
    wdn = w_ffn_down[0].astype(bf16)
    g2 = row(norm2_g[0])
    fg = row(final_g)

    x1p, nkp, nvp, ncp = _branch_prompt(x_prompt, mod_p, weights, TQ)
    ck = cache_k[0].reshape(DB, WINDOW, KV_W)
    cv = cache_v[0].reshape(DB, WINDOW, KV_W)
    x1s, nks, nvs, ncs = _branch_sample(x_sample, mod_s, ck, cv, cache_conv[0], weights, NB)

    y_p = _ffn(x1p, mod_p, g2, fg, wup, wdn, 1, TQ, "ffn_prompt")
    y_s = _ffn(x1s, mod_s, g2, fg, wup, wdn, NB, DS, "ffn_sample")

    kv5 = lambda a: a.reshape(1, a.shape[0], WINDOW, KV_HEADS, HEAD_DIM)
    return (y_p, y_s, kv5(nkp), kv5(nvp), ncp[None], kv5(nks), kv5(nvs), ncs[None])
```

```python
import functools
import math

import jax
import jax.numpy as jnp
from jax import lax
from jax.experimental import pallas as pl
from jax.experimental.pallas import tpu as pltpu

f32 = jnp.float32
bf16 = jnp.bfloat16

D_MODEL = 1024
N_HEADS = 16
KV_HEADS = 4
HEAD_DIM = 64
GROUP = N_HEADS // KV_HEADS
WINDOW = 128
CHUNK = 64
BAND = WINDOW + CHUNK
CONV_W = 31
D_FF = 2816
NUM_BUCKETS = 32
MAX_DISTANCE = 128
EPS = 1e-6
NEG = -1e30
Q_W = N_HEADS * HEAD_DIM
KV_W = KV_HEADS * HEAD_DIM
IN_W = Q_W + 2 * KV_W + 4 * D_MODEL
O_K = Q_W
O_V = O_K + KV_W
O_GLU_A = O_V + KV_W
O_GLU_G = O_GLU_A + D_MODEL
O_GA = O_GLU_G + D_MODEL
O_GB = O_GA + D_MODEL

LANES = 128
N_SLABS = D_MODEL // LANES
PAIR = 2 * CHUNK
GQ = GROUP * CHUNK
HIST = 32
ROW_BLK = 32
COL_BLK = 256
FF_BLK = 256
VMEM_LIMIT = 56 * 1024 * 1024

_NT = (((1,), (1,)), ((), ()))


def _sigmoid(x):
    return 1.0 / (1.0 + jnp.exp(-x))


def _rms(x, g):
    ms = jnp.mean(x * x, axis=-1, keepdims=True)
    return x * lax.rsqrt(ms + EPS) * g


def _for_rows(n_rows, blk, body, unroll=4):
    def step(i, c):
        body(pl.multiple_of(i * blk, blk))
        return c
    lax.fori_loop(0, n_rows // blk, step, 0, unroll=unroll)


def _mm(a, w_ref, c0, width):
    return jnp.dot(a, w_ref[:, c0:c0 + width], preferred_element_type=f32)


def _head_row(vals, k, lane):
    row = jnp.full((1, GQ), vals(GROUP * k + GROUP - 1), f32)
    for g in range(GROUP - 2, -1, -1):
        row = jnp.where(lane < CHUNK * (g + 1), vals(GROUP * k + g), row)
    return row


def _rel_bucket(rel):
    nb = NUM_BUCKETS // 2
    max_exact = nb // 2
    ret = (rel > 0).astype(jnp.int32) * nb
    n = jnp.abs(rel)
    nf = jnp.maximum(n, 1).astype(f32)
    large = max_exact + (jnp.log(nf / max_exact) / math.log(MAX_DISTANCE / max_exact)
                         * (nb - max_exact)).astype(jnp.int32)
    large = jnp.minimum(large, nb - 1)
    return ret + jnp.where(n < max_exact, n, large)


def _bias_body(idx_ref, tab_ref, o_ref):
    idx = idx_ref[...]
    lane = lax.broadcasted_iota(jnp.int32, (1, GQ), 1)
    for k in range(KV_HEADS):
        acc = jnp.zeros((BAND, GQ), f32)
        for b in range(NUM_BUCKETS):
            acc = jnp.where(idx == b, _head_row(lambda h: tab_ref[b, h], k, lane), acc)
        o_ref[k] = acc


def _bias_table(rel_table):
    kj = jnp.arange(BAND, dtype=jnp.int32)
    rel = kj[None, :] - WINDOW - jnp.arange(CHUNK, dtype=jnp.int32)[:, None]
    idx_t = jnp.tile(_rel_bucket(rel).T, (1, GROUP))
    return pl.pallas_call(
        _bias_body,
        in_specs=[pl.BlockSpec(memory_space=pltpu.VMEM), pl.BlockSpec(memory_space=pltpu.SMEM)],
        out_specs=pl.BlockSpec(memory_space=pltpu.VMEM),
        out_shape=jax.ShapeDtypeStruct((KV_HEADS, BAND, GQ), f32),
        name="rel_bias",
    )(idx_t, rel_table)


def _mod_body(c_ref, w_ref, b_ref, o_ref):
    c = c_ref[...]
    s = c * _sigmoid(c)
    o_ref[...] = jnp.dot(s.astype(bf16), w_ref[...].astype(bf16), preferred_element_type=f32) + b_ref[...]


def _modulation(c_all, w_ada, b_ada):
    rows = c_all.shape[0]
    blk = 512
    return pl.pallas_call(
        _mod_body,
        grid=(6 * D_MODEL // blk,),
        in_specs=[pl.BlockSpec((rows, D_MODEL), lambda j: (0, 0)),
                  pl.BlockSpec((D_MODEL, blk), lambda j: (0, j)),
                  pl.BlockSpec((1, blk), lambda j: (0, j))],
        out_specs=pl.BlockSpec((rows, blk), lambda j: (0, j)),
        out_shape=jax.ShapeDtypeStruct((rows, 6 * D_MODEL), f32),
        name="adaln_mod",
    )(c_all, w_ada, b_ada)


def _attn_pair(chunks, bias_ref, sink_rows, st_s, pb_s, attT_ref, grp):
    lane_lo = lax.broadcasted_iota(jnp.int32, (CHUNK, LANES), 1) < HEAD_DIM
    n_pad = 2 * LANES - BAND
    for par, (q_c, kd_bands, v_win, pad_top, valid) in enumerate(chunks):
        for k in range(KV_HEADS):
            blocks = []
            for p in range(2):
                qc = q_c[:, (2 * k + p) * LANES:(2 * k + p + 1) * LANES]
                blocks.append(jnp.where(lane_lo, qc, jnp.zeros_like(qc)))
                blocks.append(jnp.where(lane_lo, jnp.zeros_like(qc), qc))
            qm = jnp.concatenate(blocks, axis=0)
            st_s[par * KV_HEADS + k] = lax.dot_general(kd_bands[k], qm, _NT, preferred_element_type=f32)
    for par, (q_c, kd_bands, v_win, pad_top, valid) in enumerate(chunks):
        for k in range(KV_HEADS):
            i = par * KV_HEADS + k
            st = st_s[i] + bias_ref[k]
            if valid is not None:
                st = jnp.where(valid, st, NEG)
            m = jnp.maximum(jnp.max(st, axis=0, keepdims=True), sink_rows[k])
            p_ = jnp.exp(st - m)
            den = jnp.sum(p_, axis=0, keepdims=True) + jnp.exp(sink_rows[k] - m)
            r0 = n_pad if pad_top else 0
            pb_s[i, r0:r0 + BAND, :] = (p_ / den).astype(bf16)
            z0 = 0 if pad_top else BAND
            pb_s[i, z0:z0 + n_pad, :] = jnp.zeros((n_pad, GQ), bf16)
    for par, (q_c, kd_bands, v_win, pad_top, valid) in enumerate(chunks):
        for k in range(KV_HEADS):
            i = par * KV_HEADS + k
            ot = jnp.dot(v_win[k * HEAD_DIM:(k + 1) * HEAD_DIM, :], pb_s[i], preferred_element_type=f32)
            for g in range(GROUP):
                h = GROUP * k + g
                attT_ref[grp, h * HEAD_DIM:(h + 1) * HEAD_DIM, par * CHUNK:(par + 1) * CHUNK] = (
                    ot[:, g * CHUNK:(g + 1) * CHUNK])


def _conv_block(zb_ref, s, j, i, dww_ref, cb_ref, out_row0):
    r0 = i * 32
    accs = [jnp.zeros((8, LANES), f32) for _ in range(4)]
    for k in range(CONV_W):
        wk = dww_ref[k:k + 1, j * LANES:(j + 1) * LANES]
        for a in range(4):
            start = r0 + (a // 2) * 16 + (a % 2) + (HIST - (CONV_W - 1)) + k
            accs[a] = accs[a] + zb_ref[s, j, pl.ds(start, 8, stride=2), :] * wk
    for a in range(4):
        start = out_row0 + r0 + (a // 2) * 16 + (a % 2)
        cb_ref[j, pl.ds(start, 8, stride=2), :] = accs[a]


def _interleave(pieces, blocks):
    per = -(-len(blocks) // len(pieces))
    for n, piece in enumerate(pieces):
        piece()
        for blk in blocks[n * per:(n + 1) * per]:
            blk()


def _branch_body(*refs, is_prompt, nb, L, n_tiles):
    if is_prompt:
        (x_ref, mod_ref, bias_ref, sink_ref, g1_ref, win_ref, wkd_ref, wvt_ref, wao_ref, wco_ref, wo_ref,
         dww_ref, dwb_ref, lng_ref, lnb_ref,
         x1_ref, nk_ref, nv_ref, nc_ref,
         h_s, q_s, kd_s, vt_s, st_s, pb_s, attT_s, att_s, zb_s, cb_s, cbf_s, ga_s, gb_s, acc_s) = refs
        ck_ref = cv_ref = cc_ref = None
    else:
        (x_ref, mod_ref, ck_ref, cv_ref, cc_ref, bias_ref, sink_ref, g1_ref, win_ref, wkd_ref, wvt_ref,
         wao_ref, wco_ref, wo_ref, dww_ref, dwb_ref, lng_ref, lnb_ref,
         x1_ref, nk_ref, nv_ref, nc_ref,
         h_s, q_s, kd_s, vt_s, st_s, pb_s, attT_s, att_s, zb_s, cb_s, cbf_s, ga_s, gb_s, acc_s) = refs
    T = nb * L
    n_grp = T // PAIR

    if is_prompt:
        t = pl.program_id(1)

        @pl.when(t == 0)
        def _():
            kd_s[:, 0:WINDOW, :] = jnp.zeros((KV_HEADS, WINDOW, LANES), bf16)
            vt_s[0] = jnp.zeros((KV_W, LANES), bf16)
            zb_s[0, :, 0:HIST, :] = jnp.zeros((N_SLABS, HIST, LANES), f32)

    for s in range(nb):
        def norm_rows(r0, s=s):
            x = x_ref[s, pl.ds(r0, ROW_BLK), :]
            hh = _rms(x, g1_ref[...]) * (1.0 + mod_ref[s, 1:2, :]) + mod_ref[s, 0:1, :]
            h_s[pl.ds(s * L + r0, ROW_BLK), :] = hh.astype(bf16)
        _for_rows(L, ROW_BLK, norm_rows)

    hv = h_s[...]
    n_cc = D_MODEL // COL_BLK

    if not is_prompt:
        for s in range(nb):
            for j in range(N_SLABS):
                zb_s[s, j, HIST - (CONV_W - 1):HIST, :] = cc_ref[s, :, j * LANES:(j + 1) * LANES]
    for c in range(n_cc):
        za = _mm(hv, win_ref, O_GLU_A + c * COL_BLK, COL_BLK)
        zg = _mm(hv, win_ref, O_GLU_G + c * COL_BLK, COL_BLK)
        zc = za * _sigmoid(zg)
        for jj in range(COL_BLK // LANES):
            j = c * (COL_BLK // LANES) + jj
            z = zc[:, jj * LANES:(jj + 1) * LANES]
            for s in range(nb):
                zb_s[s, j, HIST:HIST + L, :] = z[s * L:(s + 1) * L]
                if not is_prompt:
                    nc_ref[s, :, j * LANES:(j + 1) * LANES] = z[s * L + L - (CONV_W - 1):(s + 1) * L]

    if not is_prompt:
        for s in range(nb):
            ck = ck_ref[s]
            for k in range(KV_HEADS):
                ckk = ck[:, k * HEAD_DIM:(k + 1) * HEAD_DIM].astype(bf16)
                kd_s[s, k, 0:WINDOW, :] = jnp.concatenate([ckk, ckk], axis=1)
            vt_s[s, :, 0:WINDOW] = cv_ref[s].T.astype(bf16)
            vt_s[s, :, BAND:2 * LANES] = jnp.zeros((KV_W, 2 * LANES - BAND), bf16)
            nk_ref[s, 0:WINDOW - L, :] = ck[L:WINDOW, :]
            nv_ref[s, 0:WINDOW - L, :] = cv_ref[s, L:WINDOW, :]

    def q_piece(c):
        qf = _mm(hv, win_ref, c * COL_BLK, COL_BLK)
        q_s[:, c * COL_BLK:(c + 1) * COL_BLK] = (qf * (HEAD_DIM ** -0.5)).astype(bf16)

    def kd_piece(c):
        kdc = _mm(hv, wkd_ref, c * COL_BLK, COL_BLK)
        for kk in range(COL_BLK // LANES):
            k = c * (COL_BLK // LANES) + kk
            kdk = kdc[:, kk * LANES:(kk + 1) * LANES].astype(bf16)
            if is_prompt:
                kd_s[k, WINDOW:WINDOW + T, :] = kdk
            else:
                for s in range(nb):
                    kd_s[s, k, WINDOW:BAND, :] = kdk[s * L:(s + 1) * L]

    def vt_piece():
        vtf = lax.dot_general(wvt_ref[...], hv, _NT, preferred_element_type=f32).astype(bf16)
        if is_prompt:
            for g in range(n_grp):
                vt_s[1 + g] = vtf[:, g * LANES:(g + 1) * LANES]
        else:
            for s in range(nb):
                vt_s[s, :, WINDOW:BAND] = vtf[:, s * L:(s + 1) * L]

    def gate_piece(dst, off, c):
        cols = slice(c * COL_BLK, (c + 1) * COL_BLK)
        dst[:, cols] = _sigmoid(_mm(hv, win_ref, off + c * COL_BLK, COL_BLK))

    def new_kv_piece(dst, off):
        kvf = _mm(hv, win_ref, off, KV_W)
        for s in range(nb):
            dst[s, WINDOW - L:WINDOW, :] = kvf[s * L:(s + 1) * L]

    pieces = [functools.partial(q_piece, c) for c in range(Q_W // COL_BLK)]
    pieces += [functools.partial(kd_piece, c) for c in range(KV_HEADS * LANES // COL_BLK)]
    pieces += [vt_piece]
    pieces += [functools.partial(gate_piece, ga_s, O_GA, c) for c in range(n_cc)]
    pieces += [functools.partial(gate_piece, gb_s, O_GB, c) for c in range(n_cc)]
    if not is_prompt:
        pieces += [functools.partial(new_kv_piece, nk_ref, O_K), functools.partial(new_kv_piece, nv_ref, O_V)]
    conv_blocks = [functools.partial(_conv_block, zb_s, s, j, i, dww_ref, cb_s, s * L)
                   for s in range(nb) for j in range(N_SLABS) for i in range(L // 32)]
    _interleave(pieces, conv_blocks)

    lane_gq = lax.broadcasted_iota(jnp.int32, (1, GQ), 1)
    sink_rows = [_head_row(lambda h: sink_ref[h], k, lane_gq) for k in range(KV_HEADS)]
    if is_prompt:
        def attn_step(c2, carry):
            row0 = pl.multiple_of(c2 * PAIR, PAIR)
            v_win = jnp.concatenate([vt_s[c2], vt_s[c2 + 1]], axis=1)
            chunks = []
            for par in range(2):
                r = pl.multiple_of(row0 + par * CHUNK, CHUNK)
                pos = t * T + r - WINDOW + lax.broadcasted_iota(jnp.int32, (BAND, 1), 0)
                chunks.append((q_s[pl.ds(r, CHUNK), :],
                               [kd_s[k, pl.ds(r, BAND), :] for k in range(KV_HEADS)],
                               v_win, par == 1, pos >= 0))
            _attn_pair(chunks, bias_ref, sink_rows, st_s, pb_s, attT_s, c2)
            return carry
    else:
        def attn_step(s2, carry):
            chunks = []
            for par in range(2):
                s = 2 * s2 + par
                r = pl.multiple_of(s * CHUNK, CHUNK)
                chunks.append((q_s[pl.ds(r, CHUNK), :],
                               [kd_s[s, k] for k in range(KV_HEADS)],
                               vt_s[s], False, None))
            _attn_pair(chunks, bias_ref, sink_rows, st_s, pb_s, attT_s, s2)
            return carry
    lax.fori_loop(0, n_grp, attn_step, 0)
    for g in range(n_grp):
        att_s[g * PAIR:(g + 1) * PAIR, :] = attT_s[g].T.astype(bf16)

    av = att_s[...]

    def ya_piece(c):
        cols = slice(c * COL_BLK, (c + 1) * COL_BLK)
        acc_s[:, cols] = ga_s[:, cols] * _mm(av, wao_ref, c * COL_BLK, COL_BLK)

    def ln_block(i):
        rows = slice(i * ROW_BLK, (i + 1) * ROW_BLK)
        ys = [cb_s[j, rows, :] + dwb_ref[:, j * LANES:(j + 1) * LANES] for j in range(N_SLABS)]
        tot = ys[0]
        for j in range(1, N_SLABS):
            tot = tot + ys[j]
        mu = jnp.sum(tot, axis=-1, keepdims=True) * (1.0 / D_MODEL)
        ds = [y - mu for y in ys]
        sq = ds[0] * ds[0]
        for j in range(1, N_SLABS):
            sq = sq + ds[j] * ds[j]
        var = jnp.sum(sq, axis=-1, keepdims=True) * (1.0 / D_MODEL)
        rstd = lax.rsqrt(var + EPS)
        for j in range(N_SLABS):
            y = ds[j] * rstd * lng_ref[:, j * LANES:(j + 1) * LANES] + lnb_ref[:, j * LANES:(j + 1) * LANES]
            cbf_s[rows, j * LANES:(j + 1) * LANES] = (y * _sigmoid(y)).astype(bf16)

    _interleave([functools.partial(ya_piece, c) for c in range(n_cc)],
                [functools.partial(ln_block, i) for i in range(T // ROW_BLK)])

    cv = cbf_s[...]
    for c in range(n_cc):
        cols = slice(c * COL_BLK, (c + 1) * COL_BLK)
        acc_s[:, cols] += gb_s[:, cols] * _mm(cv, wco_ref, c * COL_BLK, COL_BLK)

    mv = acc_s[...].astype(bf16)
    for c in range(n_cc):
        m = _mm(mv, wo_ref, c * COL_BLK, COL_BLK)
        for s in range(nb):
            cols = slice(c * COL_BLK, (c + 1) * COL_BLK)
            x1_ref[s, :, cols] = x_ref[s, :, cols] + mod_ref[s, 2:3, cols] * m[s * L:(s + 1) * L]

    if is_prompt:
        @pl.when(t == n_tiles - 1)
        def _():
            tail = h_s[T - WINDOW:T, :]
            nk_ref[0] = _mm(tail, win_ref, O_K, KV_W)
            nv_ref[0] = _mm(tail, win_ref, O_V, KV_W)
            for j in range(N_SLABS):
                nc_ref[0, :, j * LANES:(j + 1) * LANES] = zb_s[0, j, HIST + L - (CONV_W - 1):HIST + L, :]

        kd_s[:, 0:WINDOW, :] = kd_s[:, T:T + WINDOW, :]
        vt_s[0] = vt_s[n_grp]
        zb_s[0, :, 0:HIST, :] = zb_s[0, :, L:L + HIST, :]


def _const_spec(shape):
    nd = len(shape)
    return pl.BlockSpec(shape, lambda *_: (0,) * nd, pipeline_mode=pl.Buffered(1))


def _branch_weights_specs():
    return [
        _const_spec((KV_HEADS, BAND, GQ)),
        pl.BlockSpec(memory_space=pltpu.SMEM),
        _const_spec((1, D_MODEL)),
        _const_spec((D_MODEL, IN_W)),
        _const_spec((D_MODEL, KV_HEADS * LANES)),
        _const_spec((KV_W, D_MODEL)),
        _const_spec((Q_W, D_MODEL)),
        _const_spec((D_MODEL, D_MODEL)),
        _const_spec((D_MODEL, D_MODEL)),
        _const_spec((CONV_W, D_MODEL)),
        _const_spec((1, D_MODEL)),
        _const_spec((1, D_MODEL)),
        _const_spec((1, D_MODEL)),
    ]


def _branch_scratch(nb, L, is_prompt):
    T = nb * L
    if is_prompt:
        kd_shape = (KV_HEADS, WINDOW + T, LANES)
        vt_shape = (1 + T // PAIR, KV_W, LANES)
    else:
        kd_shape = (nb, KV_HEADS, BAND, LANES)
        vt_shape = (nb, KV_W, 2 * LANES)
    return [
        pltpu.VMEM((T, D_MODEL), bf16),
        pltpu.VMEM((T, Q_W), bf16),
        pltpu.VMEM(kd_shape, bf16),
        pltpu.VMEM(vt_shape, bf16),
        pltpu.VMEM((2 * KV_HEADS, BAND, GQ), f32),
        pltpu.VMEM((2 * KV_HEADS, 2 * LANES, GQ), bf16),
        pltpu.VMEM((T // PAIR, Q_W, PAIR), f32),
        pltpu.VMEM((T, Q_W), bf16),
        pltpu.VMEM((nb, N_SLABS, HIST + L, LANES), f32),
        pltpu.VMEM((N_SLABS, T, LANES), f32),
        pltpu.VMEM((T, D_MODEL), bf16),
        pltpu.VMEM((T, D_MODEL), f32),
        pltpu.VMEM((T, D_MODEL), f32),
        pltpu.VMEM((T, D_MODEL), f32),
    ]


def _branch_prompt(x, mod, weights, tq):
    B, S, _ = x.shape
    n_tiles = S // tq
    body = functools.partial(_branch_body, is_prompt=True, nb=1, L=tq, n_tiles=n_tiles)
    return pl.pallas_call(
        body,
        grid=(B, n_tiles),
        in_specs=[pl.BlockSpec((1, tq, D_MODEL), lambda b, t: (b, t, 0)),
                  pl.BlockSpec((1, 6, D_MODEL), lambda b, t: (b, 0, 0))] + _branch_weights_specs(),
        out_specs=[pl.BlockSpec((1, tq, D_MODEL), lambda b, t: (b, t, 0)),
                   pl.BlockSpec((1, WINDOW, KV_W), lambda b, t: (b, 0, 0)),
                   pl.BlockSpec((1, WINDOW, KV_W), lambda b, t: (b, 0, 0)),
                   pl.BlockSpec((1, CONV_W - 1, D_MODEL), lambda b, t: (b, 0, 0))],
        out_shape=[jax.ShapeDtypeStruct((B, S, D_MODEL), f32),
                   jax.ShapeDtypeStruct((B, WINDOW, KV_W), f32),
                   jax.ShapeDtypeStruct((B, WINDOW, KV_W), f32),
                   jax.ShapeDtypeStruct((B, CONV_W - 1, D_MODEL), f32)],
        scratch_shapes=_branch_scratch(1, tq, True),
        compiler_params=pltpu.CompilerParams(
            dimension_semantics=("arbitrary", "arbitrary"), vmem_limit_bytes=VMEM_LIMIT),
        name="branch_prompt",
    )(x, mod, *weights)


def _branch_sample(x, mod, ck, cv, cc, weights, nb):
    B, L, _ = x.shape
    body = functools.partial(_branch_body, is_prompt=False, nb=nb, L=L, n_tiles=B // nb)
    seq3 = lambda i: (i, 0, 0)
    return pl.pallas_call(
        body,
        grid=(B // nb,),
        in_specs=[pl.BlockSpec((nb, L, D_MODEL), seq3),
                  pl.BlockSpec((nb, 6, D_MODEL), seq3),
                  pl.BlockSpec((nb, WINDOW, KV_W), seq3),
                  pl.BlockSpec((nb, WINDOW, KV_W), seq3),
                  pl.BlockSpec((nb, CONV_W - 1, D_MODEL), seq3)] + _branch_weights_specs(),
        out_specs=[pl.BlockSpec((nb, L, D_MODEL), seq3),
                   pl.BlockSpec((nb, WINDOW, KV_W), seq3),
                   pl.BlockSpec((nb, WINDOW, KV_W), seq3),
                   pl.BlockSpec((nb, CONV_W - 1, D_MODEL), seq3)],
        out_shape=[jax.ShapeDtypeStruct((B, L, D_MODEL), f32),
                   jax.ShapeDtypeStruct((B, WINDOW, KV_W), f32),
                   jax.ShapeDtypeStruct((B, WINDOW, KV_W), f32),
                   jax.ShapeDtypeStruct((B, CONV_W - 1, D_MODEL), f32)],
        scratch_shapes=_branch_scratch(nb, L, False),
        compiler_params=pltpu.CompilerParams(
            dimension_semantics=("arbitrary",), vmem_limit_bytes=VMEM_LIMIT),
        name="branch_sample",
    )(x, mod, ck, cv, cc, *weights)


def _ffn_body(x1_ref, mod_ref, g2_ref, fg_ref, wup_ref, wdn_ref, y_ref, h_s, act_s, *, nb, L):
    T = nb * L
    for s in range(nb):
        def norm_rows(r0, s=s):
            x = x1_ref[s, pl.ds(r0, ROW_BLK), :]
            hh = _rms(x, g2_ref[...]) * (1.0 + mod_ref[s, 4:5, :]) + mod_ref[s, 3:4, :]
            h_s[pl.ds(s * L + r0, ROW_BLK), :] = hh.astype(bf16)
        _for_rows(L, ROW_BLK, norm_rows)
    hv = h_s[...]
    for c in range(D_FF // FF_BLK):
        gate = _mm(hv, wup_ref, c * FF_BLK, FF_BLK)
        up = _mm(hv, wup_ref, D_FF + c * FF_BLK, FF_BLK)
        act_s[:, c * FF_BLK:(c + 1) * FF_BLK] = (gate * _sigmoid(gate) * up).astype(bf16)
    av = act_s[...]
    for c in range(D_MODEL // COL_BLK):
        d = _mm(av, wdn_ref, c * COL_BLK, COL_BLK)
        for s in range(nb):
            cols = slice(c * COL_BLK, (c + 1) * COL_BLK)
            y_ref[s, :, cols] = x1_ref[s, :, cols] + mod_ref[s, 5:6, cols] * d[s * L:(s + 1) * L]
    for s in range(nb):
        def fin_rows(r0, s=s):
            x2 = y_ref[s, pl.ds(r0, ROW_BLK), :]
            y_ref[s, pl.ds(r0, ROW_BLK), :] = _rms(x2, fg_ref[...])
        _for_rows(L, ROW_BLK, fin_rows)


def _ffn(x1, mod, g2, fg, wup, wdn, nb, L, name):
    B, S, _ = x1.shape
    n_tiles = S // L
    if nb == 1:
        grid = (B, n_tiles)
        xmap = lambda b, t: (b, t, 0)
        mmap = lambda b, t: (b, 0, 0)
    else:
        grid = (B // nb,)
        xmap = mmap = lambda i: (i, 0, 0)
    body = functools.partial(_ffn_body, nb=nb, L=L)
    T = nb * L
    return pl.pallas_call(
        body,
        grid=grid,
        in_specs=[pl.BlockSpec((nb, L, D_MODEL), xmap),
                  pl.BlockSpec((nb, 6, D_MODEL), mmap),
                  _const_spec((1, D_MODEL)),
                  _const_spec((1, D_MODEL)),
                  _const_spec((D_MODEL, 2 * D_FF)),
                  _const_spec((D_FF, D_MODEL))],
        out_specs=pl.BlockSpec((nb, L, D_MODEL), xmap),
        out_shape=jax.ShapeDtypeStruct(x1.shape, f32),
        scratch_shapes=[pltpu.VMEM((T, D_MODEL), bf16), pltpu.VMEM((T, D_FF), bf16)],
        compiler_params=pltpu.CompilerParams(
            dimension_semantics=("arbitrary",) * len(grid), vmem_limit_bytes=VMEM_LIMIT),
        name=name,
    )(x1, mod, g2, fg, wup, wdn)


def kernel(x_prompt, x_sample, cache_k, cache_v, cache_conv, c_prompt, c_sample, rel_table, w_ada, b_ada, norm1_g, norm2_g, w_in, sink, w_attn_out, dw_w, dw_b, conv_ln_g, conv_ln_b, w_conv_out, w_out, w_ffn_up, w_ffn_down, final_g):
    assert w_ada.shape[0] == 1, "single-layer kernel"
    B, S, _ = x_prompt.shape
    DB, DS, _ = x_sample.shape
    TQ = 512
    NB = 4

    bias = _bias_table(rel_table)
    n_seq = B + DB
    pad = (-n_seq) % 8
    c_all = jnp.concatenate([c_prompt, c_sample, jnp.zeros((pad, D_MODEL), f32)], axis=0)
    mod = _modulation(c_all, w_ada[0], b_ada).reshape(n_seq + pad, 6, D_MODEL)
    mod_p, mod_s = mod[:B], mod[B:B + DB]

    row = lambda v: v.reshape(1, D_MODEL)
    w_in_b = w_in[0].astype(bf16)
    w_k = w_in_b[:, O_K:O_K + KV_W].reshape(D_MODEL, KV_HEADS, 1, HEAD_DIM)
    w_kd = jnp.broadcast_to(w_k, (D_MODEL, KV_HEADS, 2, HEAD_DIM)).reshape(D_MODEL, KV_HEADS * LANES)
    w_vt = w_in_b[:, O_V:O_V + KV_W].T
    weights = (bias, sink[0], row(norm1_g[0]), w_in_b, w_kd, w_vt, w_attn_out[0].astype(bf16),
               w_conv_out[0].astype(bf16), w_out[0].astype(bf16), dw_w[0], row(dw_b[0]),
               row(conv_ln_g[0]), row(conv_ln_b[0]))
    wup = w_ffn_up[0].astype(bf16)
    wdn = w_ffn_down[0].astype(bf16)
    g2 = row(norm2_g[0])
    fg = row(final_g)

    x1p, nkp, nvp, ncp = _branch_prompt(x_prompt, mod_p, weights, TQ)
    ck = cache_k[0].reshape(DB, WINDOW, KV_W)
    cv = cache_v[0].reshape(DB, WINDOW, KV_W)
    x1s, nks, nvs, ncs = _branch_sample(x_sample, mod_s, ck, cv, cache_conv[0], weights, NB)

    y_p = _ffn(x1p, mod_p, g2, fg, wup, wdn, 1, TQ, "ffn_prompt")
    y_s = _ffn(x1s, mod_s, g2, fg, wup, wdn, NB, DS, "ffn_sample")

    kv5 = lambda a: a.reshape(1, a.shape[0], WINDOW, KV_HEADS, HEAD_DIM)
    return (y_p, y_s, kv5(nkp), kv5(nvp), ncp[None], kv5(nks), kv5(nvs), ncs[None])
```

```python
import functools
import math

import jax
import jax.numpy as jnp
from jax import lax
from jax.experimental import pallas as pl
from jax.experimental.pallas import tpu as pltpu

f32 = jnp.float32
bf16 = jnp.bfloat16

D_MODEL = 1024
N_HEADS = 16
KV_HEADS = 4
HEAD_DIM = 64
GROUP = N_HEADS // KV_HEADS
WINDOW = 128
CHUNK = 64
BAND = WINDOW + CHUNK
CONV_W = 31
D_FF = 2816
NUM_BUCKETS = 32
MAX_DISTANCE = 128
EPS = 1e-6
NEG = -1e30
Q_W = N_HEADS * HEAD_DIM
KV_W = KV_HEADS * HEAD_DIM
IN_W = Q_W + 2 * KV_W + 4 * D_MODEL
O_K = Q_W
O_V = O_K + KV_W
O_GLU_A = O_V + KV_W
O_GLU_G = O_GLU_A + D_MODEL
O_GA = O_GLU_G + D_MODEL
O_GB = O_GA + D_MODEL

LANES = 128
N_SLABS = D_MODEL // LANES
PAIR = 2 * CHUNK
GQ = GROUP * CHUNK
HIST = 32
ROW_BLK = 32
COL_BLK = 256
FF_BLK = 256
VMEM_LIMIT = 56 * 1024 * 1024

_NT = (((1,), (1,)), ((), ()))


def _sigmoid(x):
    return 1.0 / (1.0 + jnp.exp(-x))


def _rms(x, g):
    ms = jnp.mean(x * x, axis=-1, keepdims=True)
    return x * lax.rsqrt(ms + EPS) * g


def _for_rows(n_rows, blk, body, unroll=4):
    def step(i, c):
        body(pl.multiple_of(i * blk, blk))
        return c
    lax.fori_loop(0, n_rows // blk, step, 0, unroll=unroll)


def _mm(a, w_ref, c0, width):
    return jnp.dot(a, w_ref[:, c0:c0 + width], preferred_element_type=f32)


def _head_row(vals, k, lane):
    row = jnp.full((1, GQ), vals(GROUP * k + GROUP - 1), f32)
    for g in range(GROUP - 2, -1, -1):
        row = jnp.where(lane < CHUNK * (g + 1), vals(GROUP * k + g), row)
    return row


def _rel_bucket(rel):
    nb = NUM_BUCKETS // 2
    max_exact = nb // 2
    ret = (rel > 0).astype(jnp.int32) * nb
    n = jnp.abs(rel)
    nf = jnp.maximum(n, 1).astype(f32)
    large = max_exact + (jnp.log(nf / max_exact) / math.log(MAX_DISTANCE / max_exact)
                         * (nb - max_exact)).astype(jnp.int32)
    large = jnp.minimum(large, nb - 1)
    return ret + jnp.where(n < max_exact, n, large)


def _bias_body(idx_ref, tab_ref, o_ref):
    idx = idx_ref[...]
    lane = lax.broadcasted_iota(jnp.int32, (1, GQ), 1)
    for k in range(KV_HEADS):
        acc = jnp.zeros((BAND, GQ), f32)
        for b in range(NUM_BUCKETS):
            acc = jnp.where(idx == b, _head_row(lambda h: tab_ref[b, h], k, lane), acc)
        o_ref[k] = acc


def _bias_table(rel_table):
    kj = jnp.arange(BAND, dtype=jnp.int32)
    rel = kj[None, :] - WINDOW - jnp.arange(CHUNK, dtype=jnp.int32)[:, None]
    idx_t = jnp.tile(_rel_bucket(rel).T, (1, GROUP))
    return pl.pallas_call(
        _bias_body,
        in_specs=[pl.BlockSpec(memory_space=pltpu.VMEM), pl.BlockSpec(memory_space=pltpu.SMEM)],
        out_specs=pl.BlockSpec(memory_space=pltpu.VMEM),
        out_shape=jax.ShapeDtypeStruct((KV_HEADS, BAND, GQ), f32),
        name="rel_bias",
    )(idx_t, rel_table)


def _mod_body(c_ref, w_ref, b_ref, o_ref):
    c = c_ref[...]
    s = c * _sigmoid(c)
    o_ref[...] = jnp.dot(s.astype(bf16), w_ref[...].astype(bf16), preferred_element_type=f32) + b_ref[...]


def _modulation(c_all, w_ada, b_ada):
    rows = c_all.shape[0]
    blk = 512
    return pl.pallas_call(
        _mod_body,
        grid=(6 * D_MODEL // blk,),
        in_specs=[pl.BlockSpec((rows, D_MODEL), lambda j: (0, 0)),
                  pl.BlockSpec((D_MODEL, blk), lambda j: (0, j)),
                  pl.BlockSpec((1, blk), lambda j: (0, j))],
        out_specs=pl.BlockSpec((rows, blk), lambda j: (0, j)),
        out_shape=jax.ShapeDtypeStruct((rows, 6 * D_MODEL), f32),
        name="adaln_mod",
    )(c_all, w_ada, b_ada)


def _attn_pair(chunks, bias_ref, sink_rows, st_s, pb_s, slot0, attT_ref, grp):
    lane_lo = lax.broadcasted_iota(jnp.int32, (CHUNK, LANES), 1) < HEAD_DIM
    n_pad = 2 * LANES - BAND
    for par, (q_c, kd_bands, v_win, pad_top, valid) in enumerate(chunks):
        for k in range(KV_HEADS):
            blocks = []
            for p in range(2):
                qc = q_c[:, (2 * k + p) * LANES:(2 * k + p + 1) * LANES]
                blocks.append(jnp.where(lane_lo, qc, jnp.zeros_like(qc)))
                blocks.append(jnp.where(lane_lo, jnp.zeros_like(qc), qc))
            qm = jnp.concatenate(blocks, axis=0)
            st_s[slot0 + par * KV_HEADS + k] = lax.dot_general(kd_bands[k], qm, _NT, preferred_element_type=f32)
    for par, (q_c, kd_bands, v_win, pad_top, valid) in enumerate(chunks):
        for k in range(KV_HEADS):
            i = slot0 + par * KV_HEADS + k
            st = st_s[i] + bias_ref[k]
            if valid is not None:
                st = jnp.where(valid, st, NEG)
            m = jnp.maximum(jnp.max(st, axis=0, keepdims=True), sink_rows[k])
            p_ = jnp.exp(st - m)
            den = jnp.sum(p_, axis=0, keepdims=True) + jnp.exp(sink_rows[k] - m)
            r0 = n_pad if pad_top else 0
            pb_s[i, r0:r0 + BAND, :] = (p_ / den).astype(bf16)
            z0 = 0 if pad_top else BAND
            pb_s[i, z0:z0 + n_pad, :] = jnp.zeros((n_pad, GQ), bf16)
    for par, (q_c, kd_bands, v_win, pad_top, valid) in enumerate(chunks):
        for k in range(KV_HEADS):
            i = slot0 + par * KV_HEADS + k
            ot = jnp.dot(v_win[k * HEAD_DIM:(k + 1) * HEAD_DIM, :], pb_s[i], preferred_element_type=f32)
            for g in range(GROUP):
                h = GROUP * k + g
                attT_ref[grp, h * HEAD_DIM:(h + 1) * HEAD_DIM, par * CHUNK:(par + 1) * CHUNK] = (
                    ot[:, g * CHUNK:(g + 1) * CHUNK])


def _conv_block(zb_ref, s, j, i, dww_ref, cb_ref, out_row0):
    r0 = i * 32
    accs = [jnp.zeros((8, LANES), f32) for _ in range(4)]
    for k in range(CONV_W):
        wk = dww_ref[k:k + 1, j * LANES:(j + 1) * LANES]
        for a in range(4):
            start = r0 + (a // 2) * 16 + (a % 2) + (HIST - (CONV_W - 1)) + k
            accs[a] = accs[a] + zb_ref[s, j, pl.ds(start, 8, stride=2), :] * wk
    for a in range(4):
        start = out_row0 + r0 + (a // 2) * 16 + (a % 2)
        cb_ref[j, pl.ds(start, 8, stride=2), :] = accs[a]


def _interleave(pieces, blocks, blocks_first=False):
    order = [((n + 0.5) / len(pieces), int(blocks_first), fn) for n, fn in enumerate(pieces)]
    order += [((n + 0.5) / len(blocks), int(not blocks_first), fn) for n, fn in enumerate(blocks)]
    for _, _, fn in sorted(order, key=lambda item: item[:2]):
        fn()


def _branch_body(*refs, is_prompt, nb, L, n_tiles):
    if is_prompt:
        (x_ref, mod_ref, bias_ref, sink_ref, g1_ref, win_ref, wkd_ref, wvt_ref, wao_ref, wco_ref, wo_ref,
         dww_ref, dwb_ref, lng_ref, lnb_ref,
         x1_ref, nk_ref, nv_ref, nc_ref,
         h_s, q_s, kd_s, vt_s, st_s, pb_s, attT_s, att_s, zb_s, cb_s, cbf_s, ga_s, gb_s, acc_s) = refs
        ck_ref = cv_ref = cc_ref = None
    else:
        (x_ref, mod_ref, ck_ref, cv_ref, cc_ref, bias_ref, sink_ref, g1_ref, win_ref, wkd_ref, wvt_ref,
         wao_ref, wco_ref, wo_ref, dww_ref, dwb_ref, lng_ref, lnb_ref,
         x1_ref, nk_ref, nv_ref, nc_ref,
         h_s, q_s, kd_s, vt_s, st_s, pb_s, attT_s, att_s, zb_s, cb_s, cbf_s, ga_s, gb_s, acc_s) = refs
    T = nb * L
    n_grp = T // PAIR

    if is_prompt:
        t = pl.program_id(1)

        @pl.when(t == 0)
        def _():
            kd_s[:, 0:WINDOW, :] = jnp.zeros((KV_HEADS, WINDOW, LANES), bf16)
            vt_s[0] = jnp.zeros((KV_W, LANES), bf16)
            zb_s[0, :, 0:HIST, :] = jnp.zeros((N_SLABS, HIST, LANES), f32)

    for s in range(nb):
        def norm_rows(r0, s=s):
            x = x_ref[s, pl.ds(r0, ROW_BLK), :]
            hh = _rms(x, g1_ref[...]) * (1.0 + mod_ref[s, 1:2, :]) + mod_ref[s, 0:1, :]
            h_s[pl.ds(s * L + r0, ROW_BLK), :] = hh.astype(bf16)
        _for_rows(L, ROW_BLK, norm_rows)

    hv = h_s[...]
    n_cc = D_MODEL // COL_BLK

    if not is_prompt:
        for s in range(nb):
            for j in range(N_SLABS):
                zb_s[s, j, HIST - (CONV_W - 1):HIST, :] = cc_ref[s, :, j * LANES:(j + 1) * LANES]
    def glu_piece(c):
        za = _mm(hv, win_ref, O_GLU_A + c * COL_BLK, COL_BLK)
        zg = _mm(hv, win_ref, O_GLU_G + c * COL_BLK, COL_BLK)
        zc = za * _sigmoid(zg)
        for jj in range(COL_BLK // LANES):
            j = c * (COL_BLK // LANES) + jj
            z = zc[:, jj * LANES:(jj + 1) * LANES]
            for s in range(nb):
                zb_s[s, j, HIST:HIST + L, :] = z[s * L:(s + 1) * L]
                if not is_prompt:
                    nc_ref[s, :, j * LANES:(j + 1) * LANES] = z[s * L + L - (CONV_W - 1):(s + 1) * L]

    glu_piece(0)

    if not is_prompt:
        for s in range(nb):
            ck = ck_ref[s]
            for k in range(KV_HEADS):
                ckk = ck[:, k * HEAD_DIM:(k + 1) * HEAD_DIM].astype(bf16)
                kd_s[s, k, 0:WINDOW, :] = jnp.concatenate([ckk, ckk], axis=1)
            vt_s[s, :, 0:WINDOW] = cv_ref[s].T.astype(bf16)
            vt_s[s, :, BAND:2 * LANES] = jnp.zeros((KV_W, 2 * LANES - BAND), bf16)
            nk_ref[s, 0:WINDOW - L, :] = ck[L:WINDOW, :]
            nv_ref[s, 0:WINDOW - L, :] = cv_ref[s, L:WINDOW, :]

    def q_piece(c):
        qf = _mm(hv, win_ref, c * COL_BLK, COL_BLK)
        q_s[:, c * COL_BLK:(c + 1) * COL_BLK] = (qf * (HEAD_DIM ** -0.5)).astype(bf16)

    def kd_piece(c):
        kdc = _mm(hv, wkd_ref, c * COL_BLK, COL_BLK)
        for kk in range(COL_BLK // LANES):
            k = c * (COL_BLK // LANES) + kk
            kdk = kdc[:, kk * LANES:(kk + 1) * LANES].astype(bf16)
            if is_prompt:
                kd_s[k, WINDOW:WINDOW + T, :] = kdk
            else:
                for s in range(nb):
                    kd_s[s, k, WINDOW:BAND, :] = kdk[s * L:(s + 1) * L]

    def vt_piece():
        vtf = lax.dot_general(wvt_ref[...], hv, _NT, preferred_element_type=f32).astype(bf16)
        if is_prompt:
            for g in range(n_grp):
                vt_s[1 + g] = vtf[:, g * LANES:(g + 1) * LANES]
        else:
            for s in range(nb):
                vt_s[s, :, WINDOW:BAND] = vtf[:, s * L:(s + 1) * L]

    def gate_piece(dst, off, c):
        cols = slice(c * COL_BLK, (c + 1) * COL_BLK)
        dst[:, cols] = _sigmoid(_mm(hv, win_ref, off + c * COL_BLK, COL_BLK))

    def new_kv_piece(dst, off):
        kvf = _mm(hv, win_ref, off, KV_W)
        for s in range(nb):
            dst[s, WINDOW - L:WINDOW, :] = kvf[s * L:(s + 1) * L]

    pieces = [functools.partial(glu_piece, c) for c in range(1, n_cc)]
    pieces += [functools.partial(q_piece, c) for c in range(Q_W // COL_BLK)]
    pieces += [functools.partial(kd_piece, c) for c in range(KV_HEADS * LANES // COL_BLK)]
    pieces += [vt_piece]
    if not is_prompt:
        pieces += [functools.partial(new_kv_piece, nk_ref, O_K), functools.partial(new_kv_piece, nv_ref, O_V)]
    conv_blocks = [functools.partial(_conv_block, zb_s, s, j, i, dww_ref, cb_s, s * L)
                   for j in range(N_SLABS) for s in range(nb) for i in range(L // 32)]
    assert len(pieces) >= 2 * n_cc
    _interleave(pieces, conv_blocks)

    lane_gq = lax.broadcasted_iota(jnp.int32, (1, GQ), 1)
    sink_rows = [_head_row(lambda h: sink_ref[h], k, lane_gq) for k in range(KV_HEADS)]
    def attn_block(c2):
        chunks = []
        if is_prompt:
            v_win = jnp.concatenate([vt_s[c2], vt_s[c2 + 1]], axis=1)
            for par in range(2):
                r = c2 * PAIR + par * CHUNK
                pos = t * T + r - WINDOW + lax.broadcasted_iota(jnp.int32, (BAND, 1), 0)
                chunks.append((q_s[r:r + CHUNK, :],
                               [kd_s[k, r:r + BAND, :] for k in range(KV_HEADS)],
                               v_win, par == 1, pos >= 0))
        else:
            for par in range(2):
                s = 2 * c2 + par
                chunks.append((q_s[s * CHUNK:(s + 1) * CHUNK, :],
                               [kd_s[s, k] for k in range(KV_HEADS)],
                               vt_s[s], False, None))
        _attn_pair(chunks, bias_ref, sink_rows, st_s, pb_s, (c2 % 2) * 2 * KV_HEADS, attT_s, c2)
        att_s[c2 * PAIR:(c2 + 1) * PAIR, :] = attT_s[c2].T.astype(bf16)

    def ln_block(i):
        rows = slice(i * ROW_BLK, (i + 1) * ROW_BLK)
        ys = [cb_s[j, rows, :] + dwb_ref[:, j * LANES:(j + 1) * LANES] for j in range(N_SLABS)]
        tot = ys[0]
        for j in range(1, N_SLABS):
            tot = tot + ys[j]
        mu = jnp.sum(tot, axis=-1, keepdims=True) * (1.0 / D_MODEL)
        ds = [y - mu for y in ys]
        sq = ds[0] * ds[0]
        for j in range(1, N_SLABS):
            sq = sq + ds[j] * ds[j]
        var = jnp.sum(sq, axis=-1, keepdims=True) * (1.0 / D_MODEL)
        rstd = lax.rsqrt(var + EPS)
        for j in range(N_SLABS):
            y = ds[j] * rstd * lng_ref[:, j * LANES:(j + 1) * LANES] + lnb_ref[:, j * LANES:(j + 1) * LANES]
            cbf_s[rows, j * LANES:(j + 1) * LANES] = (y * _sigmoid(y)).astype(bf16)

    def yb_piece(c):
        cols = slice(c * COL_BLK, (c + 1) * COL_BLK)
        acc_s[:, cols] = gb_s[:, cols] * _mm(cbf_s[...], wco_ref, c * COL_BLK, COL_BLK)

    _interleave([functools.partial(gate_piece, ga_s, O_GA, c) for c in range(n_cc)],
                [functools.partial(ln_block, i) for i in range(T // ROW_BLK)])
    _interleave([functools.partial(gate_piece, gb_s, O_GB, c) for c in range(n_cc)]
                + [functools.partial(yb_piece, c) for c in range(n_cc)],
                [functools.partial(attn_block, c2) for c2 in range(n_grp)], blocks_first=True)

    av = att_s[...]
    for c in range(n_cc):
        cols = slice(c * COL_BLK, (c + 1) * COL_BLK)
        acc_s[:, cols] += ga_s[:, cols] * _mm(av, wao_ref, c * COL_BLK, COL_BLK)

    mv = acc_s[...].astype(bf16)
    for c in range(n_cc):
        m = _mm(mv, wo_ref, c * COL_BLK, COL_BLK)
        for s in range(nb):
            cols = slice(c * COL_BLK, (c + 1) * COL_BLK)
            x1_ref[s, :, cols] = x_ref[s, :, cols] + mod_ref[s, 2:3, cols] * m[s * L:(s + 1) * L]

    if is_prompt:
        @pl.when(t == n_tiles - 1)
        def _():
            tail = h_s[T - WINDOW:T, :]
            nk_ref[0] = _mm(tail, win_ref, O_K, KV_W)
            nv_ref[0] = _mm(tail, win_ref, O_V, KV_W)
            for j in range(N_SLABS):
                nc_ref[0, :, j * LANES:(j + 1) * LANES] = zb_s[0, j, HIST + L - (CONV_W - 1):HIST + L, :]

        kd_s[:, 0:WINDOW, :] = kd_s[:, T:T + WINDOW, :]
        vt_s[0] = vt_s[n_grp]
        zb_s[0, :, 0:HIST, :] = zb_s[0, :, L:L + HIST, :]


def _const_spec(shape):
    nd = len(shape)
    return pl.BlockSpec(shape, lambda *_: (0,) * nd, pipeline_mode=pl.Buffered(1))


def _branch_weights_specs():
    return [
        _const_spec((KV_HEADS, BAND, GQ)),
        pl.BlockSpec(memory_space=pltpu.SMEM),
        _const_spec((1, D_MODEL)),
        _const_spec((D_MODEL, IN_W)),
        _const_spec((D_MODEL, KV_HEADS * LANES)),
        _const_spec((KV_W, D_MODEL)),
        _const_spec((Q_W, D_MODEL)),
        _const_spec((D_MODEL, D_MODEL)),
        _const_spec((D_MODEL, D_MODEL)),
        _const_spec((CONV_W, D_MODEL)),
        _const_spec((1, D_MODEL)),
        _const_spec((1, D_MODEL)),
        _const_spec((1, D_MODEL)),
    ]


def _branch_scratch(nb, L, is_prompt):
    T = nb * L
    if is_prompt:
        kd_shape = (KV_HEADS, WINDOW + T, LANES)
        vt_shape = (1 + T // PAIR, KV_W, LANES)
    else:
        kd_shape = (nb, KV_HEADS, BAND, LANES)
        vt_shape = (nb, KV_W, 2 * LANES)
    return [
        pltpu.VMEM((T, D_MODEL), bf16),
        pltpu.VMEM((T, Q_W), bf16),
        pltpu.VMEM(kd_shape, bf16),
        pltpu.VMEM(vt_shape, bf16),
        pltpu.VMEM((4 * KV_HEADS, BAND, GQ), f32),
        pltpu.VMEM((4 * KV_HEADS, 2 * LANES, GQ), bf16),
        pltpu.VMEM((T // PAIR, Q_W, PAIR), f32),
        pltpu.VMEM((T, Q_W), bf16),
        pltpu.VMEM((nb, N_SLABS, HIST + L, LANES), f32),
        pltpu.VMEM((N_SLABS, T, LANES), f32),
        pltpu.VMEM((T, D_MODEL), bf16),
        pltpu.VMEM((T, D_MODEL), f32),
        pltpu.VMEM((T, D_MODEL), f32),
        pltpu.VMEM((T, D_MODEL), f32),
    ]


def _branch_prompt(x, mod, weights, tq):
    B, S, _ = x.shape
    n_tiles = S // tq
    body = functools.partial(_branch_body, is_prompt=True, nb=1, L=tq, n_tiles=n_tiles)
    return pl.pallas_call(
        body,
        grid=(B, n_tiles),
        in_specs=[pl.BlockSpec((1, tq, D_MODEL), lambda b, t: (b, t, 0)),
                  pl.BlockSpec((1, 6, D_MODEL), lambda b, t: (b, 0, 0))] + _branch_weights_specs(),
        out_specs=[pl.BlockSpec((1, tq, D_MODEL), lambda b, t: (b, t, 0)),
                   pl.BlockSpec((1, WINDOW, KV_W), lambda b, t: (b, 0, 0)),
                   pl.BlockSpec((1, WINDOW, KV_W), lambda b, t: (b, 0, 0)),
                   pl.BlockSpec((1, CONV_W - 1, D_MODEL), lambda b, t: (b, 0, 0))],
        out_shape=[jax.ShapeDtypeStruct((B, S, D_MODEL), f32),
                   jax.ShapeDtypeStruct((B, WINDOW, KV_W), f32),
                   jax.ShapeDtypeStruct((B, WINDOW, KV_W), f32),
                   jax.ShapeDtypeStruct((B, CONV_W - 1, D_MODEL), f32)],
        scratch_shapes=_branch_scratch(1, tq, True),
        compiler_params=pltpu.CompilerParams(
            dimension_semantics=("arbitrary", "arbitrary"), vmem_limit_bytes=VMEM_LIMIT),
        name="branch_prompt",
    )(x, mod, *weights)


def _branch_sample(x, mod, ck, cv, cc, weights, nb):
    B, L, _ = x.shape
    body = functools.partial(_branch_body, is_prompt=False, nb=nb, L=L, n_tiles=B // nb)
    seq3 = lambda i: (i, 0, 0)
    return pl.pallas_call(
        body,
        grid=(B // nb,),
        in_specs=[pl.BlockSpec((nb, L, D_MODEL), seq3),
                  pl.BlockSpec((nb, 6, D_MODEL), seq3),
                  pl.BlockSpec((nb, WINDOW, KV_W), seq3),
                  pl.BlockSpec((nb, WINDOW, KV_W), seq3),
                  pl.BlockSpec((nb, CONV_W - 1, D_MODEL), seq3)] + _branch_weights_specs(),
        out_specs=[pl.BlockSpec((nb, L, D_MODEL), seq3),
                   pl.BlockSpec((nb, WINDOW, KV_W), seq3),
                   pl.BlockSpec((nb, WINDOW, KV_W), seq3),
                   pl.BlockSpec((nb, CONV_W - 1, D_MODEL), seq3)],
        out_shape=[jax.ShapeDtypeStruct((B, L, D_MODEL), f32),
                   jax.ShapeDtypeStruct((B, WINDOW, KV_W), f32),
                   jax.ShapeDtypeStruct((B, WINDOW, KV_W), f32),
                   jax.ShapeDtypeStruct((B, CONV_W - 1, D_MODEL), f32)],
        scratch_shapes=_branch_scratch(nb, L, False),
        compiler_params=pltpu.CompilerParams(
            dimension_semantics=("arbitrary",), vmem_limit_bytes=VMEM_LIMIT),
        name="branch_sample",
    )(x, mod, ck, cv, cc, *weights)


def _ffn_body(x1_ref, mod_ref, g2_ref, fg_ref, wup_ref, wdn_ref, y_ref, h_s, act_s, *, nb, L):
    T = nb * L
    for s in range(nb):
        def norm_rows(r0, s=s):
            x = x1_ref[s, pl.ds(r0, ROW_BLK), :]
            hh = _rms(x, g2_ref[...]) * (1.0 + mod_ref[s, 4:5, :]) + mod_ref[s, 3:4, :]
            h_s[pl.ds(s * L + r0, ROW_BLK), :] = hh.astype(bf16)
        _for_rows(L, ROW_BLK, norm_rows)
    hv = h_s[...]
    for c in range(D_FF // FF_BLK):
        gate = _mm(hv, wup_ref, c * FF_BLK, FF_BLK)
        up = _mm(hv, wup_ref, D_FF + c * FF_BLK, FF_BLK)
        act_s[:, c * FF_BLK:(c + 1) * FF_BLK] = (gate * _sigmoid(gate) * up).astype(bf16)
    av = act_s[...]
    for c in range(D_MODEL // COL_BLK):
        d = _mm(av, wdn_ref, c * COL_BLK, COL_BLK)
        for s in range(nb):
            cols = slice(c * COL_BLK, (c + 1) * COL_BLK)
            y_ref[s, :, cols] = x1_ref[s, :, cols] + mod_ref[s, 5:6, cols] * d[s * L:(s + 1) * L]
    for s in range(nb):
        def fin_rows(r0, s=s):
            x2 = y_ref[s, pl.ds(r0, ROW_BLK), :]
            y_ref[s, pl.ds(r0, ROW_BLK), :] = _rms(x2, fg_ref[...])
        _for_rows(L, ROW_BLK, fin_rows)


def _ffn(x1, mod, g2, fg, wup, wdn, nb, L, name):
    B, S, _ = x1.shape
    n_tiles = S // L
    if nb == 1:
        grid = (B, n_tiles)
        xmap = lambda b, t: (b, t, 0)
        mmap = lambda b, t: (b, 0, 0)
    else:
        grid = (B // nb,)
        xmap = mmap = lambda i: (i, 0, 0)
    body = functools.partial(_ffn_body, nb=nb, L=L)
    T = nb * L
    return pl.pallas_call(
        body,
        grid=grid,
        in_specs=[pl.BlockSpec((nb, L, D_MODEL), xmap),
                  pl.BlockSpec((nb, 6, D_MODEL), mmap),
                  _const_spec((1, D_MODEL)),
                  _const_spec((1, D_MODEL)),
                  _const_spec((D_MODEL, 2 * D_FF)),
                  _const_spec((D_FF, D_MODEL))],
        out_specs=pl.BlockSpec((nb, L, D_MODEL), xmap),
        out_shape=jax.ShapeDtypeStruct(x1.shape, f32),
        scratch_shapes=[pltpu.VMEM((T, D_MODEL), bf16), pltpu.VMEM((T, D_FF), bf16)],
        compiler_params=pltpu.CompilerParams(
            dimension_semantics=("arbitrary",) * len(grid), vmem_limit_bytes=VMEM_LIMIT),
        name=name,
    )(x1, mod, g2, fg, wup, wdn)


def kernel(x_prompt, x_sample, cache_k, cache_v, cache_conv, c_prompt, c_sample, rel_table, w_ada, b_ada, norm1_g, norm2_g, w_in, sink, w_attn_out, dw_w, dw_b, conv_ln_g, conv_ln_b, w_conv_out, w_out, w_ffn_up, w_ffn_down, final_g):
    assert w_ada.shape[0] == 1, "single-layer kernel"
    B, S, _ = x_prompt.shape
    DB, DS, _ = x_sample.shape
    TQ = 512
    NB = 4

    bias = _bias_table(rel_table)
    n_seq = B + DB
    pad = (-n_seq) % 8
    c_all = jnp.concatenate([c_prompt, c_sample, jnp.zeros((pad, D_MODEL), f32)], axis=0)
    mod = _modulation(c_all, w_ada[0], b_ada).reshape(n_seq + pad, 6, D_MODEL)
    mod_p, mod_s = mod[:B], mod[B:B + DB]

    row = lambda v: v.reshape(1, D_MODEL)
    w_in_b = w_in[0].astype(bf16)
    w_k = w_in_b[:, O_K:O_K + KV_W].reshape(D_MODEL, KV_HEADS, 1, HEAD_DIM)
    w_kd = jnp.broadcast_to(w_k, (D_MODEL, KV_HEADS, 2, HEAD_DIM)).reshape(D_MODEL, KV_HEADS * LANES)
    w_vt = w_in_b[:, O_V:O_V + KV_W].T
    weights = (bias, sink[0], row(norm1_g[0]), w_in_b, w_kd, w_vt, w_attn_out[0].astype(bf16),
               w_conv_out[0].astype(bf16), w_out[0].astype(bf16), dw_w[0], row(dw_b[0]),
               row(conv_ln_g[0]), row(conv_ln_b[0]))
    wup = w_ffn_up[0].astype(bf16)
    wdn = w_ffn_down[0].astype(bf16)
    g2 = row(norm2_g[0])
    fg = row(final_g)

    x1p, nkp, nvp, ncp = _branch_prompt(x_prompt, mod_p, weights, TQ)
    ck = cache_k[0].reshape(DB, WINDOW, KV_W)
    cv = cache_v[0].reshape(DB, WINDOW, KV_W)
    x1s, nks, nvs, ncs = _branch_sample(x_sample, mod_s, ck, cv, cache_conv[0], weights, NB)

    y_p = _ffn(x1p, mod_p, g2, fg, wup, wdn, 1, TQ, "ffn_prompt")
    y_s = _ffn(x1s, mod_s, g2, fg, wup, wdn, NB, DS, "ffn_sample")

    kv5 = lambda a: a.reshape(1, a.shape[0], WINDOW, KV_HEADS, HEAD_DIM)
    return (y_p, y_s, kv5(nkp), kv5(nvp), ncp[None], kv5(nks), kv5(nvs), ncs[None])
```

```python
import functools
import math

import jax
import jax.numpy as jnp
from jax import lax
from jax.experimental import pallas as pl
from jax.experimental.pallas import tpu as pltpu

f32 = jnp.float32
bf16 = jnp.bfloat16

D_MODEL = 1024
N_HEADS = 16
KV_HEADS = 4
HEAD_DIM = 64
GROUP = N_HEADS // KV_HEADS
WINDOW = 128
CHUNK = 64
BAND = WINDOW + CHUNK
CONV_W = 31
D_FF = 2816
NUM_BUCKETS = 32
MAX_DISTANCE = 128
EPS = 1e-6
NEG = -1e30
Q_W = N_HEADS * HEAD_DIM
KV_W = KV_HEADS * HEAD_DIM
O_K = Q_W
O_V = O_K + KV_W
O_GLU = O_V + KV_W
O_GA = O_GLU + 2 * D_MODEL
O_GB = O_GA + D_MODEL

LANES = 128
N_SLABS = D_MODEL // LANES
PAIR = 2 * CHUNK
GQ = GROUP * CHUNK
HIST = 32
ROW_BLK = 32
COL_BLK = 256
N_CC = D_MODEL // COL_BLK
VMEM_LIMIT = 56 * 1024 * 1024
ATTN_TOKENS = 512
MIX_TOKENS = 256

_NT = (((1,), (1,)), ((), ()))


def _sigmoid(x):
    return 1.0 / (1.0 + jnp.exp(-x))


def _rms(x, g):
    ms = jnp.mean(x * x, axis=-1, keepdims=True)
    return x * lax.rsqrt(ms + EPS) * g


def _mm(a, w_ref, c0, width):
    return jnp.dot(a, w_ref[:, c0:c0 + width], preferred_element_type=f32)


def _cols(c):
    return slice(c * COL_BLK, (c + 1) * COL_BLK)


def _head_row(vals, k, lane):
    row = jnp.full((1, GQ), vals(GROUP * k + GROUP - 1), f32)
    for g in range(GROUP - 2, -1, -1):
        row = jnp.where(lane < CHUNK * (g + 1), vals(GROUP * k + g), row)
    return row


def _interleave(pieces, blocks, blocks_first=False):
    order = [((n + 0.5) / len(pieces), int(blocks_first), fn) for n, fn in enumerate(pieces)]
    order += [((n + 0.5) / len(blocks), int(not blocks_first), fn) for n, fn in enumerate(blocks)]
    for _, _, fn in sorted(order, key=lambda item: item[:2]):
        fn()


def _const_spec(shape):
    nd = len(shape)
    return pl.BlockSpec(shape, lambda *_: (0,) * nd, pipeline_mode=pl.Buffered(1))


def _rel_bucket(rel):
    nb = NUM_BUCKETS // 2
    max_exact = nb // 2
    ret = (rel > 0).astype(jnp.int32) * nb
    n = jnp.abs(rel)
    nf = jnp.maximum(n, 1).astype(f32)
    large = max_exact + (jnp.log(nf / max_exact) / math.log(MAX_DISTANCE / max_exact)
                         * (nb - max_exact)).astype(jnp.int32)
    large = jnp.minimum(large, nb - 1)
    return ret + jnp.where(n < max_exact, n, large)


def _bias_body(idx_ref, tab_ref, o_ref):
    idx = idx_ref[...]
    lane = lax.broadcasted_iota(jnp.int32, (1, GQ), 1)
    for k in range(KV_HEADS):
        acc = jnp.zeros((BAND, GQ), f32)
        for b in range(NUM_BUCKETS):
            acc = jnp.where(idx == b, _head_row(lambda h: tab_ref[b, h], k, lane), acc)
        o_ref[k] = acc


def _bias_table(rel_table):
    kj = jnp.arange(BAND, dtype=jnp.int32)
    rel = kj[None, :] - WINDOW - jnp.arange(CHUNK, dtype=jnp.int32)[:, None]
    idx_t = jnp.tile(_rel_bucket(rel).T, (1, GROUP))
    return pl.pallas_call(
        _bias_body,
        in_specs=[pl.BlockSpec(memory_space=pltpu.VMEM), pl.BlockSpec(memory_space=pltpu.SMEM)],
        out_specs=pl.BlockSpec(memory_space=pltpu.VMEM),
        out_shape=jax.ShapeDtypeStruct((KV_HEADS, BAND, GQ), f32),
        name="rel_bias",
    )(idx_t, rel_table)


def _mod_body(c_ref, w_ref, b_ref, o_ref):
    c = c_ref[...]
    s = c * _sigmoid(c)
    o_ref[...] = jnp.dot(s.astype(bf16), w_ref[...].astype(bf16), preferred_element_type=f32) + b_ref[...]


def _modulation(c_all, w_ada, b_ada):
    rows = c_all.shape[0]
    blk = 512
    return pl.pallas_call(
        _mod_body,
        grid=(6 * D_MODEL // blk,),
        in_specs=[pl.BlockSpec((rows, D_MODEL), lambda j: (0, 0)),
                  pl.BlockSpec((D_MODEL, blk), lambda j: (0, j)),
                  pl.BlockSpec((1, blk), lambda j: (0, j))],
        out_specs=pl.BlockSpec((rows, blk), lambda j: (0, j)),
        out_shape=jax.ShapeDtypeStruct((rows, 6 * D_MODEL), f32),
        name="adaln_mod",
    )(c_all, w_ada, b_ada)


def _attn_pair(chunks, bias_ref, sink_rows, st_s, pb_s, slot0, attT_ref, grp):
    lane_lo = lax.broadcasted_iota(jnp.int32, (CHUNK, LANES), 1) < HEAD_DIM
    n_pad = 2 * LANES - BAND
    for par, (q_c, kd_bands, v_win, pad_top, valid) in enumerate(chunks):
        for k in range(KV_HEADS):
            blocks = []
            for p in range(2):
                qc = q_c[:, (2 * k + p) * LANES:(2 * k + p + 1) * LANES]
                blocks.append(jnp.where(lane_lo, qc, jnp.zeros_like(qc)))
                blocks.append(jnp.where(lane_lo, jnp.zeros_like(qc), qc))
            qm = jnp.concatenate(blocks, axis=0)
            st_s[slot0 + par * KV_HEADS + k] = lax.dot_general(kd_bands[k], qm, _NT, preferred_element_type=f32)
    for par, (q_c, kd_bands, v_win, pad_top, valid) in enumerate(chunks):
        for k in range(KV_HEADS):
            i = slot0 + par * KV_HEADS + k
            st = st_s[i] + bias_ref[k]
            if valid is not None:
                st = jnp.where(valid, st, NEG)
            m = jnp.maximum(jnp.max(st, axis=0, keepdims=True), sink_rows[k])
            p_ = jnp.exp(st - m)
            den = jnp.sum(p_, axis=0, keepdims=True) + jnp.exp(sink_rows[k] - m)
            r0 = n_pad if pad_top else 0
            pb_s[i, r0:r0 + BAND, :] = (p_ / den).astype(bf16)
            z0 = 0 if pad_top else BAND
            pb_s[i, z0:z0 + n_pad, :] = jnp.zeros((n_pad, GQ), bf16)
    for par, (q_c, kd_bands, v_win, pad_top, valid) in enumerate(chunks):
        for k in range(KV_HEADS):
            i = slot0 + par * KV_HEADS + k
            ot = jnp.dot(v_win[k * HEAD_DIM:(k + 1) * HEAD_DIM, :], pb_s[i], preferred_element_type=f32)
            for g in range(GROUP):
                h = GROUP * k + g
                attT_ref[grp, h * HEAD_DIM:(h + 1) * HEAD_DIM, par * CHUNK:(par + 1) * CHUNK] = (
                    ot[:, g * CHUNK:(g + 1) * CHUNK])


def _attn_body(*refs, is_prompt, nb, L, n_tiles):
    if is_prompt:
        (x_ref, mod_ref, bias_ref, sink_ref, g1_ref, wq_ref, wkd_ref, wvt_ref, wkv_ref, wga_ref, wao_ref,
         a1_ref, nk_ref, nv_ref,
         h_s, q_s, kd_s, vt_s, st_s, pb_s, attT_s, att_s, ga_s) = refs
        ck_ref = cv_ref = None
    else:
        (x_ref, mod_ref, ck_ref, cv_ref, bias_ref, sink_ref, g1_ref, wq_ref, wkd_ref, wvt_ref, wkv_ref, wga_ref,
         wao_ref,
         a1_ref, nk_ref, nv_ref,
         h_s, q_s, kd_s, vt_s, st_s, pb_s, attT_s, att_s, ga_s) = refs
    T = nb * L
    n_grp = T // PAIR

    if is_prompt:
        t = pl.program_id(1)

        @pl.when(t == 0)
        def _():
            kd_s[:, 0:WINDOW, :] = jnp.zeros((KV_HEADS, WINDOW, LANES), bf16)
            vt_s[0] = jnp.zeros((KV_W, LANES), bf16)

    for s in range(nb):
        for i in range(L // ROW_BLK):
            rows = slice(i * ROW_BLK, (i + 1) * ROW_BLK)
            hh = _rms(x_ref[s, rows, :], g1_ref[...]) * (1.0 + mod_ref[s, 1:2, :]) + mod_ref[s, 0:1, :]
            h_s[s * L + i * ROW_BLK:s * L + (i + 1) * ROW_BLK, :] = hh.astype(bf16)
    hv = h_s[...]

    if not is_prompt:
        for s in range(nb):
            ck = ck_ref[s]
            for k in range(KV_HEADS):
                ckk = ck[:, k * HEAD_DIM:(k + 1) * HEAD_DIM].astype(bf16)
                kd_s[s, k, 0:WINDOW, :] = jnp.concatenate([ckk, ckk], axis=1)
            vt_s[s, :, 0:WINDOW] = cv_ref[s].T.astype(bf16)
            vt_s[s, :, BAND:2 * LANES] = jnp.zeros((KV_W, 2 * LANES - BAND), bf16)
            nk_ref[s, 0:WINDOW - L, :] = ck[L:WINDOW, :]
            nv_ref[s, 0:WINDOW - L, :] = cv_ref[s, L:WINDOW, :]
        kvf = _mm(hv, wkv_ref, 0, 2 * KV_W)
        for s in range(nb):
            nk_ref[s, WINDOW - L:WINDOW, :] = kvf[s * L:(s + 1) * L, 0:KV_W]
            nv_ref[s, WINDOW - L:WINDOW, :] = kvf[s * L:(s + 1) * L, KV_W:2 * KV_W]

    for c in range(Q_W // COL_BLK):
        qf = _mm(hv, wq_ref, c * COL_BLK, COL_BLK)
        q_s[:, _cols(c)] = (qf * (HEAD_DIM ** -0.5)).astype(bf16)
    for c in range(KV_HEADS * LANES // COL_BLK):
        kdc = _mm(hv, wkd_ref, c * COL_BLK, COL_BLK)
        for kk in range(COL_BLK // LANES):
            k = c * (COL_BLK // LANES) + kk
            kdk = kdc[:, kk * LANES:(kk + 1) * LANES].astype(bf16)
            if is_prompt:
                kd_s[k, WINDOW:WINDOW + T, :] = kdk
            else:
                for s in range(nb):
                    kd_s[s, k, WINDOW:BAND, :] = kdk[s * L:(s + 1) * L]
    vtf = lax.dot_general(wvt_ref[...], hv, _NT, preferred_element_type=f32).astype(bf16)
    if is_prompt:
        for g in range(n_grp):
            vt_s[1 + g] = vtf[:, g * LANES:(g + 1) * LANES]
    else:
        for s in range(nb):
            vt_s[s, :, WINDOW:BAND] = vtf[:, s * L:(s + 1) * L]

    lane_gq = lax.broadcasted_iota(jnp.int32, (1, GQ), 1)
    sink_rows = [_head_row(lambda h: sink_ref[h], k, lane_gq) for k in range(KV_HEADS)]

    def attn_block(c2):
        chunks = []
        if is_prompt:
            v_win = jnp.concatenate([vt_s[c2], vt_s[c2 + 1]], axis=1)
            for par in range(2):
                r = c2 * PAIR + par * CHUNK
                pos = t * T + r - WINDOW + lax.broadcasted_iota(jnp.int32, (BAND, 1), 0)
                chunks.append((q_s[r:r + CHUNK, :],
                               [kd_s[k, r:r + BAND, :] for k in range(KV_HEADS)],
                               v_win, par == 1, pos >= 0))
        else:
            for par in range(2):
                s = 2 * c2 + par
                chunks.append((q_s[s * CHUNK:(s + 1) * CHUNK, :],
                               [kd_s[s, k] for k in range(KV_HEADS)],
                               vt_s[s], False, None))
        _attn_pair(chunks, bias_ref, sink_rows, st_s, pb_s, (c2 % 2) * 2 * KV_HEADS, attT_s, c2)
        att_s[c2 * PAIR:(c2 + 1) * PAIR, :] = attT_s[c2].T.astype(bf16)

    def ga_piece(c):
        ga_s[:, _cols(c)] = _sigmoid(_mm(hv, wga_ref, c * COL_BLK, COL_BLK))

    _interleave([functools.partial(ga_piece, c) for c in range(N_CC)],
                [functools.partial(attn_block, c2) for c2 in range(n_grp)], blocks_first=True)

    av = att_s[...]
    for c in range(N_CC):
        a1c = ga_s[:, _cols(c)] * _mm(av, wao_ref, c * COL_BLK, COL_BLK)
        for s in range(nb):
            a1_ref[s, :, _cols(c)] = a1c[s * L:(s + 1) * L]

    if is_prompt:
        @pl.when(t == n_tiles - 1)
        def _():
            kvf = _mm(h_s[T - WINDOW:T, :], wkv_ref, 0, 2 * KV_W)
            nk_ref[0] = kvf[:, 0:KV_W]
            nv_ref[0] = kvf[:, KV_W:2 * KV_W]

        kd_s[:, 0:WINDOW, :] = kd_s[:, T:T + WINDOW, :]
        vt_s[0] = vt_s[n_grp]


def _attn_call(x, mod, caches, weights, nb, L, is_prompt, name):
    B, S, _ = x.shape
    T = nb * L
    if is_prompt:
        n_tiles = S // L
        grid = (B, n_tiles)
        xmap = lambda b, t: (b, t, 0)
        smap = lambda b, t: (b, 0, 0)
        kd_shape = (KV_HEADS, WINDOW + T, LANES)
        vt_shape = (1 + T // PAIR, KV_W, LANES)
    else:
        n_tiles = B // nb
        grid = (n_tiles,)
        xmap = smap = lambda i: (i, 0, 0)
        kd_shape = (nb, KV_HEADS, BAND, LANES)
        vt_shape = (nb, KV_W, 2 * LANES)
    body = functools.partial(_attn_body, is_prompt=is_prompt, nb=nb, L=L, n_tiles=n_tiles)
    cache_specs = [pl.BlockSpec((nb, WINDOW, KV_W), smap)] * len(caches)
    weight_specs = [
        _const_spec((KV_HEADS, BAND, GQ)),
        pl.BlockSpec(memory_space=pltpu.SMEM),
        _const_spec((1, D_MODEL)),
        _const_spec((D_MODEL, Q_W)),
        _const_spec((D_MODEL, KV_HEADS * LANES)),
        _const_spec((KV_W, D_MODEL)),
        _const_spec((D_MODEL, 2 * KV_W)),
        _const_spec((D_MODEL, D_MODEL)),
        _const_spec((Q_W, D_MODEL)),
    ]
    return pl.pallas_call(
        body,
        grid=grid,
        in_specs=[pl.BlockSpec((nb, L, D_MODEL), xmap), pl.BlockSpec((nb, 6, D_MODEL), smap)]
        + cache_specs + weight_specs,
        out_specs=[pl.BlockSpec((nb, L, D_MODEL), xmap),
                   pl.BlockSpec((nb, WINDOW, KV_W), smap),
                   pl.BlockSpec((nb, WINDOW, KV_W), smap)],
        out_shape=[jax.ShapeDtypeStruct((B, S, D_MODEL), f32),
                   jax.ShapeDtypeStruct((B, WINDOW, KV_W), f32),
                   jax.ShapeDtypeStruct((B, WINDOW, KV_W), f32)],
        scratch_shapes=[
            pltpu.VMEM((T, D_MODEL), bf16),
            pltpu.VMEM((T, Q_W), bf16),
            pltpu.VMEM(kd_shape, bf16),
            pltpu.VMEM(vt_shape, bf16),
            pltpu.VMEM((4 * KV_HEADS, BAND, GQ), f32),
            pltpu.VMEM((4 * KV_HEADS, 2 * LANES, GQ), bf16),
            pltpu.VMEM((T // PAIR, Q_W, PAIR), f32),
            pltpu.VMEM((T, Q_W), bf16),
            pltpu.VMEM((T, D_MODEL), f32),
        ],
        compiler_params=pltpu.CompilerParams(
            dimension_semantics=("arbitrary",) * len(grid), vmem_limit_bytes=VMEM_LIMIT),
        name=name,
    )(x, mod, *caches, *weights)


def _conv_block(zb_ref, s, j, i, dww_ref, cb_ref, out_row0):
    r0 = i * 32
    accs = [jnp.zeros((8, LANES), f32) for _ in range(4)]
    for k in range(CONV_W):
        wk = dww_ref[k:k + 1, j * LANES:(j + 1) * LANES]
        for a in range(4):
            start = r0 + (a // 2) * 16 + (a % 2) + (HIST - (CONV_W - 1)) + k
            accs[a] = accs[a] + zb_ref[s, j, pl.ds(start, 8, stride=2), :] * wk
    for a in range(4):
        start = out_row0 + r0 + (a // 2) * 16 + (a % 2)
        cb_ref[j, pl.ds(start, 8, stride=2), :] = accs[a]


def _mix_body(*refs, is_prompt, nb, L, n_tiles, tiles_per_seq):
    if is_prompt:
        (x_ref, a1_ref, mod1_ref, mod2_ref, g1_ref, wglu_ref, wgb_ref, dww_ref, dwb_ref, lng_ref, lnb_ref,
         wco_ref, wo_ref, g2_ref, fg_ref, wup_ref, wdn_ref,
         y_ref, nc_ref,
         h_s, zb_s, cb_s, cbf_s, gb_s, acc_s, x1_s, h2_s, act_s, x2_s) = refs
        cc_ref = None
    else:
        (x_ref, a1_ref, mod1_ref, mod2_ref, cc_ref, g1_ref, wglu_ref, wgb_ref, dww_ref, dwb_ref, lng_ref, lnb_ref,
         wco_ref, wo_ref, g2_ref, fg_ref, wup_ref, wdn_ref,
         y_ref, nc_ref,
         h_s, zb_s, cb_s, cbf_s, gb_s, acc_s, x1_s, h2_s, act_s, x2_s) = refs
    T = nb * L
    step = pl.program_id(0)
    t_in_seq = jnp.minimum(step, n_tiles - 1) % tiles_per_seq

    @pl.when(step == 0)
    def _():
        x1_s[...] = jnp.zeros((T, D_MODEL), f32)

    if is_prompt:
        @pl.when(t_in_seq == 0)
        def _():
            zb_s[0, :, 0:HIST, :] = jnp.zeros((N_SLABS, HIST, LANES), f32)

    for s in range(nb):
        for i in range(L // ROW_BLK):
            rows = slice(s * L + i * ROW_BLK, s * L + (i + 1) * ROW_BLK)
            hh = _rms(x1_s[rows, :], g2_ref[...]) * (1.0 + mod2_ref[0, s, 4:5, :]) + mod2_ref[0, s, 3:4, :]
            h2_s[rows, :] = hh.astype(bf16)
    for s in range(nb):
        for i in range(L // ROW_BLK):
            rows = slice(i * ROW_BLK, (i + 1) * ROW_BLK)
            hh = _rms(x_ref[0, s, rows, :], g1_ref[...]) * (1.0 + mod1_ref[0, s, 1:2, :]) + mod1_ref[0, s, 0:1, :]
            h_s[s * L + i * ROW_BLK:s * L + (i + 1) * ROW_BLK, :] = hh.astype(bf16)
    hv = h_s[...]
    h2v = h2_s[...]

    if not is_prompt:
        for s in range(nb):
            for j in range(N_SLABS):
                zb_s[s, j, HIST - (CONV_W - 1):HIST, :] = cc_ref[0, s, :, j * LANES:(j + 1) * LANES]

    def glu_piece(c):
        za = _mm(hv, wglu_ref, c * COL_BLK, COL_BLK)
        zg = _mm(hv, wglu_ref, D_MODEL + c * COL_BLK, COL_BLK)
        zc = za * _sigmoid(zg)
        for jj in range(COL_BLK // LANES):
            j = c * (COL_BLK // LANES) + jj
            z = zc[:, jj * LANES:(jj + 1) * LANES]
            for s in range(nb):
                zb_s[s, j, HIST:HIST + L, :] = z[s * L:(s + 1) * L]
                if not is_prompt:
                    nc_ref[0, s, :, j * LANES:(j + 1) * LANES] = z[s * L + L - (CONV_W - 1):(s + 1) * L]

    def gb_piece(c):
        gb_s[:, _cols(c)] = _sigmoid(_mm(hv, wgb_ref, c * COL_BLK, COL_BLK))

    def up_piece(c):
        gate = _mm(h2v, wup_ref, c * COL_BLK, COL_BLK)
        up = _mm(h2v, wup_ref, D_FF + c * COL_BLK, COL_BLK)
        act_s[:, _cols(c)] = (gate * _sigmoid(gate) * up).astype(bf16)

    def down_piece(c):
        d = _mm(act_s[...], wdn_ref, c * COL_BLK, COL_BLK)
        for s in range(nb):
            rows = slice(s * L, (s + 1) * L)
            x2_s[rows, _cols(c)] = x1_s[rows, _cols(c)] + mod2_ref[0, s, 5:6, _cols(c)] * d[rows]

    def ln_block(i):
        rows = slice(i * ROW_BLK, (i + 1) * ROW_BLK)
        ys = [cb_s[j, rows, :] + dwb_ref[:, j * LANES:(j + 1) * LANES] for j in range(N_SLABS)]
        tot = ys[0]
        for j in range(1, N_SLABS):
            tot = tot + ys[j]
        mu = jnp.sum(tot, axis=-1, keepdims=True) * (1.0 / D_MODEL)
        ds = [y - mu for y in ys]
        sq = ds[0] * ds[0]
        for j in range(1, N_SLABS):
            sq = sq + ds[j] * ds[j]
        var = jnp.sum(sq, axis=-1, keepdims=True) * (1.0 / D_MODEL)
        rstd = lax.rsqrt(var + EPS)
        for j in range(N_SLABS):
            y = ds[j] * rstd * lng_ref[:, j * LANES:(j + 1) * LANES] + lnb_ref[:, j * LANES:(j + 1) * LANES]
            cbf_s[rows, j * LANES:(j + 1) * LANES] = (y * _sigmoid(y)).astype(bf16)

    def yb_piece(c):
        yb = _mm(cbf_s[...], wco_ref, c * COL_BLK, COL_BLK)
        for s in range(nb):
            rows = slice(s * L, (s + 1) * L)
            acc_s[rows, _cols(c)] = a1_ref[0, s, :, _cols(c)] + gb_s[rows, _cols(c)] * yb[rows]

    def final_block(i):
        s, r = divmod(i * ROW_BLK, L)
        y_ref[0, s, r:r + ROW_BLK, :] = _rms(x2_s[i * ROW_BLK:(i + 1) * ROW_BLK, :], fg_ref[...])

    def out_piece(c):
        m = _mm(acc_s[...].astype(bf16), wo_ref, c * COL_BLK, COL_BLK)
        for s in range(nb):
            rows = slice(s * L, (s + 1) * L)
            x1_s[rows, _cols(c)] = x_ref[0, s, :, _cols(c)] + mod1_ref[0, s, 2:3, _cols(c)] * m[rows]

    glu_piece(0)
    pieces = [functools.partial(glu_piece, c) for c in range(1, N_CC)]
    pieces += [functools.partial(up_piece, c) for c in range(D_FF // COL_BLK)]
    pieces += [functools.partial(gb_piece, c) for c in range(N_CC)]
    assert len(pieces) >= 2 * N_CC
    conv_blocks = [functools.partial(_conv_block, zb_s, s, j, i, dww_ref, cb_s, s * L)
                   for j in range(N_SLABS) for s in range(nb) for i in range(L // 32)]
    _interleave(pieces, conv_blocks)
    _interleave([functools.partial(down_piece, c) for c in range(N_CC)],
                [functools.partial(ln_block, i) for i in range(T // ROW_BLK)])
    _interleave([functools.partial(yb_piece, c) for c in range(N_CC)],
                [functools.partial(final_block, i) for i in range(T // ROW_BLK)])
    for c in range(N_CC):
        out_piece(c)

    if is_prompt:
        @pl.when(t_in_seq == tiles_per_seq - 1)
        def _():
            for j in range(N_SLABS):
                nc_ref[0, 0, :, j * LANES:(j + 1) * LANES] = zb_s[0, j, HIST + L - (CONV_W - 1):HIST + L, :]

        zb_s[0, :, 0:HIST, :] = zb_s[0, :, L:L + HIST, :]


def _mix_call(x, a1, mod, cc, weights, nb, L, is_prompt, name):
    n_tiles = x.shape[0]
    tiles_per_seq = n_tiles // mod.shape[0]
    T = nb * L
    tile1 = lambda i: (jnp.minimum(i, n_tiles - 1), 0, 0, 0)
    tile2 = lambda i: (jnp.maximum(i - 1, 0), 0, 0, 0)
    grp1 = lambda i: (jnp.minimum(i, n_tiles - 1) // tiles_per_seq, 0, 0, 0)
    grp2 = lambda i: (jnp.maximum(i - 1, 0) // tiles_per_seq, 0, 0, 0)
    body = functools.partial(_mix_body, is_prompt=is_prompt, nb=nb, L=L, n_tiles=n_tiles,
                             tiles_per_seq=tiles_per_seq)
    row = _const_spec((1, D_MODEL))
    in_specs = [pl.BlockSpec((1, nb, L, D_MODEL), tile1),
                pl.BlockSpec((1, nb, L, D_MODEL), tile1),
                pl.BlockSpec((1, nb, 6, D_MODEL), grp1),
                pl.BlockSpec((1, nb, 6, D_MODEL), grp2)]
    args = [x, a1, mod, mod]
    if not is_prompt:
        in_specs.append(pl.BlockSpec((1, nb, CONV_W - 1, D_MODEL), tile1))
        args.append(cc)
    in_specs += [
        row,
        _const_spec((D_MODEL, 2 * D_MODEL)),
        _const_spec((D_MODEL, D_MODEL)),
        _const_spec((CONV_W, D_MODEL)),
        row, row, row,
        _const_spec((D_MODEL, D_MODEL)),
        _const_spec((D_MODEL, D_MODEL)),
        row, row,
        _const_spec((D_MODEL, 2 * D_FF)),
        _const_spec((D_FF, D_MODEL)),
    ]
    n_groups = mod.shape[0]
    return pl.pallas_call(
        body,
        grid=(n_tiles + 1,),
        in_specs=in_specs,
        out_specs=[pl.BlockSpec((1, nb, L, D_MODEL), tile2),
                   pl.BlockSpec((1, nb, CONV_W - 1, D_MODEL), grp1)],
        out_shape=[jax.ShapeDtypeStruct(x.shape, f32),
                   jax.ShapeDtypeStruct((n_groups, nb, CONV_W - 1, D_MODEL), f32)],
        scratch_shapes=[
            pltpu.VMEM((T, D_MODEL), bf16),
            pltpu.VMEM((nb, N_SLABS, HIST + L, LANES), f32),
            pltpu.VMEM((N_SLABS, T, LANES), f32),
            pltpu.VMEM((T, D_MODEL), bf16),
            pltpu.VMEM((T, D_MODEL), f32),
            pltpu.VMEM((T, D_MODEL), f32),
            pltpu.VMEM((T, D_MODEL), f32),
            pltpu.VMEM((T, D_MODEL), bf16),
            pltpu.VMEM((T, D_FF), bf16),
            pltpu.VMEM((T, D_MODEL), f32),
        ],
        compiler_params=pltpu.CompilerParams(
            dimension_semantics=("arbitrary",), vmem_limit_bytes=VMEM_LIMIT),
        name=name,
    )(*args, *weights)


def kernel(x_prompt, x_sample, cache_k, cache_v, cache_conv, c_prompt, c_sample, rel_table, w_ada, b_ada, norm1_g, norm2_g, w_in, sink, w_attn_out, dw_w, dw_b, conv_ln_g, conv_ln_b, w_conv_out, w_out, w_ffn_up, w_ffn_down, final_g):
    assert w_ada.shape[0] == 1, "single-layer kernel"
    B, S, _ = x_prompt.shape
    DB, DS, _ = x_sample.shape
    nb_attn = ATTN_TOKENS // DS
    nb_mix = MIX_TOKENS // DS

    bias = _bias_table(rel_table)
    n_seq = B + DB
    pad = (-n_seq) % 8
    c_all = jnp.concatenate([c_prompt, c_sample, jnp.zeros((pad, D_MODEL), f32)], axis=0)
    mod = _modulation(c_all, w_ada[0], b_ada).reshape(n_seq + pad, 6, D_MODEL)
    mod_p, mod_s = mod[:B], mod[B:B + DB]

    row = lambda v: v.reshape(1, D_MODEL)
    w = w_in[0]
    w_k = w[:, O_K:O_V].astype(bf16).reshape(D_MODEL, KV_HEADS, 1, HEAD_DIM)
    w_kd = jnp.broadcast_to(w_k, (D_MODEL, KV_HEADS, 2, HEAD_DIM)).reshape(D_MODEL, KV_HEADS * LANES)
    attn_w = (bias, sink[0], row(norm1_g[0]), w[:, :Q_W].astype(bf16), w_kd, w[:, O_V:O_GLU].T.astype(bf16),
              w[:, O_K:O_GLU].astype(bf16), w[:, O_GA:O_GB].astype(bf16), w_attn_out[0].astype(bf16))
    mix_w = (row(norm1_g[0]), w[:, O_GLU:O_GA].astype(bf16), w[:, O_GB:].astype(bf16), dw_w[0], row(dw_b[0]),
             row(conv_ln_g[0]), row(conv_ln_b[0]), w_conv_out[0].astype(bf16), w_out[0].astype(bf16),
             row(norm2_g[0]), row(final_g), w_ffn_up[0].astype(bf16), w_ffn_down[0].astype(bf16))

    a1p, nkp, nvp = _attn_call(x_prompt, mod_p, (), attn_w, 1, ATTN_TOKENS, True, "attn_prompt")
    ck = cache_k[0].reshape(DB, WINDOW, KV_W)
    cv = cache_v[0].reshape(DB, WINDOW, KV_W)
    a1s, nks, nvs = _attn_call(x_sample, mod_s, (ck, cv), attn_w, nb_attn, DS, False, "attn_sample")

    tiles = lambda a, n, l: a.reshape(-1, n, l, D_MODEL)
    y_p, ncp = _mix_call(tiles(x_prompt, 1, MIX_TOKENS), tiles(a1p, 1, MIX_TOKENS), mod_p[:, None], None,
                         mix_w, 1, MIX_TOKENS, True, "mix_prompt")
    y_s, ncs = _mix_call(tiles(x_sample, nb_mix, DS), tiles(a1s, nb_mix, DS), tiles(mod_s, nb_mix, 6),
                         tiles(cache_conv[0], nb_mix, CONV_W - 1), mix_w, nb_mix, DS, False, "mix_sample")

    kv5 = lambda a: a.reshape(1, a.shape[0], WINDOW, KV_HEADS, HEAD_DIM)
    return (y_p.reshape(B, S, D_MODEL), y_s.reshape(DB, DS, D_MODEL), kv5(nkp), kv5(nvp),
            ncp.reshape(1, B, CONV_W - 1, D_MODEL), kv5(nks), kv5(nvs), ncs.reshape(1, DB, CONV_W - 1, D_MODEL))
```

```python
import functools
import math

import jax
import jax.numpy as jnp
from jax import lax
from jax.experimental import pallas as pl
from jax.experimental.pallas import tpu as pltpu

f32 = jnp.float32
bf16 = jnp.bfloat16

D_MODEL = 1024
N_HEADS = 16
KV_HEADS = 4
HEAD_DIM = 64
GROUP = N_HEADS // KV_HEADS
WINDOW = 128
CHUNK = 64
BAND = WINDOW + CHUNK
CONV_W = 31
D_FF = 2816
NUM_BUCKETS = 32
MAX_DISTANCE = 128
EPS = 1e-6
NEG = -1e30
Q_W = N_HEADS * HEAD_DIM
KV_W = KV_HEADS * HEAD_DIM
O_K = Q_W
O_V = O_K + KV_W
O_GLU = O_V + KV_W
O_GA = O_GLU + 2 * D_MODEL
O_GB = O_GA + D_MODEL

LANES = 128
N_SLABS = D_MODEL // LANES
PAIR = 2 * CHUNK
GQ = GROUP * CHUNK
HIST = 32
ROW_BLK = 32
COL_BLK = 256
N_CC = D_MODEL // COL_BLK
N_FF = D_FF // COL_BLK
CONV_UNROLL = 2
VMEM_LIMIT = 56 * 1024 * 1024
ATTN_TOKENS = 512
MIX_TOKENS = 256

_NT = (((1,), (1,)), ((), ()))


def _sigmoid(x):
    return 1.0 / (1.0 + jnp.exp(-x))


def _rms(x, g):
    ms = jnp.mean(x * x, axis=-1, keepdims=True)
    return x * lax.rsqrt(ms + EPS) * g


def _mm(a, w_ref, c0, width):
    return jnp.dot(a, w_ref[:, c0:c0 + width], preferred_element_type=f32)


def _cols(c):
    return slice(c * COL_BLK, (c + 1) * COL_BLK)


def _head_row(vals, k, lane):
    row = jnp.full((1, GQ), vals(GROUP * k + GROUP - 1), f32)
    for g in range(GROUP - 2, -1, -1):
        row = jnp.where(lane < CHUNK * (g + 1), vals(GROUP * k + g), row)
    return row


def _interleave(pieces, blocks, blocks_first=False):
    order = [((n + 0.5) / len(pieces), int(blocks_first), fn) for n, fn in enumerate(pieces)]
    order += [((n + 0.5) / len(blocks), int(not blocks_first), fn) for n, fn in enumerate(blocks)]
    for _, _, fn in sorted(order, key=lambda item: item[:2]):
        fn()


def _const_spec(shape):
    nd = len(shape)
    return pl.BlockSpec(shape, lambda *_: (0,) * nd, pipeline_mode=pl.Buffered(1))


def _rel_bucket(rel):
    nb = NUM_BUCKETS // 2
    max_exact = nb // 2
    ret = (rel > 0).astype(jnp.int32) * nb
    n = jnp.abs(rel)
    nf = jnp.maximum(n, 1).astype(f32)
    large = max_exact + (jnp.log(nf / max_exact) / math.log(MAX_DISTANCE / max_exact)
                         * (nb - max_exact)).astype(jnp.int32)
    large = jnp.minimum(large, nb - 1)
    return ret + jnp.where(n < max_exact, n, large)


def _bias_body(idx_ref, tab_ref, o_ref):
    idx = idx_ref[...]
    lane = lax.broadcasted_iota(jnp.int32, (1, GQ), 1)
    for k in range(KV_HEADS):
        acc = jnp.zeros((BAND, GQ), f32)
        for b in range(NUM_BUCKETS):
            acc = jnp.where(idx == b, _head_row(lambda h: tab_ref[b, h], k, lane), acc)
        o_ref[k] = acc


def _bias_table(rel_table):
    kj = jnp.arange(BAND, dtype=jnp.int32)
    rel = kj[None, :] - WINDOW - jnp.arange(CHUNK, dtype=jnp.int32)[:, None]
    idx_t = jnp.tile(_rel_bucket(rel).T, (1, GROUP))
    return pl.pallas_call(
        _bias_body,
        in_specs=[pl.BlockSpec(memory_space=pltpu.VMEM), pl.BlockSpec(memory_space=pltpu.SMEM)],
        out_specs=pl.BlockSpec(memory_space=pltpu.VMEM),
        out_shape=jax.ShapeDtypeStruct((KV_HEADS, BAND, GQ), f32),
        name="rel_bias",
    )(idx_t, rel_table)


def _mod_body(c_ref, w_ref, b_ref, o_ref):
    c = c_ref[...]
    s = c * _sigmoid(c)
    o_ref[...] = jnp.dot(s.astype(bf16), w_ref[...].astype(bf16), preferred_element_type=f32) + b_ref[...]


def _modulation(c_all, w_ada, b_ada):
    rows = c_all.shape[0]
    blk = 512
    return pl.pallas_call(
        _mod_body,
        grid=(6 * D_MODEL // blk,),
        in_specs=[pl.BlockSpec((rows, D_MODEL), lambda j: (0, 0)),
                  pl.BlockSpec((D_MODEL, blk), lambda j: (0, j)),
                  pl.BlockSpec((1, blk), lambda j: (0, j))],
        out_specs=pl.BlockSpec((rows, blk), lambda j: (0, j)),
        out_shape=jax.ShapeDtypeStruct((rows, 6 * D_MODEL), f32),
        name="adaln_mod",
    )(c_all, w_ada, b_ada)


def _attn_pair(chunks, bias_ref, sink_rows, st_s, pb_s, slot0, attT_ref, grp):
    lane_lo = lax.broadcasted_iota(jnp.int32, (CHUNK, LANES), 1) < HEAD_DIM
    n_pad = 2 * LANES - BAND
    for par, (q_c, kd_bands, v_win, pad_top, valid) in enumerate(chunks):
        for k in range(KV_HEADS):
            blocks = []
            for p in range(2):
                qc = q_c[:, (2 * k + p) * LANES:(2 * k + p + 1) * LANES]
                blocks.append(jnp.where(lane_lo, qc, jnp.zeros_like(qc)))
                blocks.append(jnp.where(lane_lo, jnp.zeros_like(qc), qc))
            qm = jnp.concatenate(blocks, axis=0)
            st_s[slot0 + par * KV_HEADS + k] = lax.dot_general(kd_bands[k], qm, _NT, preferred_element_type=f32)
    for par, (q_c, kd_bands, v_win, pad_top, valid) in enumerate(chunks):
        for k in range(KV_HEADS):
            i = slot0 + par * KV_HEADS + k
            st = st_s[i] + bias_ref[k]
            if valid is not None:
                st = jnp.where(valid, st, NEG)
            m = jnp.maximum(jnp.max(st, axis=0, keepdims=True), sink_rows[k])
            p_ = jnp.exp(st - m)
            den = jnp.sum(p_, axis=0, keepdims=True) + jnp.exp(sink_rows[k] - m)
            r0 = n_pad if pad_top else 0
            pb_s[i, r0:r0 + BAND, :] = (p_ / den).astype(bf16)
            z0 = 0 if pad_top else BAND
            pb_s[i, z0:z0 + n_pad, :] = jnp.zeros((n_pad, GQ), bf16)
    for par, (q_c, kd_bands, v_win, pad_top, valid) in enumerate(chunks):
        for k in range(KV_HEADS):
            i = slot0 + par * KV_HEADS + k
            ot = jnp.dot(v_win[k * HEAD_DIM:(k + 1) * HEAD_DIM, :], pb_s[i], preferred_element_type=f32)
            for g in range(GROUP):
                h = GROUP * k + g
                attT_ref[grp, h * HEAD_DIM:(h + 1) * HEAD_DIM, par * CHUNK:(par + 1) * CHUNK] = (
                    ot[:, g * CHUNK:(g + 1) * CHUNK])


def _attn_body(*refs, is_prompt, nb, L, n_tiles):
    if is_prompt:
        (x_ref, mod_ref, bias_ref, sink_ref, g1_ref, wq_ref, wkd_ref, wvt_ref, wkv_ref, wga_ref, wao_ref,
         a1_ref, nk_ref, nv_ref,
         h_s, q_s, kd_s, vt_s, st_s, pb_s, attT_s, att_s, ga_s) = refs
        ck_ref = cv_ref = None
    else:
        (x_ref, mod_ref, ck_ref, cv_ref, bias_ref, sink_ref, g1_ref, wq_ref, wkd_ref, wvt_ref, wkv_ref, wga_ref,
         wao_ref,
         a1_ref, nk_ref, nv_ref,
         h_s, q_s, kd_s, vt_s, st_s, pb_s, attT_s, att_s, ga_s) = refs
    T = nb * L
    n_grp = T // PAIR

    if is_prompt:
        t = pl.program_id(1)

        @pl.when(t == 0)
        def _():
            kd_s[:, 0:WINDOW, :] = jnp.zeros((KV_HEADS, WINDOW, LANES), bf16)
            vt_s[0] = jnp.zeros((KV_W, LANES), bf16)

    for s in range(nb):
        for i in range(L // ROW_BLK):
            rows = slice(i * ROW_BLK, (i + 1) * ROW_BLK)
            hh = _rms(x_ref[s, rows, :], g1_ref[...]) * (1.0 + mod_ref[s, 1:2, :]) + mod_ref[s, 0:1, :]
            h_s[s * L + i * ROW_BLK:s * L + (i + 1) * ROW_BLK, :] = hh.astype(bf16)
    hv = h_s[...]

    if not is_prompt:
        for s in range(nb):
            ck = ck_ref[s]
            for k in range(KV_HEADS):
                ckk = ck[:, k * HEAD_DIM:(k + 1) * HEAD_DIM].astype(bf16)
                kd_s[s, k, 0:WINDOW, :] = jnp.concatenate([ckk, ckk], axis=1)
            vt_s[s, :, 0:WINDOW] = cv_ref[s].T.astype(bf16)
            vt_s[s, :, BAND:2 * LANES] = jnp.zeros((KV_W, 2 * LANES - BAND), bf16)
            nk_ref[s, 0:WINDOW - L, :] = ck[L:WINDOW, :]
            nv_ref[s, 0:WINDOW - L, :] = cv_ref[s, L:WINDOW, :]
        kvf = _mm(hv, wkv_ref, 0, 2 * KV_W)
        for s in range(nb):
            nk_ref[s, WINDOW - L:WINDOW, :] = kvf[s * L:(s + 1) * L, 0:KV_W]
            nv_ref[s, WINDOW - L:WINDOW, :] = kvf[s * L:(s + 1) * L, KV_W:2 * KV_W]

    for c in range(Q_W // COL_BLK):
        qf = _mm(hv, wq_ref, c * COL_BLK, COL_BLK)
        q_s[:, _cols(c)] = (qf * (HEAD_DIM ** -0.5)).astype(bf16)
    for c in range(KV_HEADS * LANES // COL_BLK):
        kdc = _mm(hv, wkd_ref, c * COL_BLK, COL_BLK)
        for kk in range(COL_BLK // LANES):
            k = c * (COL_BLK // LANES) + kk
            kdk = kdc[:, kk * LANES:(kk + 1) * LANES].astype(bf16)
            if is_prompt:
                kd_s[k, WINDOW:WINDOW + T, :] = kdk
            else:
                for s in range(nb):
                    kd_s[s, k, WINDOW:BAND, :] = kdk[s * L:(s + 1) * L]
    vtf = lax.dot_general(wvt_ref[...], hv, _NT, preferred_element_type=f32).astype(bf16)
    if is_prompt:
        for g in range(n_grp):
            vt_s[1 + g] = vtf[:, g * LANES:(g + 1) * LANES]
    else:
        for s in range(nb):
            vt_s[s, :, WINDOW:BAND] = vtf[:, s * L:(s + 1) * L]

    lane_gq = lax.broadcasted_iota(jnp.int32, (1, GQ), 1)
    sink_rows = [_head_row(lambda h: sink_ref[h], k, lane_gq) for k in range(KV_HEADS)]

    def attn_block(c2):
        chunks = []
        if is_prompt:
            v_win = jnp.concatenate([vt_s[c2], vt_s[c2 + 1]], axis=1)
            for par in range(2):
                r = c2 * PAIR + par * CHUNK
                pos = t * T + r - WINDOW + lax.broadcasted_iota(jnp.int32, (BAND, 1), 0)
                chunks.append((q_s[r:r + CHUNK, :],
                               [kd_s[k, r:r + BAND, :] for k in range(KV_HEADS)],
                               v_win, par == 1, pos >= 0))
        else:
            for par in range(2):
                s = 2 * c2 + par
                chunks.append((q_s[s * CHUNK:(s + 1) * CHUNK, :],
                               [kd_s[s, k] for k in range(KV_HEADS)],
                               vt_s[s], False, None))
        _attn_pair(chunks, bias_ref, sink_rows, st_s, pb_s, (c2 % 2) * 2 * KV_HEADS, attT_s, c2)
        att_s[c2 * PAIR:(c2 + 1) * PAIR, :] = attT_s[c2].T.astype(bf16)

    def ga_piece(c):
        ga_s[:, _cols(c)] = _sigmoid(_mm(hv, wga_ref, c * COL_BLK, COL_BLK))

    _interleave([functools.partial(ga_piece, c) for c in range(N_CC)],
                [functools.partial(attn_block, c2) for c2 in range(n_grp)], blocks_first=True)

    av = att_s[...]
    for c in range(N_CC):
        a1c = ga_s[:, _cols(c)] * _mm(av, wao_ref, c * COL_BLK, COL_BLK)
        for s in range(nb):
            a1_ref[s, :, _cols(c)] = a1c[s * L:(s + 1) * L]

    if is_prompt:
        @pl.when(t == n_tiles - 1)
        def _():
            kvf = _mm(h_s[T - WINDOW:T, :], wkv_ref, 0, 2 * KV_W)
            nk_ref[0] = kvf[:, 0:KV_W]
            nv_ref[0] = kvf[:, KV_W:2 * KV_W]

        kd_s[:, 0:WINDOW, :] = kd_s[:, T:T + WINDOW, :]
        vt_s[0] = vt_s[n_grp]


def _attn_call(x, mod, caches, weights, nb, L, is_prompt, name):
    B, S, _ = x.shape
    T = nb * L
    if is_prompt:
        n_tiles = S // L
        grid = (B, n_tiles)
        xmap = lambda b, t: (b, t, 0)
        smap = lambda b, t: (b, 0, 0)
        kd_shape = (KV_HEADS, WINDOW + T, LANES)
        vt_shape = (1 + T // PAIR, KV_W, LANES)
    else:
        n_tiles = B // nb
        grid = (n_tiles,)
        xmap = smap = lambda i: (i, 0, 0)
        kd_shape = (nb, KV_HEADS, BAND, LANES)
        vt_shape = (nb, KV_W, 2 * LANES)
    body = functools.partial(_attn_body, is_prompt=is_prompt, nb=nb, L=L, n_tiles=n_tiles)
    cache_specs = [pl.BlockSpec((nb, WINDOW, KV_W), smap)] * len(caches)
    weight_specs = [
        _const_spec((KV_HEADS, BAND, GQ)),
        pl.BlockSpec(memory_space=pltpu.SMEM),
        _const_spec((1, D_MODEL)),
        _const_spec((D_MODEL, Q_W)),
        _const_spec((D_MODEL, KV_HEADS * LANES)),
        _const_spec((KV_W, D_MODEL)),
        _const_spec((D_MODEL, 2 * KV_W)),
        _const_spec((D_MODEL, D_MODEL)),
        _const_spec((Q_W, D_MODEL)),
    ]
    return pl.pallas_call(
        body,
        grid=grid,
        in_specs=[pl.BlockSpec((nb, L, D_MODEL), xmap), pl.BlockSpec((nb, 6, D_MODEL), smap)]
        + cache_specs + weight_specs,
        out_specs=[pl.BlockSpec((nb, L, D_MODEL), xmap),
                   pl.BlockSpec((nb, WINDOW, KV_W), smap),
                   pl.BlockSpec((nb, WINDOW, KV_W), smap)],
        out_shape=[jax.ShapeDtypeStruct((B, S, D_MODEL), f32),
                   jax.ShapeDtypeStruct((B, WINDOW, KV_W), f32),
                   jax.ShapeDtypeStruct((B, WINDOW, KV_W), f32)],
        scratch_shapes=[
            pltpu.VMEM((T, D_MODEL), bf16),
            pltpu.VMEM((T, Q_W), bf16),
            pltpu.VMEM(kd_shape, bf16),
            pltpu.VMEM(vt_shape, bf16),
            pltpu.VMEM((4 * KV_HEADS, BAND, GQ), f32),
            pltpu.VMEM((4 * KV_HEADS, 2 * LANES, GQ), bf16),
            pltpu.VMEM((T // PAIR, Q_W, PAIR), f32),
            pltpu.VMEM((T, Q_W), bf16),
            pltpu.VMEM((T, D_MODEL), f32),
        ],
        compiler_params=pltpu.CompilerParams(
            dimension_semantics=("arbitrary",) * len(grid), vmem_limit_bytes=VMEM_LIMIT),
        name=name,
    )(x, mod, *caches, *weights)


def _conv_block(zb_ref, s, j, i, dww_ref, cb_ref, out_row0):
    r0 = i * 32
    accs = [jnp.zeros((8, LANES), f32) for _ in range(4)]
    for k in range(CONV_W):
        wk = dww_ref[j, k:k + 1, :]
        for a in range(4):
            start = r0 + (a // 2) * 16 + (a % 2) + (HIST - (CONV_W - 1)) + k
            accs[a] = accs[a] + zb_ref[s, j, pl.ds(start, 8, stride=2), :] * wk
    for a in range(4):
        start = out_row0 + r0 + (a // 2) * 16 + (a % 2)
        cb_ref[j, pl.ds(start, 8, stride=2), :] = accs[a]


def _mix_body(*refs, is_prompt, nb, L, n_tiles, tiles_per_seq):
    if is_prompt:
        (xa_ref, xb_ref, a1_ref, moda_ref, modb_ref, modc_ref, g1_ref, wglu_ref, wgb_ref, dww_ref, dwb_ref,
         lng_ref, lnb_ref, wco_ref, wo_ref, g2_ref, fg_ref, wup_ref, wdn_ref,
         y_ref, nc_ref,
         h_s, zb_s, cb_s, cbf_s, gb_s, acc_s, x1_s, h2_s, act_s, d_s) = refs
        cc_ref = None
    else:
        (xa_ref, xb_ref, a1_ref, moda_ref, modb_ref, modc_ref, cc_ref, g1_ref, wglu_ref, wgb_ref, dww_ref, dwb_ref,
         lng_ref, lnb_ref, wco_ref, wo_ref, g2_ref, fg_ref, wup_ref, wdn_ref,
         y_ref, nc_ref,
         h_s, zb_s, cb_s, cbf_s, gb_s, acc_s, x1_s, h2_s, act_s, d_s) = refs
    T = nb * L
    step = pl.program_id(0)
    t_in_seq = jnp.minimum(step, n_tiles - 1) % tiles_per_seq

    @pl.when(step == 0)
    def _():
        x1_s[...] = jnp.zeros((T, D_MODEL), f32)
        h_s[...] = jnp.zeros((T, D_MODEL), bf16)
        cb_s[...] = jnp.zeros((N_SLABS, T, LANES), f32)

    if is_prompt:
        @pl.when(t_in_seq == 0)
        def _():
            zb_s[0, :, 0:HIST, :] = jnp.zeros((N_SLABS, HIST, LANES), f32)

    def norm2_block(i):
        s = i * ROW_BLK // L
        rows = slice(i * ROW_BLK, (i + 1) * ROW_BLK)
        hh = _rms(x1_s[rows, :], g2_ref[...]) * (1.0 + modc_ref[0, s, 4:5, :]) + modc_ref[0, s, 3:4, :]
        h2_s[rows, :] = hh.astype(bf16)

    def norm1_block(i):
        s, r = divmod(i * ROW_BLK, L)
        rows = slice(i * ROW_BLK, (i + 1) * ROW_BLK)
        hh = _rms(xa_ref[0, s, r:r + ROW_BLK, :], g1_ref[...]) * (1.0 + moda_ref[0, s, 1:2, :]) + moda_ref[0, s, 0:1, :]
        h_s[rows, :] = hh.astype(bf16)

    def glu_piece(c):
        hv = h_s[...]
        za = _mm(hv, wglu_ref, c * COL_BLK, COL_BLK)
        zg = _mm(hv, wglu_ref, D_MODEL + c * COL_BLK, COL_BLK)
        zc = za * _sigmoid(zg)
        for jj in range(COL_BLK // LANES):
            j = c * (COL_BLK // LANES) + jj
            z = zc[:, jj * LANES:(jj + 1) * LANES]
            for s in range(nb):
                zb_s[s, j, HIST:HIST + L, :] = z[s * L:(s + 1) * L]
                if not is_prompt:
                    nc_ref[0, s, :, j * LANES:(j + 1) * LANES] = z[s * L + L - (CONV_W - 1):(s + 1) * L]

    def gb_piece(c):
        gb_s[:, _cols(c)] = _sigmoid(_mm(h_s[...], wgb_ref, c * COL_BLK, COL_BLK))

    def up_piece(c):
        h2v = h2_s[...]
        gate = jnp.dot(h2v, wup_ref[c], preferred_element_type=f32)
        up = jnp.dot(h2v, wup_ref[N_FF + c], preferred_element_type=f32)
        act_s[c % 2] = (gate * _sigmoid(gate) * up).astype(bf16)

    def down_piece(c):
        d_s[...] += jnp.dot(act_s[c % 2], wdn_ref[c], preferred_element_type=f32)

    def ln_block(i):
        rows = slice(i * ROW_BLK, (i + 1) * ROW_BLK)
        ys = [cb_s[j, rows, :] + dwb_ref[:, j * LANES:(j + 1) * LANES] for j in range(N_SLABS)]
        tot = ys[0]
        for j in range(1, N_SLABS):
            tot = tot + ys[j]
        mu = jnp.sum(tot, axis=-1, keepdims=True) * (1.0 / D_MODEL)
        ds = [y - mu for y in ys]
        sq = ds[0] * ds[0]
        for j in range(1, N_SLABS):
            sq = sq + ds[j] * ds[j]
        var = jnp.sum(sq, axis=-1, keepdims=True) * (1.0 / D_MODEL)
        rstd = lax.rsqrt(var + EPS)
        for j in range(N_SLABS):
            y = ds[j] * rstd * lng_ref[:, j * LANES:(j + 1) * LANES] + lnb_ref[:, j * LANES:(j + 1) * LANES]
            cbf_s[rows, j * LANES:(j + 1) * LANES] = (y * _sigmoid(y)).astype(bf16)

    def yb_piece(c):
        yb = _mm(cbf_s[...], wco_ref, c * COL_BLK, COL_BLK)
        for s in range(nb):
            rows = slice(s * L, (s + 1) * L)
            acc_s[rows, _cols(c)] = a1_ref[0, s, :, _cols(c)] + gb_s[rows, _cols(c)] * yb[rows]

    def final_block(i):
        s, r = divmod(i * ROW_BLK, L)
        rows = slice(i * ROW_BLK, (i + 1) * ROW_BLK)
        x2 = x1_s[rows, :] + modc_ref[0, s, 5:6, :] * d_s[rows, :]
        y_ref[0, s, r:r + ROW_BLK, :] = _rms(x2, fg_ref[...])

    def out_piece(c):
        m = _mm(acc_s[...].astype(bf16), wo_ref, c * COL_BLK, COL_BLK)
        for s in range(nb):
            rows = slice(s * L, (s + 1) * L)
            x1_s[rows, _cols(c)] = xb_ref[0, s, :, _cols(c)] + modb_ref[0, s, 2:3, _cols(c)] * m[rows]

    n_blk = T // ROW_BLK
    _interleave([functools.partial(gb_piece, c) for c in range(N_CC)],
                [functools.partial(ln_block, i) for i in range(n_blk)])
    for i in range(n_blk):
        norm1_block(i)
    if not is_prompt:
        for s in range(nb):
            for j in range(N_SLABS):
                zb_s[s, j, HIST - (CONV_W - 1):HIST, :] = cc_ref[0, s, :, j * LANES:(j + 1) * LANES]
    _interleave([functools.partial(glu_piece, c) for c in range(N_CC)],
                [functools.partial(norm2_block, i) for i in range(n_blk)])
    d_s[...] = jnp.zeros((T, D_MODEL), f32)
    up_piece(0)
    pieces = []
    for c in range(N_FF):
        if c + 1 < N_FF:
            pieces.append(functools.partial(up_piece, c + 1))
        pieces.append(functools.partial(down_piece, c))
    pieces += [functools.partial(yb_piece, c) for c in range(N_CC)]
    conv_blocks = [functools.partial(_conv_block, zb_s, s, j, i, dww_ref, cb_s, s * L)
                   for j in range(N_SLABS) for s in range(nb) for i in range(L // 32)]
    _interleave(pieces, conv_blocks)
    for i in range(n_blk):
        final_block(i)
    for c in range(N_CC):
        out_piece(c)

    if is_prompt:
        @pl.when(t_in_seq == tiles_per_seq - 1)
        def _():
            for j in range(N_SLABS):
                nc_ref[0, 0, :, j * LANES:(j + 1) * LANES] = zb_s[0, j, HIST + L - (CONV_W - 1):HIST + L, :]

        zb_s[0, :, 0:HIST, :] = zb_s[0, :, L:L + HIST, :]


def _mix_call(x, a1, mod, cc, weights, nb, L, is_prompt, name):
    n_tiles = x.shape[0]
    tiles_per_seq = n_tiles // mod.shape[0]
    T = nb * L
    def lagged(lag, per_group):
        def index_map(i):
            t = jnp.clip(i - lag, 0, n_tiles - 1)
            return (t // per_group, 0, 0, 0)
        return index_map

    body = functools.partial(_mix_body, is_prompt=is_prompt, nb=nb, L=L, n_tiles=n_tiles,
                             tiles_per_seq=tiles_per_seq)
    row = _const_spec((1, D_MODEL))
    tile_blk = (1, nb, L, D_MODEL)
    mod_blk = (1, nb, 6, D_MODEL)
    in_specs = [pl.BlockSpec(tile_blk, lagged(0, 1)),
                pl.BlockSpec(tile_blk, lagged(1, 1)),
                pl.BlockSpec(tile_blk, lagged(1, 1)),
                pl.BlockSpec(mod_blk, lagged(0, tiles_per_seq)),
                pl.BlockSpec(mod_blk, lagged(1, tiles_per_seq)),
                pl.BlockSpec(mod_blk, lagged(2, tiles_per_seq))]
    args = [x, x, a1, mod, mod, mod]
    if not is_prompt:
        in_specs.append(pl.BlockSpec((1, nb, CONV_W - 1, D_MODEL), lagged(0, 1)))
        args.append(cc)
    in_specs += [
        row,
        _const_spec((D_MODEL, 2 * D_MODEL)),
        _const_spec((D_MODEL, D_MODEL)),
        _const_spec((N_SLABS, CONV_W, LANES)),
        row, row, row,
        _const_spec((D_MODEL, D_MODEL)),
        _const_spec((D_MODEL, D_MODEL)),
        row, row,
        _const_spec((2 * N_FF, D_MODEL, COL_BLK)),
        _const_spec((N_FF, COL_BLK, D_MODEL)),
    ]
    n_groups = mod.shape[0]
    return pl.pallas_call(
        body,
        grid=(n_tiles + 2,),
        in_specs=in_specs,
        out_specs=[pl.BlockSpec(tile_blk, lagged(2, 1)),
                   pl.BlockSpec((1, nb, CONV_W - 1, D_MODEL), lagged(0, tiles_per_seq))],
        out_shape=[jax.ShapeDtypeStruct(x.shape, f32),
                   jax.ShapeDtypeStruct((n_groups, nb, CONV_W - 1, D_MODEL), f32)],
        scratch_shapes=[
            pltpu.VMEM((T, D_MODEL), bf16),
            pltpu.VMEM((nb, N_SLABS, HIST + L, LANES), f32),
            pltpu.VMEM((N_SLABS, T, LANES), f32),
            pltpu.VMEM((T, D_MODEL), bf16),
            pltpu.VMEM((T, D_MODEL), f32),
            pltpu.VMEM((T, D_MODEL), f32),
            pltpu.VMEM((T, D_MODEL), f32),
            pltpu.VMEM((T, D_MODEL), bf16),
            pltpu.VMEM((2, T, COL_BLK), bf16),
            pltpu.VMEM((T, D_MODEL), f32),
        ],
        compiler_params=pltpu.CompilerParams(
            dimension_semantics=("arbitrary",), vmem_limit_bytes=VMEM_LIMIT),
        name=name,
    )(*args, *weights)


def kernel(x_prompt, x_sample, cache_k, cache_v, cache_conv, c_prompt, c_sample, rel_table, w_ada, b_ada, norm1_g, norm2_g, w_in, sink, w_attn_out, dw_w, dw_b, conv_ln_g, conv_ln_b, w_conv_out, w_out, w_ffn_up, w_ffn_down, final_g):
    assert w_ada.shape[0] == 1, "single-layer kernel"
    B, S, _ = x_prompt.shape
    DB, DS, _ = x_sample.shape
    nb_attn = ATTN_TOKENS // DS
    nb_mix = MIX_TOKENS // DS

    bias = _bias_table(rel_table)
    n_seq = B + DB
    pad = (-n_seq) % 8
    c_all = jnp.concatenate([c_prompt, c_sample, jnp.zeros((pad, D_MODEL), f32)], axis=0)
    mod = _modulation(c_all, w_ada[0], b_ada).reshape(n_seq + pad, 6, D_MODEL)
    mod_p, mod_s = mod[:B], mod[B:B + DB]

    row = lambda v: v.reshape(1, D_MODEL)
    w = w_in[0]
    w_k = w[:, O_K:O_V].astype(bf16).reshape(D_MODEL, KV_HEADS, 1, HEAD_DIM)
    w_kd = jnp.broadcast_to(w_k, (D_MODEL, KV_HEADS, 2, HEAD_DIM)).reshape(D_MODEL, KV_HEADS * LANES)
    attn_w = (bias, sink[0], row(norm1_g[0]), w[:, :Q_W].astype(bf16), w_kd, w[:, O_V:O_GLU].T.astype(bf16),
              w[:, O_K:O_GLU].astype(bf16), w[:, O_GA:O_GB].astype(bf16), w_attn_out[0].astype(bf16))
    dw_slabs = dw_w[0].reshape(CONV_W, N_SLABS, LANES).transpose(1, 0, 2)
    w_up = w_ffn_up[0].astype(bf16).reshape(D_MODEL, 2 * N_FF, COL_BLK).transpose(1, 0, 2)
    mix_w = (row(norm1_g[0]), w[:, O_GLU:O_GA].astype(bf16), w[:, O_GB:].astype(bf16), dw_slabs, row(dw_b[0]),
             row(conv_ln_g[0]), row(conv_ln_b[0]), w_conv_out[0].astype(bf16), w_out[0].astype(bf16),
             row(norm2_g[0]), row(final_g), w_up, w_ffn_down[0].astype(bf16).reshape(N_FF, COL_BLK, D_MODEL))

    a1p, nkp, nvp = _attn_call(x_prompt, mod_p, (), attn_w, 1, ATTN_TOKENS, True, "attn_prompt")
    ck = cache_k[0].reshape(DB, WINDOW, KV_W)
    cv = cache_v[0].reshape(DB, WINDOW, KV_W)
    a1s, nks, nvs = _attn_call(x_sample, mod_s, (ck, cv), attn_w, nb_attn, DS, False, "attn_sample")

    tiles = lambda a, n, l: a.reshape(-1, n, l, D_MODEL)
    y_p, ncp = _mix_call(tiles(x_prompt, 1, MIX_TOKENS), tiles(a1p, 1, MIX_TOKENS), mod_p[:, None], None,
                         mix_w, 1, MIX_TOKENS, True, "mix_prompt")
    y_s, ncs = _mix_call(tiles(x_sample, nb_mix, DS), tiles(a1s, nb_mix, DS), tiles(mod_s, nb_mix, 6),
                         tiles(cache_conv[0], nb_mix, CONV_W - 1), mix_w, nb_mix, DS, False, "mix_sample")

    kv5 = lambda a: a.reshape(1, a.shape[0], WINDOW, KV_HEADS, HEAD_DIM)
    return (y_p.reshape(B, S, D_MODEL), y_s.reshape(DB, DS, D_MODEL), kv5(nkp), kv5(nvp),
            ncp.reshape(1, B, CONV_W - 1, D_MODEL), kv5(nks), kv5(nvs), ncs.reshape(1, DB, CONV_W - 1, D_MODEL))
```

```python
import functools
import math

import jax
import jax.numpy as jnp
from jax import lax
from jax.experimental import pallas as pl
from jax.experimental.pallas import tpu as pltpu

f32 = jnp.float32
bf16 = jnp.bfloat16

D_MODEL = 1024
N_HEADS = 16
KV_HEADS = 4
HEAD_DIM = 64
GROUP = N_HEADS // KV_HEADS
WINDOW = 128
CHUNK = 64
BAND = WINDOW + CHUNK
CONV_W = 31
D_FF = 2816
NUM_BUCKETS = 32
MAX_DISTANCE = 128
EPS = 1e-6
NEG = -1e30
Q_W = N_HEADS * HEAD_DIM
KV_W = KV_HEADS * HEAD_DIM
O_K = Q_W
O_V = O_K + KV_W
O_GLU = O_V + KV_W
O_GA = O_GLU + 2 * D_MODEL
O_GB = O_GA + D_MODEL

LANES = 128
N_SLABS = D_MODEL // LANES
PAIR = 2 * CHUNK
GQ = GROUP * CHUNK
HIST = 32
ROW_BLK = 32
COL_BLK = 256
N_CC = D_MODEL // COL_BLK
N_FF = D_FF // COL_BLK
VMEM_LIMIT = 56 * 1024 * 1024
ATTN_TOKENS = 512
MIX_TOKENS = 256

_NT = (((1,), (1,)), ((), ()))


def _sigmoid(x):
    return 1.0 / (1.0 + jnp.exp(-x))


def _rms(x, g):
    ms = jnp.mean(x * x, axis=-1, keepdims=True)
    return x * lax.rsqrt(ms + EPS) * g


def _mm(a, w_ref, c0, width):
    return jnp.dot(a, w_ref[:, c0:c0 + width], preferred_element_type=f32)


def _cols(c):
    return slice(c * COL_BLK, (c + 1) * COL_BLK)


def _head_row(vals, k, lane):
    row = jnp.full((1, GQ), vals(GROUP * k + GROUP - 1), f32)
    for g in range(GROUP - 2, -1, -1):
        row = jnp.where(lane < CHUNK * (g + 1), vals(GROUP * k + g), row)
    return row


def _interleave(pieces, blocks, blocks_first=False):
    order = [((n + 0.5) / len(pieces), int(blocks_first), fn) for n, fn in enumerate(pieces)]
    order += [((n + 0.5) / len(blocks), int(not blocks_first), fn) for n, fn in enumerate(blocks)]
    for _, _, fn in sorted(order, key=lambda item: item[:2]):
        fn()


def _const_spec(shape):
    nd = len(shape)
    return pl.BlockSpec(shape, lambda *_: (0,) * nd, pipeline_mode=pl.Buffered(1))


def _rel_bucket(rel):
    nb = NUM_BUCKETS // 2
    max_exact = nb // 2
    ret = (rel > 0).astype(jnp.int32) * nb
    n = jnp.abs(rel)
    nf = jnp.maximum(n, 1).astype(f32)
    large = max_exact + (jnp.log(nf / max_exact) / math.log(MAX_DISTANCE / max_exact)
                         * (nb - max_exact)).astype(jnp.int32)
    large = jnp.minimum(large, nb - 1)
    return ret + jnp.where(n < max_exact, n, large)


def _bias_body(idx_ref, tab_ref, o_ref):
    idx = idx_ref[...]
    lane = lax.broadcasted_iota(jnp.int32, (1, GQ), 1)
    for k in range(KV_HEADS):
        acc = jnp.zeros((BAND, GQ), f32)
        for b in range(NUM_BUCKETS):
            acc = jnp.where(idx == b, _head_row(lambda h: tab_ref[b, h], k, lane), acc)
        o_ref[k] = acc


def _bias_table(rel_table):
    kj = jnp.arange(BAND, dtype=jnp.int32)
    rel = kj[None, :] - WINDOW - jnp.arange(CHUNK, dtype=jnp.int32)[:, None]
    idx_t = jnp.tile(_rel_bucket(rel).T, (1, GROUP))
    return pl.pallas_call(
        _bias_body,
        in_specs=[pl.BlockSpec(memory_space=pltpu.VMEM), pl.BlockSpec(memory_space=pltpu.SMEM)],
        out_specs=pl.BlockSpec(memory_space=pltpu.VMEM),
        out_shape=jax.ShapeDtypeStruct((KV_HEADS, BAND, GQ), f32),
        name="rel_bias",
    )(idx_t, rel_table)


def _mod_body(c_ref, w_ref, b_ref, o_ref):
    c = c_ref[...]
    s = c * _sigmoid(c)
    o_ref[...] = jnp.dot(s.astype(bf16), w_ref[...].astype(bf16), preferred_element_type=f32) + b_ref[...]


def _modulation(c_all, w_ada, b_ada):
    rows = c_all.shape[0]
    blk = 512
    return pl.pallas_call(
        _mod_body,
        grid=(6 * D_MODEL // blk,),
        in_specs=[pl.BlockSpec((rows, D_MODEL), lambda j: (0, 0)),
                  pl.BlockSpec((D_MODEL, blk), lambda j: (0, j)),
                  pl.BlockSpec((1, blk), lambda j: (0, j))],
        out_specs=pl.BlockSpec((rows, blk), lambda j: (0, j)),
        out_shape=jax.ShapeDtypeStruct((rows, 6 * D_MODEL), f32),
        name="adaln_mod",
    )(c_all, w_ada, b_ada)


def _attn_pair(chunks, bias_ref, sink_rows, st_s, pb_s, slot0, attT_ref, grp):
    lane_lo = lax.broadcasted_iota(jnp.int32, (CHUNK, LANES), 1) < HEAD_DIM
    n_pad = 2 * LANES - BAND
    for par, (q_c, kd_bands, v_win, pad_top, valid) in enumerate(chunks):
        for k in range(KV_HEADS):
            blocks = []
            for p in range(2):
                qc = q_c[:, (2 * k + p) * LANES:(2 * k + p + 1) * LANES]
                blocks.append(jnp.where(lane_lo, qc, jnp.zeros_like(qc)))
                blocks.append(jnp.where(lane_lo, jnp.zeros_like(qc), qc))
            qm = jnp.concatenate(blocks, axis=0)
            st_s[slot0 + par * KV_HEADS + k] = lax.dot_general(kd_bands[k], qm, _NT, preferred_element_type=f32)
    for par, (q_c, kd_bands, v_win, pad_top, valid) in enumerate(chunks):
        for k in range(KV_HEADS):
            i = slot0 + par * KV_HEADS + k
            st = st_s[i] + bias_ref[k]
            if valid is not None:
                st = jnp.where(valid, st, NEG)
            m = jnp.maximum(jnp.max(st, axis=0, keepdims=True), sink_rows[k])
            p_ = jnp.exp(st - m)
            den = jnp.sum(p_, axis=0, keepdims=True) + jnp.exp(sink_rows[k] - m)
            r0 = n_pad if pad_top else 0
            pb_s[i, r0:r0 + BAND, :] = (p_ / den).astype(bf16)
            z0 = 0 if pad_top else BAND
            pb_s[i, z0:z0 + n_pad, :] = jnp.zeros((n_pad, GQ), bf16)
    for par, (q_c, kd_bands, v_win, pad_top, valid) in enumerate(chunks):
        for k in range(KV_HEADS):
            i = slot0 + par * KV_HEADS + k
            ot = jnp.dot(v_win[k * HEAD_DIM:(k + 1) * HEAD_DIM, :], pb_s[i], preferred_element_type=f32)
            for g in range(GROUP):
                h = GROUP * k + g
                attT_ref[grp, h * HEAD_DIM:(h + 1) * HEAD_DIM, par * CHUNK:(par + 1) * CHUNK] = (
                    ot[:, g * CHUNK:(g + 1) * CHUNK])


def _attn_body(*refs, is_prompt, nb, L, n_tiles):
    if is_prompt:
        (x_ref, mod_ref, bias_ref, sink_ref, g1_ref, wq_ref, wkd_ref, wvt_ref, wkv_ref, wga_ref, wao_ref,
         a1_ref, nk_ref, nv_ref,
         h_s, q_s, kd_s, vt_s, st_s, pb_s, attT_s, att_s, ga_s) = refs
        ck_ref = cv_ref = None
    else:
        (x_ref, mod_ref, ck_ref, cv_ref, bias_ref, sink_ref, g1_ref, wq_ref, wkd_ref, wvt_ref, wkv_ref, wga_ref,
         wao_ref,
         a1_ref, nk_ref, nv_ref,
         h_s, q_s, kd_s, vt_s, st_s, pb_s, attT_s, att_s, ga_s) = refs
    T = nb * L
    n_grp = T // PAIR

    if is_prompt:
        t = pl.program_id(1)

        @pl.when(t == 0)
        def _():
            kd_s[:, 0:WINDOW, :] = jnp.zeros((KV_HEADS, WINDOW, LANES), bf16)
            vt_s[0] = jnp.zeros((KV_W, LANES), bf16)

    for s in range(nb):
        for i in range(L // ROW_BLK):
            rows = slice(i * ROW_BLK, (i + 1) * ROW_BLK)
            hh = _rms(x_ref[s, rows, :], g1_ref[...]) * (1.0 + mod_ref[s, 1:2, :]) + mod_ref[s, 0:1, :]
            h_s[s * L + i * ROW_BLK:s * L + (i + 1) * ROW_BLK, :] = hh.astype(bf16)
    hv = h_s[...]

    if not is_prompt:
        for s in range(nb):
            ck = ck_ref[s]
            for k in range(KV_HEADS):
                ckk = ck[:, k * HEAD_DIM:(k + 1) * HEAD_DIM].astype(bf16)
                kd_s[s, k, 0:WINDOW, :] = jnp.concatenate([ckk, ckk], axis=1)
            vt_s[s, :, 0:WINDOW] = cv_ref[s].T.astype(bf16)
            vt_s[s, :, BAND:2 * LANES] = jnp.zeros((KV_W, 2 * LANES - BAND), bf16)
            nk_ref[s, 0:WINDOW - L, :] = ck[L:WINDOW, :]
            nv_ref[s, 0:WINDOW - L, :] = cv_ref[s, L:WINDOW, :]
        kvf = _mm(hv, wkv_ref, 0, 2 * KV_W)
        for s in range(nb):
            nk_ref[s, WINDOW - L:WINDOW, :] = kvf[s * L:(s + 1) * L, 0:KV_W]
            nv_ref[s, WINDOW - L:WINDOW, :] = kvf[s * L:(s + 1) * L, KV_W:2 * KV_W]

    for c in range(Q_W // COL_BLK):
        qf = _mm(hv, wq_ref, c * COL_BLK, COL_BLK)
        q_s[:, _cols(c)] = (qf * (HEAD_DIM ** -0.5)).astype(bf16)
    for c in range(KV_HEADS * LANES // COL_BLK):
        kdc = _mm(hv, wkd_ref, c * COL_BLK, COL_BLK)
        for kk in range(COL_BLK // LANES):
            k = c * (COL_BLK // LANES) + kk
            kdk = kdc[:, kk * LANES:(kk + 1) * LANES].astype(bf16)
            if is_prompt:
                kd_s[k, WINDOW:WINDOW + T, :] = kdk
            else:
                for s in range(nb):
                    kd_s[s, k, WINDOW:BAND, :] = kdk[s * L:(s + 1) * L]
    vtf = lax.dot_general(wvt_ref[...], hv, _NT, preferred_element_type=f32).astype(bf16)
    if is_prompt:
        for g in range(n_grp):
            vt_s[1 + g] = vtf[:, g * LANES:(g + 1) * LANES]
    else:
        for s in range(nb):
            vt_s[s, :, WINDOW:BAND] = vtf[:, s * L:(s + 1) * L]

    lane_gq = lax.broadcasted_iota(jnp.int32, (1, GQ), 1)
    sink_rows = [_head_row(lambda h: sink_ref[h], k, lane_gq) for k in range(KV_HEADS)]

    def attn_block(c2):
        chunks = []
        if is_prompt:
            v_win = jnp.concatenate([vt_s[c2], vt_s[c2 + 1]], axis=1)
            for par in range(2):
                r = c2 * PAIR + par * CHUNK
                pos = t * T + r - WINDOW + lax.broadcasted_iota(jnp.int32, (BAND, 1), 0)
                chunks.append((q_s[r:r + CHUNK, :],
                               [kd_s[k, r:r + BAND, :] for k in range(KV_HEADS)],
                               v_win, par == 1, pos >= 0))
        else:
            for par in range(2):
                s = 2 * c2 + par
                chunks.append((q_s[s * CHUNK:(s + 1) * CHUNK, :],
                               [kd_s[s, k] for k in range(KV_HEADS)],
                               vt_s[s], False, None))
        _attn_pair(chunks, bias_ref, sink_rows, st_s, pb_s, (c2 % 2) * 2 * KV_HEADS, attT_s, c2)
        att_s[c2 * PAIR:(c2 + 1) * PAIR, :] = attT_s[c2].T.astype(bf16)

    def ga_piece(c):
        ga_s[:, _cols(c)] = _sigmoid(_mm(hv, wga_ref, c * COL_BLK, COL_BLK))

    _interleave([functools.partial(ga_piece, c) for c in range(N_CC)],
                [functools.partial(attn_block, c2) for c2 in range(n_grp)], blocks_first=True)

    av = att_s[...]
    for c in range(N_CC):
        a1c = ga_s[:, _cols(c)] * _mm(av, wao_ref, c * COL_BLK, COL_BLK)
        for s in range(nb):
            a1_ref[s, :, _cols(c)] = a1c[s * L:(s + 1) * L]

    if is_prompt:
        @pl.when(t == n_tiles - 1)
        def _():
            kvf = _mm(h_s[T - WINDOW:T, :], wkv_ref, 0, 2 * KV_W)
            nk_ref[0] = kvf[:, 0:KV_W]
            nv_ref[0] = kvf[:, KV_W:2 * KV_W]

        kd_s[:, 0:WINDOW, :] = kd_s[:, T:T + WINDOW, :]
        vt_s[0] = vt_s[n_grp]


def _attn_call(x, mod, caches, weights, nb, L, is_prompt, name):
    B, S, _ = x.shape
    T = nb * L
    if is_prompt:
        n_tiles = S // L
        grid = (B, n_tiles)
        xmap = lambda b, t: (b, t, 0)
        smap = lambda b, t: (b, 0, 0)
        kd_shape = (KV_HEADS, WINDOW + T, LANES)
        vt_shape = (1 + T // PAIR, KV_W, LANES)
    else:
        n_tiles = B // nb
        grid = (n_tiles,)
        xmap = smap = lambda i: (i, 0, 0)
        kd_shape = (nb, KV_HEADS, BAND, LANES)
        vt_shape = (nb, KV_W, 2 * LANES)
    body = functools.partial(_attn_body, is_prompt=is_prompt, nb=nb, L=L, n_tiles=n_tiles)
    cache_specs = [pl.BlockSpec((nb, WINDOW, KV_W), smap)] * len(caches)
    weight_specs = [
        _const_spec((KV_HEADS, BAND, GQ)),
        pl.BlockSpec(memory_space=pltpu.SMEM),
        _const_spec((1, D_MODEL)),
        _const_spec((D_MODEL, Q_W)),
        _const_spec((D_MODEL, KV_HEADS * LANES)),
        _const_spec((KV_W, D_MODEL)),
        _const_spec((D_MODEL, 2 * KV_W)),
        _const_spec((D_MODEL, D_MODEL)),
        _const_spec((Q_W, D_MODEL)),
    ]
    return pl.pallas_call(
        body,
        grid=grid,
        in_specs=[pl.BlockSpec((nb, L, D_MODEL), xmap), pl.BlockSpec((nb, 6, D_MODEL), smap)]
        + cache_specs + weight_specs,
        out_specs=[pl.BlockSpec((nb, L, D_MODEL), xmap),
                   pl.BlockSpec((nb, WINDOW, KV_W), smap),
                   pl.BlockSpec((nb, WINDOW, KV_W), smap)],
        out_shape=[jax.ShapeDtypeStruct((B, S, D_MODEL), f32),
                   jax.ShapeDtypeStruct((B, WINDOW, KV_W), f32),
                   jax.ShapeDtypeStruct((B, WINDOW, KV_W), f32)],
        scratch_shapes=[
            pltpu.VMEM((T, D_MODEL), bf16),
            pltpu.VMEM((T, Q_W), bf16),
            pltpu.VMEM(kd_shape, bf16),
            pltpu.VMEM(vt_shape, bf16),
            pltpu.VMEM((4 * KV_HEADS, BAND, GQ), f32),
            pltpu.VMEM((4 * KV_HEADS, 2 * LANES, GQ), bf16),
            pltpu.VMEM((T // PAIR, Q_W, PAIR), f32),
            pltpu.VMEM((T, Q_W), bf16),
            pltpu.VMEM((T, D_MODEL), f32),
        ],
        compiler_params=pltpu.CompilerParams(
            dimension_semantics=("arbitrary",) * len(grid), vmem_limit_bytes=VMEM_LIMIT),
        name=name,
    )(x, mod, *caches, *weights)


def _conv_block(zb_ref, s, j, i, dww_ref, cb_ref, out_row0):
    r0 = i * 32
    accs = [jnp.zeros((8, LANES), f32) for _ in range(4)]
    for k in range(CONV_W):
        wk = dww_ref[k:k + 1, j * LANES:(j + 1) * LANES]
        for a in range(4):
            start = r0 + (a // 2) * 16 + (a % 2) + (HIST - (CONV_W - 1)) + k
            accs[a] = accs[a] + zb_ref[s, j, pl.ds(start, 8, stride=2), :] * wk
    for a in range(4):
        start = out_row0 + r0 + (a // 2) * 16 + (a % 2)
        cb_ref[j, pl.ds(start, 8, stride=2), :] = accs[a]


def _mix_body(*refs, is_prompt, nb, L, n_tiles, tiles_per_seq):
    if is_prompt:
        (xa_ref, xb_ref, a1_ref, moda_ref, modb_ref, modc_ref, g1_ref, wglu_ref, wgb_ref, dww_ref, dwb_ref,
         lng_ref, lnb_ref, wco_ref, wo_ref, g2_ref, fg_ref, wup_ref, wdn_ref,
         y_ref, nc_ref,
         h_s, zb_s, cb_s, cbf_s, gb_s, acc_s, x1_s, h2_s, act_s, d_s) = refs
        cc_ref = None
    else:
        (xa_ref, xb_ref, a1_ref, moda_ref, modb_ref, modc_ref, cc_ref, g1_ref, wglu_ref, wgb_ref, dww_ref, dwb_ref,
         lng_ref, lnb_ref, wco_ref, wo_ref, g2_ref, fg_ref, wup_ref, wdn_ref,
         y_ref, nc_ref,
         h_s, zb_s, cb_s, cbf_s, gb_s, acc_s, x1_s, h2_s, act_s, d_s) = refs
    T = nb * L
    step = pl.program_id(0)
    t_in_seq = jnp.minimum(step, n_tiles - 1) % tiles_per_seq

    @pl.when(step == 0)
    def _():
        x1_s[...] = jnp.zeros((T, D_MODEL), f32)
        h_s[...] = jnp.zeros((T, D_MODEL), bf16)
        cb_s[...] = jnp.zeros((N_SLABS, T, LANES), f32)

    if is_prompt:
        @pl.when(t_in_seq == 0)
        def _():
            zb_s[0, :, 0:HIST, :] = jnp.zeros((N_SLABS, HIST, LANES), f32)

    def norm2_block(i):
        s = i * ROW_BLK // L
        rows = slice(i * ROW_BLK, (i + 1) * ROW_BLK)
        hh = _rms(x1_s[rows, :], g2_ref[...]) * (1.0 + modc_ref[0, s, 4:5, :]) + modc_ref[0, s, 3:4, :]
        h2_s[rows, :] = hh.astype(bf16)

    def norm1_block(i):
        s, r = divmod(i * ROW_BLK, L)
        rows = slice(i * ROW_BLK, (i + 1) * ROW_BLK)
        hh = _rms(xa_ref[0, s, r:r + ROW_BLK, :], g1_ref[...]) * (1.0 + moda_ref[0, s, 1:2, :]) + moda_ref[0, s, 0:1, :]
        h_s[rows, :] = hh.astype(bf16)

    def glu_piece(c):
        hv = h_s[...]
        za = _mm(hv, wglu_ref, c * COL_BLK, COL_BLK)
        zg = _mm(hv, wglu_ref, D_MODEL + c * COL_BLK, COL_BLK)
        zc = za * _sigmoid(zg)
        for jj in range(COL_BLK // LANES):
            j = c * (COL_BLK // LANES) + jj
            z = zc[:, jj * LANES:(jj + 1) * LANES]
            for s in range(nb):
                zb_s[s, j, HIST:HIST + L, :] = z[s * L:(s + 1) * L]
                if not is_prompt:
                    nc_ref[0, s, :, j * LANES:(j + 1) * LANES] = z[s * L + L - (CONV_W - 1):(s + 1) * L]

    def gb_piece(c):
        gb_s[:, _cols(c)] = _sigmoid(_mm(h_s[...], wgb_ref, c * COL_BLK, COL_BLK))

    def up_piece(c):
        h2v = h2_s[...]
        gate = _mm(h2v, wup_ref, c * COL_BLK, COL_BLK)
        up = _mm(h2v, wup_ref, D_FF + c * COL_BLK, COL_BLK)
        act_s[c % 2] = (gate * _sigmoid(gate) * up).astype(bf16)

    def down_piece(c):
        d_s[...] += jnp.dot(act_s[c % 2], wdn_ref[c * COL_BLK:(c + 1) * COL_BLK, :], preferred_element_type=f32)

    def ln_block(i):
        rows = slice(i * ROW_BLK, (i + 1) * ROW_BLK)
        ys = [cb_s[j, rows, :] + dwb_ref[:, j * LANES:(j + 1) * LANES] for j in range(N_SLABS)]
        tot = ys[0]
        for j in range(1, N_SLABS):
            tot = tot + ys[j]
        mu = jnp.sum(tot, axis=-1, keepdims=True) * (1.0 / D_MODEL)
        ds = [y - mu for y in ys]
        sq = ds[0] * ds[0]
        for j in range(1, N_SLABS):
            sq = sq + ds[j] * ds[j]
        var = jnp.sum(sq, axis=-1, keepdims=True) * (1.0 / D_MODEL)
        rstd = lax.rsqrt(var + EPS)
        for j in range(N_SLABS):
            y = ds[j] * rstd * lng_ref[:, j * LANES:(j + 1) * LANES] + lnb_ref[:, j * LANES:(j + 1) * LANES]
            cbf_s[rows, j * LANES:(j + 1) * LANES] = (y * _sigmoid(y)).astype(bf16)

    def yb_piece(c):
        yb = _mm(cbf_s[...], wco_ref, c * COL_BLK, COL_BLK)
        for s in range(nb):
            rows = slice(s * L, (s + 1) * L)
            acc_s[rows, _cols(c)] = a1_ref[0, s, :, _cols(c)] + gb_s[rows, _cols(c)] * yb[rows]

    def final_block(i):
        s, r = divmod(i * ROW_BLK, L)
        rows = slice(i * ROW_BLK, (i + 1) * ROW_BLK)
        x2 = x1_s[rows, :] + modc_ref[0, s, 5:6, :] * d_s[rows, :]
        y_ref[0, s, r:r + ROW_BLK, :] = _rms(x2, fg_ref[...])

    def out_piece(c):
        m = _mm(acc_s[...].astype(bf16), wo_ref, c * COL_BLK, COL_BLK)
        for s in range(nb):
            rows = slice(s * L, (s + 1) * L)
            x1_s[rows, _cols(c)] = xb_ref[0, s, :, _cols(c)] + modb_ref[0, s, 2:3, _cols(c)] * m[rows]

    n_blk = T // ROW_BLK
    _interleave([functools.partial(gb_piece, c) for c in range(N_CC)],
                [functools.partial(ln_block, i) for i in range(n_blk)])
    for i in range(n_blk):
        norm1_block(i)
    if not is_prompt:
        for s in range(nb):
            for j in range(N_SLABS):
                zb_s[s, j, HIST - (CONV_W - 1):HIST, :] = cc_ref[0, s, :, j * LANES:(j + 1) * LANES]
    _interleave([functools.partial(glu_piece, c) for c in range(N_CC)],
                [functools.partial(norm2_block, i) for i in range(n_blk)])
    d_s[...] = jnp.zeros((T, D_MODEL), f32)
    up_piece(0)
    pieces = []
    for c in range(N_FF):
        if c + 1 < N_FF:
            pieces.append(functools.partial(up_piece, c + 1))
        pieces.append(functools.partial(down_piece, c))
    pieces += [functools.partial(yb_piece, c) for c in range(N_CC)]
    conv_blocks = [functools.partial(_conv_block, zb_s, s, j, i, dww_ref, cb_s, s * L)
                   for j in range(N_SLABS) for s in range(nb) for i in range(L // 32)]
    _interleave(pieces, conv_blocks)
    for i in range(n_blk):
        final_block(i)
    for c in range(N_CC):
        out_piece(c)

    if is_prompt:
        @pl.when(t_in_seq == tiles_per_seq - 1)
        def _():
            for j in range(N_SLABS):
                nc_ref[0, 0, :, j * LANES:(j + 1) * LANES] = zb_s[0, j, HIST + L - (CONV_W - 1):HIST + L, :]

        zb_s[0, :, 0:HIST, :] = zb_s[0, :, L:L + HIST, :]


def _mix_call(x, a1, mod, cc, weights, nb, L, is_prompt, name):
    n_tiles = x.shape[0]
    tiles_per_seq = n_tiles // mod.shape[0]
    T = nb * L
    def lagged(lag, per_group):
        def index_map(i):
            t = jnp.clip(i - lag, 0, n_tiles - 1)
            return (t // per_group, 0, 0, 0)
        return index_map

    body = functools.partial(_mix_body, is_prompt=is_prompt, nb=nb, L=L, n_tiles=n_tiles,
                             tiles_per_seq=tiles_per_seq)
    row = _const_spec((1, D_MODEL))
    tile_blk = (1, nb, L, D_MODEL)
    mod_blk = (1, nb, 6, D_MODEL)
    in_specs = [pl.BlockSpec(tile_blk, lagged(0, 1)),
                pl.BlockSpec(tile_blk, lagged(1, 1)),
                pl.BlockSpec(tile_blk, lagged(1, 1)),
                pl.BlockSpec(mod_blk, lagged(0, tiles_per_seq)),
                pl.BlockSpec(mod_blk, lagged(1, tiles_per_seq)),
                pl.BlockSpec(mod_blk, lagged(2, tiles_per_seq))]
    args = [x, x, a1, mod, mod, mod]
    if not is_prompt:
        in_specs.append(pl.BlockSpec((1, nb, CONV_W - 1, D_MODEL), lagged(0, 1)))
        args.append(cc)
    in_specs += [
        row,
        _const_spec((D_MODEL, 2 * D_MODEL)),
        _const_spec((D_MODEL, D_MODEL)),
        _const_spec((CONV_W, D_MODEL)),
        row, row, row,
        _const_spec((D_MODEL, D_MODEL)),
        _const_spec((D_MODEL, D_MODEL)),
        row, row,
        _const_spec((D_MODEL, 2 * D_FF)),
        _const_spec((D_FF, D_MODEL)),
    ]
    n_groups = mod.shape[0]
    return pl.pallas_call(
        body,
        grid=(n_tiles + 2,),
        in_specs=in_specs,
        out_specs=[pl.BlockSpec(tile_blk, lagged(2, 1)),
                   pl.BlockSpec((1, nb, CONV_W - 1, D_MODEL), lagged(0, tiles_per_seq))],
        out_shape=[jax.ShapeDtypeStruct(x.shape, f32),
                   jax.ShapeDtypeStruct((n_groups, nb, CONV_W - 1, D_MODEL), f32)],
        scratch_shapes=[
            pltpu.VMEM((T, D_MODEL), bf16),
            pltpu.VMEM((nb, N_SLABS, HIST + L, LANES), f32),
            pltpu.VMEM((N_SLABS, T, LANES), f32),
            pltpu.VMEM((T, D_MODEL), bf16),
            pltpu.VMEM((T, D_MODEL), f32),
            pltpu.VMEM((T, D_MODEL), f32),
            pltpu.VMEM((T, D_MODEL), f32),
            pltpu.VMEM((T, D_MODEL), bf16),
            pltpu.VMEM((2, T, COL_BLK), bf16),
            pltpu.VMEM((T, D_MODEL), f32),
        ],
        compiler_params=pltpu.CompilerParams(
            dimension_semantics=("arbitrary",), vmem_limit_bytes=VMEM_LIMIT),
        name=name,
    )(*args, *weights)


def kernel(x_prompt, x_sample, cache_k, cache_v, cache_conv, c_prompt, c_sample, rel_table, w_ada, b_ada, norm1_g, norm2_g, w_in, sink, w_attn_out, dw_w, dw_b, conv_ln_g, conv_ln_b, w_conv_out, w_out, w_ffn_up, w_ffn_down, final_g):
    assert w_ada.shape[0] == 1, "single-layer kernel"
    B, S, _ = x_prompt.shape
    DB, DS, _ = x_sample.shape
    nb_attn = ATTN_TOKENS // DS
    nb_mix = MIX_TOKENS // DS

    bias = _bias_table(rel_table)
    n_seq = B + DB
    pad = (-n_seq) % 8
    c_all = jnp.concatenate([c_prompt, c_sample, jnp.zeros((pad, D_MODEL), f32)], axis=0)
    mod = _modulation(c_all, w_ada[0], b_ada).reshape(n_seq + pad, 6, D_MODEL)
    mod_p, mod_s = mod[:B], mod[B:B + DB]

    row = lambda v: v.reshape(1, D_MODEL)
    w = w_in[0]
    w_k = w[:, O_K:O_V].astype(bf16).reshape(D_MODEL, KV_HEADS, 1, HEAD_DIM)
    w_kd = jnp.broadcast_to(w_k, (D_MODEL, KV_HEADS, 2, HEAD_DIM)).reshape(D_MODEL, KV_HEADS * LANES)
    attn_w = (bias, sink[0], row(norm1_g[0]), w[:, :Q_W].astype(bf16), w_kd, w[:, O_V:O_GLU].T.astype(bf16),
              w[:, O_K:O_GLU].astype(bf16), w[:, O_GA:O_GB].astype(bf16), w_attn_out[0].astype(bf16))
    mix_w = (row(norm1_g[0]), w[:, O_GLU:O_GA].astype(bf16), w[:, O_GB:].astype(bf16), dw_w[0], row(dw_b[0]),
             row(conv_ln_g[0]), row(conv_ln_b[0]), w_conv_out[0].astype(bf16), w_out[0].astype(bf16),
             row(norm2_g[0]), row(final_g), w_ffn_up[0].astype(bf16), w_ffn_down[0].astype(bf16))

    a1p, nkp, nvp = _attn_call(x_prompt, mod_p, (), attn_w, 1, ATTN_TOKENS, True, "attn_prompt")
    ck = cache_k[0].reshape(DB, WINDOW, KV_W)
    cv = cache_v[0].reshape(DB, WINDOW, KV_W)
    a1s, nks, nvs = _attn_call(x_sample, mod_s, (ck, cv), attn_w, nb_attn, DS, False, "attn_sample")

    tiles = lambda a, n, l: a.reshape(-1, n, l, D_MODEL)
    y_p, ncp = _mix_call(tiles(x_prompt, 1, MIX_TOKENS), tiles(a1p, 1, MIX_TOKENS), mod_p[:, None], None,
                         mix_w, 1, MIX_TOKENS, True, "mix_prompt")
    y_s, ncs = _mix_call(tiles(x_sample, nb_mix, DS), tiles(a1s, nb_mix, DS), tiles(mod_s, nb_mix, 6),
                         tiles(cache_conv[0], nb_mix, CONV_W - 1), mix_w, nb_mix, DS, False, "mix_sample")

    kv5 = lambda a: a.reshape(1, a.shape[0], WINDOW, KV_HEADS, HEAD_DIM)
    return (y_p.reshape(B, S, D_MODEL), y_s.reshape(DB, DS, D_MODEL), kv5(nkp), kv5(nvp),
            ncp.reshape(1, B, CONV_W - 1, D_MODEL), kv5(nks), kv5(nvs), ncs.reshape(1, DB, CONV_W - 1, D_MODEL))
```

```python
import functools
import math

import jax
import jax.numpy as jnp
from jax import lax
from jax.experimental import pallas as pl
from jax.experimental.pallas import tpu as pltpu

f32 = jnp.float32
bf16 = jnp.bfloat16

D_MODEL = 1024
N_HEADS = 16
KV_HEADS = 4
HEAD_DIM = 64
GROUP = N_HEADS // KV_HEADS
WINDOW = 128
CHUNK = 64
BAND = WINDOW + CHUNK
CONV_W = 31
D_FF = 2816
NUM_BUCKETS = 32
MAX_DISTANCE = 128
EPS = 1e-6
NEG = -1e30
Q_W = N_HEADS * HEAD_DIM
KV_W = KV_HEADS * HEAD_DIM
O_K = Q_W
O_V = O_K + KV_W
O_GLU = O_V + KV_W
O_GA = O_GLU + 2 * D_MODEL
O_GB = O_GA + D_MODEL

LANES = 128
N_SLABS = D_MODEL // LANES
PAIR = 2 * CHUNK
GQ = GROUP * CHUNK
HIST = 32
ROW_BLK = 32
COL_BLK = 256
N_CC = D_MODEL // COL_BLK
N_FF = D_FF // COL_BLK
VMEM_LIMIT = 56 * 1024 * 1024
ATTN_TOKENS = 1024
ATTN_SAMPLE_TOKENS = 512
MIX_TOKENS = 256

_NT = (((1,), (1,)), ((), ()))


def _sigmoid(x):
    return 1.0 / (1.0 + jnp.exp(-x))


def _rms(x, g):
    ms = jnp.mean(x * x, axis=-1, keepdims=True)
    return x * lax.rsqrt(ms + EPS) * g


def _mm(a, w_ref, c0, width):
    return jnp.dot(a, w_ref[:, c0:c0 + width], preferred_element_type=f32)


def _cols(c):
    return slice(c * COL_BLK, (c + 1) * COL_BLK)


def _head_row(vals, k, lane):
    row = jnp.full((1, GQ), vals(GROUP * k + GROUP - 1), f32)
    for g in range(GROUP - 2, -1, -1):
        row = jnp.where(lane < CHUNK * (g + 1), vals(GROUP * k + g), row)
    return row


def _interleave(pieces, blocks, blocks_first=False):
    order = [((n + 0.5) / len(pieces), int(blocks_first), fn) for n, fn in enumerate(pieces)]
    order += [((n + 0.5) / len(blocks), int(not blocks_first), fn) for n, fn in enumerate(blocks)]
    for _, _, fn in sorted(order, key=lambda item: item[:2]):
        fn()


def _const_spec(shape):
    nd = len(shape)
    return pl.BlockSpec(shape, lambda *_: (0,) * nd, pipeline_mode=pl.Buffered(1))


def _rel_bucket(rel):
    nb = NUM_BUCKETS // 2
    max_exact = nb // 2
    ret = (rel > 0).astype(jnp.int32) * nb
    n = jnp.abs(rel)
    nf = jnp.maximum(n, 1).astype(f32)
    large = max_exact + (jnp.log(nf / max_exact) / math.log(MAX_DISTANCE / max_exact)
                         * (nb - max_exact)).astype(jnp.int32)
    large = jnp.minimum(large, nb - 1)
    return ret + jnp.where(n < max_exact, n, large)


def _bias_body(idx_ref, tab_ref, o_ref):
    idx = idx_ref[...]
    lane = lax.broadcasted_iota(jnp.int32, (1, GQ), 1)
    for k in range(KV_HEADS):
        acc = jnp.zeros((BAND, GQ), f32)
        for b in range(NUM_BUCKETS):
            acc = jnp.where(idx == b, _head_row(lambda h: tab_ref[b, h], k, lane), acc)
        o_ref[k] = acc


def _bias_table(rel_table):
    kj = jnp.arange(BAND, dtype=jnp.int32)
    rel = kj[None, :] - WINDOW - jnp.arange(CHUNK, dtype=jnp.int32)[:, None]
    idx_t = jnp.tile(_rel_bucket(rel).T, (1, GROUP))
    return pl.pallas_call(
        _bias_body,
        in_specs=[pl.BlockSpec(memory_space=pltpu.VMEM), pl.BlockSpec(memory_space=pltpu.SMEM)],
        out_specs=pl.BlockSpec(memory_space=pltpu.VMEM),
        out_shape=jax.ShapeDtypeStruct((KV_HEADS, BAND, GQ), f32),
        name="rel_bias",
    )(idx_t, rel_table)


def _mod_body(c_ref, w_ref, b_ref, o_ref):
    c = c_ref[...]
    s = c * _sigmoid(c)
    o_ref[...] = jnp.dot(s.astype(bf16), w_ref[...].astype(bf16), preferred_element_type=f32) + b_ref[...]


def _modulation(c_all, w_ada, b_ada):
    rows = c_all.shape[0]
    blk = 512
    return pl.pallas_call(
        _mod_body,
        grid=(6 * D_MODEL // blk,),
        in_specs=[pl.BlockSpec((rows, D_MODEL), lambda j: (0, 0)),
                  pl.BlockSpec((D_MODEL, blk), lambda j: (0, j)),
                  pl.BlockSpec((1, blk), lambda j: (0, j))],
        out_specs=pl.BlockSpec((rows, blk), lambda j: (0, j)),
        out_shape=jax.ShapeDtypeStruct((rows, 6 * D_MODEL), f32),
        name="adaln_mod",
    )(c_all, w_ada, b_ada)


def _attn_pair(chunks, bias_ref, sink_rows, st_s, pb_s, slot0, attT_ref, grp):
    lane_lo = lax.broadcasted_iota(jnp.int32, (CHUNK, LANES), 1) < HEAD_DIM
    n_pad = 2 * LANES - BAND
    for par, (q_c, kd_bands, v_win, pad_top, valid) in enumerate(chunks):
        for k in range(KV_HEADS):
            blocks = []
            for p in range(2):
                qc = q_c[:, (2 * k + p) * LANES:(2 * k + p + 1) * LANES]
                blocks.append(jnp.where(lane_lo, qc, jnp.zeros_like(qc)))
                blocks.append(jnp.where(lane_lo, jnp.zeros_like(qc), qc))
            qm = jnp.concatenate(blocks, axis=0)
            st_s[slot0 + par * KV_HEADS + k] = lax.dot_general(kd_bands[k], qm, _NT, preferred_element_type=f32)
    for par, (q_c, kd_bands, v_win, pad_top, valid) in enumerate(chunks):
        for k in range(KV_HEADS):
            i = slot0 + par * KV_HEADS + k
            st = st_s[i] + bias_ref[k]
            if valid is not None:
                st = jnp.where(valid, st, NEG)
            m = jnp.maximum(jnp.max(st, axis=0, keepdims=True), sink_rows[k])
            p_ = jnp.exp(st - m)
            den = jnp.sum(p_, axis=0, keepdims=True) + jnp.exp(sink_rows[k] - m)
            r0 = n_pad if pad_top else 0
            pb_s[i, r0:r0 + BAND, :] = (p_ / den).astype(bf16)
            z0 = 0 if pad_top else BAND
            pb_s[i, z0:z0 + n_pad, :] = jnp.zeros((n_pad, GQ), bf16)
    for par, (q_c, kd_bands, v_win, pad_top, valid) in enumerate(chunks):
        for k in range(KV_HEADS):
            i = slot0 + par * KV_HEADS + k
            ot = jnp.dot(v_win[k * HEAD_DIM:(k + 1) * HEAD_DIM, :], pb_s[i], preferred_element_type=f32)
            for g in range(GROUP):
                h = GROUP * k + g
                attT_ref[grp, h * HEAD_DIM:(h + 1) * HEAD_DIM, par * CHUNK:(par + 1) * CHUNK] = (
                    ot[:, g * CHUNK:(g + 1) * CHUNK])


def _attn_body(*refs, is_prompt, nb, L, n_tiles):
    if is_prompt:
        (x_ref, mod_ref, bias_ref, sink_ref, g1_ref, wq_ref, wkd_ref, wvt_ref, wkv_ref, wga_ref, wao_ref,
         a1_ref, nk_ref, nv_ref,
         h_s, q_s, kd_s, vt_s, st_s, pb_s, attT_s, att_s, ga_s) = refs
        ck_ref = cv_ref = None
    else:
        (x_ref, mod_ref, ck_ref, cv_ref, bias_ref, sink_ref, g1_ref, wq_ref, wkd_ref, wvt_ref, wkv_ref, wga_ref,
         wao_ref,
         a1_ref, nk_ref, nv_ref,
         h_s, q_s, kd_s, vt_s, st_s, pb_s, attT_s, att_s, ga_s) = refs
    T = nb * L
    n_grp = T // PAIR

    if is_prompt:
        t = pl.program_id(1)

        @pl.when(t == 0)
        def _():
            kd_s[:, 0:WINDOW, :] = jnp.zeros((KV_HEADS, WINDOW, LANES), bf16)
            vt_s[0] = jnp.zeros((KV_W, LANES), bf16)

    for s in range(nb):
        for i in range(L // ROW_BLK):
            rows = slice(i * ROW_BLK, (i + 1) * ROW_BLK)
            hh = _rms(x_ref[s, rows, :], g1_ref[...]) * (1.0 + mod_ref[s, 1:2, :]) + mod_ref[s, 0:1, :]
            h_s[s * L + i * ROW_BLK:s * L + (i + 1) * ROW_BLK, :] = hh.astype(bf16)
    hv = h_s[...]

    if not is_prompt:
        for s in range(nb):
            ck = ck_ref[s]
            for k in range(KV_HEADS):
                ckk = ck[:, k * HEAD_DIM:(k + 1) * HEAD_DIM].astype(bf16)
                kd_s[s, k, 0:WINDOW, :] = jnp.concatenate([ckk, ckk], axis=1)
            vt_s[s, :, 0:WINDOW] = cv_ref[s].T.astype(bf16)
            vt_s[s, :, BAND:2 * LANES] = jnp.zeros((KV_W, 2 * LANES - BAND), bf16)
            nk_ref[s, 0:WINDOW - L, :] = ck[L:WINDOW, :]
            nv_ref[s, 0:WINDOW - L, :] = cv_ref[s, L:WINDOW, :]
        kvf = _mm(hv, wkv_ref, 0, 2 * KV_W)
        for s in range(nb):
            nk_ref[s, WINDOW - L:WINDOW, :] = kvf[s * L:(s + 1) * L, 0:KV_W]
            nv_ref[s, WINDOW - L:WINDOW, :] = kvf[s * L:(s + 1) * L, KV_W:2 * KV_W]

    for c in range(Q_W // COL_BLK):
        qf = _mm(hv, wq_ref, c * COL_BLK, COL_BLK)
        q_s[:, _cols(c)] = (qf * (HEAD_DIM ** -0.5)).astype(bf16)
    for c in range(KV_HEADS * LANES // COL_BLK):
        kdc = _mm(hv, wkd_ref, c * COL_BLK, COL_BLK)
        for kk in range(COL_BLK // LANES):
            k = c * (COL_BLK // LANES) + kk
            kdk = kdc[:, kk * LANES:(kk + 1) * LANES].astype(bf16)
            if is_prompt:
                kd_s[k, WINDOW:WINDOW + T, :] = kdk
            else:
                for s in range(nb):
                    kd_s[s, k, WINDOW:BAND, :] = kdk[s * L:(s + 1) * L]
    vtf = lax.dot_general(wvt_ref[...], hv, _NT, preferred_element_type=f32).astype(bf16)
    if is_prompt:
        for g in range(n_grp):
            vt_s[1 + g] = vtf[:, g * LANES:(g + 1) * LANES]
    else:
        for s in range(nb):
            vt_s[s, :, WINDOW:BAND] = vtf[:, s * L:(s + 1) * L]

    lane_gq = lax.broadcasted_iota(jnp.int32, (1, GQ), 1)
    sink_rows = [_head_row(lambda h: sink_ref[h], k, lane_gq) for k in range(KV_HEADS)]

    def attn_block(c2):
        chunks = []
        if is_prompt:
            v_win = jnp.concatenate([vt_s[c2], vt_s[c2 + 1]], axis=1)
            for par in range(2):
                r = c2 * PAIR + par * CHUNK
                pos = t * T + r - WINDOW + lax.broadcasted_iota(jnp.int32, (BAND, 1), 0)
                chunks.append((q_s[r:r + CHUNK, :],
                               [kd_s[k, r:r + BAND, :] for k in range(KV_HEADS)],
                               v_win, par == 1, pos >= 0))
        else:
            for par in range(2):
                s = 2 * c2 + par
                chunks.append((q_s[s * CHUNK:(s + 1) * CHUNK, :],
                               [kd_s[s, k] for k in range(KV_HEADS)],
                               vt_s[s], False, None))
        _attn_pair(chunks, bias_ref, sink_rows, st_s, pb_s, (c2 % 2) * 2 * KV_HEADS, attT_s, c2)
        att_s[c2 * PAIR:(c2 + 1) * PAIR, :] = attT_s[c2].T.astype(bf16)

    def ga_piece(c):
        ga_s[:, _cols(c)] = _sigmoid(_mm(hv, wga_ref, c * COL_BLK, COL_BLK))

    _interleave([functools.partial(ga_piece, c) for c in range(N_CC)],
                [functools.partial(attn_block, c2) for c2 in range(n_grp)], blocks_first=True)

    av = att_s[...]
    for c in range(N_CC):
        a1c = ga_s[:, _cols(c)] * _mm(av, wao_ref, c * COL_BLK, COL_BLK)
        for s in range(nb):
            a1_ref[s, :, _cols(c)] = a1c[s * L:(s + 1) * L]

    if is_prompt:
        @pl.when(t == n_tiles - 1)
        def _():
            kvf = _mm(h_s[T - WINDOW:T, :], wkv_ref, 0, 2 * KV_W)
            nk_ref[0] = kvf[:, 0:KV_W]
            nv_ref[0] = kvf[:, KV_W:2 * KV_W]

        kd_s[:, 0:WINDOW, :] = kd_s[:, T:T + WINDOW, :]
        vt_s[0] = vt_s[n_grp]


def _attn_call(x, mod, caches, weights, nb, L, is_prompt, name):
    B, S, _ = x.shape
    T = nb * L
    if is_prompt:
        n_tiles = S // L
        grid = (B, n_tiles)
        xmap = lambda b, t: (b, t, 0)
        smap = lambda b, t: (b, 0, 0)
        kd_shape = (KV_HEADS, WINDOW + T, LANES)
        vt_shape = (1 + T // PAIR, KV_W, LANES)
    else:
        n_tiles = B // nb
        grid = (n_tiles,)
        xmap = smap = lambda i: (i, 0, 0)
        kd_shape = (nb, KV_HEADS, BAND, LANES)
        vt_shape = (nb, KV_W, 2 * LANES)
    body = functools.partial(_attn_body, is_prompt=is_prompt, nb=nb, L=L, n_tiles=n_tiles)
    cache_specs = [pl.BlockSpec((nb, WINDOW, KV_W), smap)] * len(caches)
    weight_specs = [
        _const_spec((KV_HEADS, BAND, GQ)),
        pl.BlockSpec(memory_space=pltpu.SMEM),
        _const_spec((1, D_MODEL)),
        _const_spec((D_MODEL, Q_W)),
        _const_spec((D_MODEL, KV_HEADS * LANES)),
        _const_spec((KV_W, D_MODEL)),
        _const_spec((D_MODEL, 2 * KV_W)),
        _const_spec((D_MODEL, D_MODEL)),
        _const_spec((Q_W, D_MODEL)),
    ]
    return pl.pallas_call(
        body,
        grid=grid,
        in_specs=[pl.BlockSpec((nb, L, D_MODEL), xmap), pl.BlockSpec((nb, 6, D_MODEL), smap)]
        + cache_specs + weight_specs,
        out_specs=[pl.BlockSpec((nb, L, D_MODEL), xmap),
                   pl.BlockSpec((nb, WINDOW, KV_W), smap),
                   pl.BlockSpec((nb, WINDOW, KV_W), smap)],
        out_shape=[jax.ShapeDtypeStruct((B, S, D_MODEL), f32),
                   jax.ShapeDtypeStruct((B, WINDOW, KV_W), f32),
                   jax.ShapeDtypeStruct((B, WINDOW, KV_W), f32)],
        scratch_shapes=[
            pltpu.VMEM((T, D_MODEL), bf16),
            pltpu.VMEM((T, Q_W), bf16),
            pltpu.VMEM(kd_shape, bf16),
            pltpu.VMEM(vt_shape, bf16),
            pltpu.VMEM((4 * KV_HEADS, BAND, GQ), f32),
            pltpu.VMEM((4 * KV_HEADS, 2 * LANES, GQ), bf16),
            pltpu.VMEM((T // PAIR, Q_W, PAIR), f32),
            pltpu.VMEM((T, Q_W), bf16),
            pltpu.VMEM((T, D_MODEL), f32),
        ],
        compiler_params=pltpu.CompilerParams(
            dimension_semantics=("arbitrary",) * len(grid), vmem_limit_bytes=VMEM_LIMIT),
        name=name,
    )(x, mod, *caches, *weights)


def _conv_block(zb_ref, s, j, i, dww_ref, cb_ref, out_row0):
    r0 = i * 32
    accs = [jnp.zeros((8, LANES), f32) for _ in range(4)]
    for k in range(CONV_W):
        wk = dww_ref[k:k + 1, j * LANES:(j + 1) * LANES]
        for a in range(4):
            start = r0 + (a // 2) * 16 + (a % 2) + (HIST - (CONV_W - 1)) + k
            accs[a] = accs[a] + zb_ref[s, j, pl.ds(start, 8, stride=2), :] * wk
    for a in range(4):
        start = out_row0 + r0 + (a // 2) * 16 + (a % 2)
        cb_ref[j, pl.ds(start, 8, stride=2), :] = accs[a]


def _mix_body(*refs, is_prompt, nb, L, n_tiles, tiles_per_seq):
    if is_prompt:
        (xa_ref, xb_ref, a1_ref, moda_ref, modb_ref, modc_ref, g1_ref, wglu_ref, wgb_ref, dww_ref, dwb_ref,
         lng_ref, lnb_ref, wco_ref, wo_ref, g2_ref, fg_ref, wup_ref, wdn_ref,
         y_ref, nc_ref,
         h_s, zb_s, cb_s, cbf_s, gb_s, acc_s, x1_s, h2_s, act_s, d_s) = refs
        cc_ref = None
    else:
        (xa_ref, xb_ref, a1_ref, moda_ref, modb_ref, modc_ref, cc_ref, g1_ref, wglu_ref, wgb_ref, dww_ref, dwb_ref,
         lng_ref, lnb_ref, wco_ref, wo_ref, g2_ref, fg_ref, wup_ref, wdn_ref,
         y_ref, nc_ref,
         h_s, zb_s, cb_s, cbf_s, gb_s, acc_s, x1_s, h2_s, act_s, d_s) = refs
    T = nb * L
    step = pl.program_id(0)
    t_in_seq = jnp.minimum(step, n_tiles - 1) % tiles_per_seq

    @pl.when(step == 0)
    def _():
        x1_s[...] = jnp.zeros((T, D_MODEL), f32)
        h_s[...] = jnp.zeros((T, D_MODEL), bf16)
        cb_s[...] = jnp.zeros((N_SLABS, T, LANES), f32)

    if is_prompt:
        @pl.when(t_in_seq == 0)
        def _():
            zb_s[0, :, 0:HIST, :] = jnp.zeros((N_SLABS, HIST, LANES), f32)

    def norm2_block(i):
        s = i * ROW_BLK // L
        rows = slice(i * ROW_BLK, (i + 1) * ROW_BLK)
        hh = _rms(x1_s[rows, :], g2_ref[...]) * (1.0 + modc_ref[0, s, 4:5, :]) + modc_ref[0, s, 3:4, :]
        h2_s[rows, :] = hh.astype(bf16)

    def norm1_block(i):
        s, r = divmod(i * ROW_BLK, L)
        rows = slice(i * ROW_BLK, (i + 1) * ROW_BLK)
        hh = _rms(xa_ref[0, s, r:r + ROW_BLK, :], g1_ref[...]) * (1.0 + moda_ref[0, s, 1:2, :]) + moda_ref[0, s, 0:1, :]
        h_s[rows, :] = hh.astype(bf16)

    def glu_piece(c):
        hv = h_s[...]
        za = _mm(hv, wglu_ref, c * COL_BLK, COL_BLK)
        zg = _mm(hv, wglu_ref, D_MODEL + c * COL_BLK, COL_BLK)
        zc = za * _sigmoid(zg)
        for jj in range(COL_BLK // LANES):
            j = c * (COL_BLK // LANES) + jj
            z = zc[:, jj * LANES:(jj + 1) * LANES]
            for s in range(nb):
                zb_s[s, j, HIST:HIST + L, :] = z[s * L:(s + 1) * L]
                if not is_prompt:
                    nc_ref[0, s, :, j * LANES:(j + 1) * LANES] = z[s * L + L - (CONV_W - 1):(s + 1) * L]

    def gb_piece(c):
        gb_s[:, _cols(c)] = _sigmoid(_mm(h_s[...], wgb_ref, c * COL_BLK, COL_BLK))

    def up_piece(c):
        h2v = h2_s[...]
        gate = _mm(h2v, wup_ref, c * COL_BLK, COL_BLK)
        up = _mm(h2v, wup_ref, D_FF + c * COL_BLK, COL_BLK)
        act_s[c % 2] = (gate * _sigmoid(gate) * up).astype(bf16)

    def down_piece(c):
        d_s[...] += jnp.dot(act_s[c % 2], wdn_ref[c * COL_BLK:(c + 1) * COL_BLK, :], preferred_element_type=f32)

    def ln_block(i):
        rows = slice(i * ROW_BLK, (i + 1) * ROW_BLK)
        ys = [cb_s[j, rows, :] + dwb_ref[:, j * LANES:(j + 1) * LANES] for j in range(N_SLABS)]
        tot = ys[0]
        for j in range(1, N_SLABS):
            tot = tot + ys[j]
        mu = jnp.sum(tot, axis=-1, keepdims=True) * (1.0 / D_MODEL)
        ds = [y - mu for y in ys]
        sq = ds[0] * ds[0]
        for j in range(1, N_SLABS):
            sq = sq + ds[j] * ds[j]
        var = jnp.sum(sq, axis=-1, keepdims=True) * (1.0 / D_MODEL)
        rstd = lax.rsqrt(var + EPS)
        for j in range(N_SLABS):
            y = ds[j] * rstd * lng_ref[:, j * LANES:(j + 1) * LANES] + lnb_ref[:, j * LANES:(j + 1) * LANES]
            cbf_s[rows, j * LANES:(j + 1) * LANES] = (y * _sigmoid(y)).astype(bf16)

    def yb_piece(c):
        yb = _mm(cbf_s[...], wco_ref, c * COL_BLK, COL_BLK)
        for s in range(nb):
            rows = slice(s * L, (s + 1) * L)
            acc_s[rows, _cols(c)] = a1_ref[0, s, :, _cols(c)] + gb_s[rows, _cols(c)] * yb[rows]

    def final_block(i):
        s, r = divmod(i * ROW_BLK, L)
        rows = slice(i * ROW_BLK, (i + 1) * ROW_BLK)
        x2 = x1_s[rows, :] + modc_ref[0, s, 5:6, :] * d_s[rows, :]
        y_ref[0, s, r:r + ROW_BLK, :] = _rms(x2, fg_ref[...])

    def out_piece(c):
        m = _mm(acc_s[...].astype(bf16), wo_ref, c * COL_BLK, COL_BLK)
        for s in range(nb):
            rows = slice(s * L, (s + 1) * L)
            x1_s[rows, _cols(c)] = xb_ref[0, s, :, _cols(c)] + modb_ref[0, s, 2:3, _cols(c)] * m[rows]

    n_blk = T // ROW_BLK
    _interleave([functools.partial(gb_piece, c) for c in range(N_CC)],
                [functools.partial(ln_block, i) for i in range(n_blk)])
    for i in range(n_blk):
        norm1_block(i)
    if not is_prompt:
        for s in range(nb):
            for j in range(N_SLABS):
                zb_s[s, j, HIST - (CONV_W - 1):HIST, :] = cc_ref[0, s, :, j * LANES:(j + 1) * LANES]
    _interleave([functools.partial(glu_piece, c) for c in range(N_CC)],
                [functools.partial(norm2_block, i) for i in range(n_blk)])
    d_s[...] = jnp.zeros((T, D_MODEL), f32)
    up_piece(0)
    pieces = []
    for c in range(N_FF):
        if c + 1 < N_FF:
            pieces.append(functools.partial(up_piece, c + 1))
        pieces.append(functools.partial(down_piece, c))
    pieces += [functools.partial(yb_piece, c) for c in range(N_CC)]
    conv_blocks = [functools.partial(_conv_block, zb_s, s, j, i, dww_ref, cb_s, s * L)
                   for j in range(N_SLABS) for s in range(nb) for i in range(L // 32)]
    _interleave(pieces, conv_blocks)
    for i in range(n_blk):
        final_block(i)
    for c in range(N_CC):
        out_piece(c)

    if is_prompt:
        @pl.when(t_in_seq == tiles_per_seq - 1)
        def _():
            for j in range(N_SLABS):
                nc_ref[0, 0, :, j * LANES:(j + 1) * LANES] = zb_s[0, j, HIST + L - (CONV_W - 1):HIST + L, :]

        zb_s[0, :, 0:HIST, :] = zb_s[0, :, L:L + HIST, :]


def _mix_call(x, a1, mod, cc, weights, nb, L, is_prompt, name):
    n_tiles = x.shape[0]
    tiles_per_seq = n_tiles // mod.shape[0]
    T = nb * L

    def lagged(lag, per_group):
        def index_map(i):
            t = jnp.clip(i - lag, 0, n_tiles - 1)
            return (t // per_group, 0, 0, 0)
        return index_map

    body = functools.partial(_mix_body, is_prompt=is_prompt, nb=nb, L=L, n_tiles=n_tiles,
                             tiles_per_seq=tiles_per_seq)
    row = _const_spec((1, D_MODEL))
    tile_blk = (1, nb, L, D_MODEL)
    mod_blk = (1, nb, 6, D_MODEL)
    in_specs = [pl.BlockSpec(tile_blk, lagged(0, 1)),
                pl.BlockSpec(tile_blk, lagged(1, 1)),
                pl.BlockSpec(tile_blk, lagged(1, 1)),
                pl.BlockSpec(mod_blk, lagged(0, tiles_per_seq)),
                pl.BlockSpec(mod_blk, lagged(1, tiles_per_seq)),
                pl.BlockSpec(mod_blk, lagged(2, tiles_per_seq))]
    args = [x, x, a1, mod, mod, mod]
    if not is_prompt:
        in_specs.append(pl.BlockSpec((1, nb, CONV_W - 1, D_MODEL), lagged(0, 1)))
        args.append(cc)
    in_specs += [
        row,
        _const_spec((D_MODEL, 2 * D_MODEL)),
        _const_spec((D_MODEL, D_MODEL)),
        _const_spec((CONV_W, D_MODEL)),
        row, row, row,
        _const_spec((D_MODEL, D_MODEL)),
        _const_spec((D_MODEL, D_MODEL)),
        row, row,
        _const_spec((D_MODEL, 2 * D_FF)),
        _const_spec((D_FF, D_MODEL)),
    ]
    n_groups = mod.shape[0]
    return pl.pallas_call(
        body,
        grid=(n_tiles + 2,),
        in_specs=in_specs,
        out_specs=[pl.BlockSpec(tile_blk, lagged(2, 1)),
                   pl.BlockSpec((1, nb, CONV_W - 1, D_MODEL), lagged(0, tiles_per_seq))],
        out_shape=[jax.ShapeDtypeStruct(x.shape, f32),
                   jax.ShapeDtypeStruct((n_groups, nb, CONV_W - 1, D_MODEL), f32)],
        scratch_shapes=[
            pltpu.VMEM((T, D_MODEL), bf16),
            pltpu.VMEM((nb, N_SLABS, HIST + L, LANES), f32),
            pltpu.VMEM((N_SLABS, T, LANES), f32),
            pltpu.VMEM((T, D_MODEL), bf16),
            pltpu.VMEM((T, D_MODEL), f32),
            pltpu.VMEM((T, D_MODEL), f32),
            pltpu.VMEM((T, D_MODEL), f32),
            pltpu.VMEM((T, D_MODEL), bf16),
            pltpu.VMEM((2, T, COL_BLK), bf16),
            pltpu.VMEM((T, D_MODEL), f32),
        ],
        compiler_params=pltpu.CompilerParams(
            dimension_semantics=("arbitrary",), vmem_limit_bytes=VMEM_LIMIT),
        name=name,
    )(*args, *weights)


def kernel(x_prompt, x_sample, cache_k, cache_v, cache_conv, c_prompt, c_sample, rel_table, w_ada, b_ada, norm1_g, norm2_g, w_in, sink, w_attn_out, dw_w, dw_b, conv_ln_g, conv_ln_b, w_conv_out, w_out, w_ffn_up, w_ffn_down, final_g):
    assert w_ada.shape[0] == 1, "single-layer kernel"
    B, S, _ = x_prompt.shape
    DB, DS, _ = x_sample.shape
    nb_attn = ATTN_SAMPLE_TOKENS // DS
    nb_mix = MIX_TOKENS // DS

    bias = _bias_table(rel_table)
    n_seq = B + DB
    pad = (-n_seq) % 8
    c_all = jnp.concatenate([c_prompt, c_sample, jnp.zeros((pad, D_MODEL), f32)], axis=0)
    mod = _modulation(c_all, w_ada[0], b_ada).reshape(n_seq + pad, 6, D_MODEL)
    mod_p, mod_s = mod[:B], mod[B:B + DB]

    row = lambda v: v.reshape(1, D_MODEL)
    w = w_in[0]
    w_k = w[:, O_K:O_V].astype(bf16).reshape(D_MODEL, KV_HEADS, 1, HEAD_DIM)
    w_kd = jnp.broadcast_to(w_k, (D_MODEL, KV_HEADS, 2, HEAD_DIM)).reshape(D_MODEL, KV_HEADS * LANES)
    attn_w = (bias, sink[0], row(norm1_g[0]), w[:, :Q_W].astype(bf16), w_kd, w[:, O_V:O_GLU].T.astype(bf16),
              w[:, O_K:O_GLU].astype(bf16), w[:, O_GA:O_GB].astype(bf16), w_attn_out[0].astype(bf16))
    mix_w = (row(norm1_g[0]), w[:, O_GLU:O_GA].astype(bf16), w[:, O_GB:].astype(bf16), dw_w[0], row(dw_b[0]),
             row(conv_ln_g[0]), row(conv_ln_b[0]), w_conv_out[0].astype(bf16), w_out[0].astype(bf16),
             row(norm2_g[0]), row(final_g), w_ffn_up[0].astype(bf16), w_ffn_down[0].astype(bf16))

    a1p, nkp, nvp = _attn_call(x_prompt, mod_p, (), attn_w, 1, ATTN_TOKENS, True, "attn_prompt")
    ck = cache_k[0].reshape(DB, WINDOW, KV_W)
    cv = cache_v[0].reshape(DB, WINDOW, KV_W)
    a1s, nks, nvs = _attn_call(x_sample, mod_s, (ck, cv), attn_w, nb_attn, DS, False, "attn_sample")

    tiles = lambda a, n, l: a.reshape(-1, n, l, D_MODEL)
    y_p, ncp = _mix_call(tiles(x_prompt, 1, MIX_TOKENS), tiles(a1p, 1, MIX_TOKENS), mod_p[:, None], None,
                         mix_w, 1, MIX_TOKENS, True, "mix_prompt")
    y_s, ncs = _mix_call(tiles(x_sample, nb_mix, DS), tiles(a1s, nb_mix, DS), tiles(mod_s, nb_mix, 6),
                         tiles(cache_conv[0], nb_mix, CONV_W - 1), mix_w, nb_mix, DS, False, "mix_sample")

    kv5 = lambda a: a.reshape(1, a.shape[0], WINDOW, KV_HEADS, HEAD_DIM)
    return (y_p.reshape(B, S, D_MODEL), y_s.reshape(DB, DS, D_MODEL), kv5(nkp), kv5(nvp),
            ncp.reshape(1, B, CONV_W - 1, D_MODEL), kv5(nks), kv5(nvs), ncs.reshape(1, DB, CONV_W - 1, D_MODEL))
```

```python
import functools
import math

import jax
import jax.numpy as jnp
from jax import lax
from jax.experimental import pallas as pl
from jax.experimental.pallas import tpu as pltpu

f32 = jnp.float32
bf16 = jnp.bfloat16

D_MODEL = 1024
N_HEADS = 16
KV_HEADS = 4
HEAD_DIM = 64
GROUP = N_HEADS // KV_HEADS
WINDOW = 128
CHUNK = 64
BAND = WINDOW + CHUNK
CONV_W = 31
D_FF = 2816
NUM_BUCKETS = 32
MAX_DISTANCE = 128
EPS = 1e-6
NEG = -1e30
Q_W = N_HEADS * HEAD_DIM
KV_W = KV_HEADS * HEAD_DIM
O_K = Q_W
O_V = O_K + KV_W
O_GLU = O_V + KV_W
O_GA = O_GLU + 2 * D_MODEL
O_GB = O_GA + D_MODEL

LANES = 128
N_SLABS = D_MODEL // LANES
PAIR = 2 * CHUNK
GQ = GROUP * CHUNK
HIST = 32
ROW_BLK = 32
COL_BLK = 256
N_CC = D_MODEL // COL_BLK
N_FF = D_FF // COL_BLK
VMEM_LIMIT = 56 * 1024 * 1024
ATTN_TOKENS = 1024
ATTN_SAMPLE_TOKENS = 512
MIX_TOKENS = 256

_NT = (((1,), (1,)), ((), ()))


def _sigmoid(x):
    return 1.0 / (1.0 + jnp.exp(-x))


def _rms(x, g):
    ms = jnp.mean(x * x, axis=-1, keepdims=True)
    return x * lax.rsqrt(ms + EPS) * g


def _mm(a, w_ref, c0, width):
    return jnp.dot(a, w_ref[:, c0:c0 + width], preferred_element_type=f32)


def _cols(c):
    return slice(c * COL_BLK, (c + 1) * COL_BLK)


def _head_row(vals, k, lane):
    row = jnp.full((1, GQ), vals(GROUP * k + GROUP - 1), f32)
    for g in range(GROUP - 2, -1, -1):
        row = jnp.where(lane < CHUNK * (g + 1), vals(GROUP * k + g), row)
    return row


def _interleave(pieces, blocks, blocks_first=False):
    order = [((n + 0.5) / len(pieces), int(blocks_first), fn) for n, fn in enumerate(pieces)]
    order += [((n + 0.5) / len(blocks), int(not blocks_first), fn) for n, fn in enumerate(blocks)]
    for _, _, fn in sorted(order, key=lambda item: item[:2]):
        fn()


def _const_spec(shape):
    nd = len(shape)
    return pl.BlockSpec(shape, lambda *_: (0,) * nd, pipeline_mode=pl.Buffered(1))


def _rel_bucket(rel):
    nb = NUM_BUCKETS // 2
    max_exact = nb // 2
    ret = (rel > 0).astype(jnp.int32) * nb
    n = jnp.abs(rel)
    nf = jnp.maximum(n, 1).astype(f32)
    large = max_exact + (jnp.log(nf / max_exact) / math.log(MAX_DISTANCE / max_exact)
                         * (nb - max_exact)).astype(jnp.int32)
    large = jnp.minimum(large, nb - 1)
    return ret + jnp.where(n < max_exact, n, large)


def _bias_body(idx_ref, tab_ref, o_ref):
    idx = idx_ref[...]
    lane = lax.broadcasted_iota(jnp.int32, (1, GQ), 1)
    for k in range(KV_HEADS):
        acc = jnp.zeros((BAND, GQ), f32)
        for b in range(NUM_BUCKETS):
            acc = jnp.where(idx == b, _head_row(lambda h: tab_ref[b, h], k, lane), acc)
        o_ref[k] = acc


def _bias_table(rel_table):
    kj = jnp.arange(BAND, dtype=jnp.int32)
    rel = kj[None, :] - WINDOW - jnp.arange(CHUNK, dtype=jnp.int32)[:, None]
    idx_t = jnp.tile(_rel_bucket(rel).T, (1, GROUP))
    return pl.pallas_call(
        _bias_body,
        in_specs=[pl.BlockSpec(memory_space=pltpu.VMEM), pl.BlockSpec(memory_space=pltpu.SMEM)],
        out_specs=pl.BlockSpec(memory_space=pltpu.VMEM),
        out_shape=jax.ShapeDtypeStruct((KV_HEADS, BAND, GQ), f32),
        name="rel_bias",
    )(idx_t, rel_table)


def _mod_body(c_ref, w_ref, b_ref, o_ref):
    c = c_ref[...]
    s = c * _sigmoid(c)
    o_ref[...] = jnp.dot(s.astype(bf16), w_ref[...].astype(bf16), preferred_element_type=f32) + b_ref[...]


def _modulation(c_all, w_ada, b_ada):
    rows = c_all.shape[0]
    blk = 2048
    return pl.pallas_call(
        _mod_body,
        grid=(6 * D_MODEL // blk,),
        in_specs=[pl.BlockSpec((rows, D_MODEL), lambda j: (0, 0)),
                  pl.BlockSpec((D_MODEL, blk), lambda j: (0, j)),
                  pl.BlockSpec((1, blk), lambda j: (0, j))],
        out_specs=pl.BlockSpec((rows, blk), lambda j: (0, j)),
        out_shape=jax.ShapeDtypeStruct((rows, 6 * D_MODEL), f32),
        name="adaln_mod",
    )(c_all, w_ada, b_ada)


def _attn_pair(chunks, bias_ref, sink_rows, st_s, pb_s, slot0, attT_ref, grp):
    lane_lo = lax.broadcasted_iota(jnp.int32, (CHUNK, LANES), 1) < HEAD_DIM
    n_pad = 2 * LANES - BAND
    for par, (q_c, kd_bands, v_win, pad_top, valid) in enumerate(chunks):
        for k in range(KV_HEADS):
            blocks = []
            for p in range(2):
                qc = q_c[:, (2 * k + p) * LANES:(2 * k + p + 1) * LANES]
                blocks.append(jnp.where(lane_lo, qc, jnp.zeros_like(qc)))
                blocks.append(jnp.where(lane_lo, jnp.zeros_like(qc), qc))
            qm = jnp.concatenate(blocks, axis=0)
            st_s[slot0 + par * KV_HEADS + k] = lax.dot_general(kd_bands[k], qm, _NT, preferred_element_type=f32)
    for par, (q_c, kd_bands, v_win, pad_top, valid) in enumerate(chunks):
        for k in range(KV_HEADS):
            i = slot0 + par * KV_HEADS + k
            st = st_s[i] + bias_ref[k]
            if valid is not None:
                st = jnp.where(valid, st, NEG)
            m = jnp.maximum(jnp.max(st, axis=0, keepdims=True), sink_rows[k])
            p_ = jnp.exp(st - m)
            den = jnp.sum(p_, axis=0, keepdims=True) + jnp.exp(sink_rows[k] - m)
            r0 = n_pad if pad_top else 0
            pb_s[i, r0:r0 + BAND, :] = (p_ / den).astype(bf16)
            z0 = 0 if pad_top else BAND
            pb_s[i, z0:z0 + n_pad, :] = jnp.zeros((n_pad, GQ), bf16)
    for par, (q_c, kd_bands, v_win, pad_top, valid) in enumerate(chunks):
        for k in range(KV_HEADS):
            i = slot0 + par * KV_HEADS + k
            ot = jnp.dot(v_win[k * HEAD_DIM:(k + 1) * HEAD_DIM, :], pb_s[i], preferred_element_type=f32)
            for g in range(GROUP):
                h = GROUP * k + g
                attT_ref[grp, h * HEAD_DIM:(h + 1) * HEAD_DIM, par * CHUNK:(par + 1) * CHUNK] = (
                    ot[:, g * CHUNK:(g + 1) * CHUNK])


def _attn_body(*refs, is_prompt, nb, L, n_tiles):
    if is_prompt:
        (x_ref, mod_ref, bias_ref, sink_ref, g1_ref, wq_ref, wkd_ref, wvt_ref, wkv_ref, wga_ref, wao_ref,
         a1_ref, nk_ref, nv_ref,
         h_s, q_s, kd_s, vt_s, st_s, pb_s, attT_s, att_s, ga_s) = refs
        ck_ref = cv_ref = None
    else:
        (x_ref, mod_ref, ck_ref, cv_ref, bias_ref, sink_ref, g1_ref, wq_ref, wkd_ref, wvt_ref, wkv_ref, wga_ref,
         wao_ref,
         a1_ref, nk_ref, nv_ref,
         h_s, q_s, kd_s, vt_s, st_s, pb_s, attT_s, att_s, ga_s) = refs
    T = nb * L
    n_grp = T // PAIR

    if is_prompt:
        t = pl.program_id(1)

        @pl.when(t == 0)
        def _():
            kd_s[:, 0:WINDOW, :] = jnp.zeros((KV_HEADS, WINDOW, LANES), bf16)
            vt_s[0] = jnp.zeros((KV_W, LANES), bf16)

    for s in range(nb):
        for i in range(L // ROW_BLK):
            rows = slice(i * ROW_BLK, (i + 1) * ROW_BLK)
            hh = _rms(x_ref[s, rows, :], g1_ref[...]) * (1.0 + mod_ref[s, 1:2, :]) + mod_ref[s, 0:1, :]
            h_s[s * L + i * ROW_BLK:s * L + (i + 1) * ROW_BLK, :] = hh.astype(bf16)
    hv = h_s[...]

    if not is_prompt:
        for s in range(nb):
            ck = ck_ref[s]
            for k in range(KV_HEADS):
                ckk = ck[:, k * HEAD_DIM:(k + 1) * HEAD_DIM].astype(bf16)
                kd_s[s, k, 0:WINDOW, :] = jnp.concatenate([ckk, ckk], axis=1)
            vt_s[s, :, 0:WINDOW] = cv_ref[s].T.astype(bf16)
            vt_s[s, :, BAND:2 * LANES] = jnp.zeros((KV_W, 2 * LANES - BAND), bf16)
            nk_ref[s, 0:WINDOW - L, :] = ck[L:WINDOW, :]
            nv_ref[s, 0:WINDOW - L, :] = cv_ref[s, L:WINDOW, :]
        kvf = _mm(hv, wkv_ref, 0, 2 * KV_W)
        for s in range(nb):
            nk_ref[s, WINDOW - L:WINDOW, :] = kvf[s * L:(s + 1) * L, 0:KV_W]
            nv_ref[s, WINDOW - L:WINDOW, :] = kvf[s * L:(s + 1) * L, KV_W:2 * KV_W]

    for c in range(Q_W // COL_BLK):
        qf = _mm(hv, wq_ref, c * COL_BLK, COL_BLK)
        q_s[:, _cols(c)] = (qf * (HEAD_DIM ** -0.5)).astype(bf16)
    for c in range(KV_HEADS * LANES // COL_BLK):
        kdc = _mm(hv, wkd_ref, c * COL_BLK, COL_BLK)
        for kk in range(COL_BLK // LANES):
            k = c * (COL_BLK // LANES) + kk
            kdk = kdc[:, kk * LANES:(kk + 1) * LANES].astype(bf16)
            if is_prompt:
                kd_s[k, WINDOW:WINDOW + T, :] = kdk
            else:
                for s in range(nb):
                    kd_s[s, k, WINDOW:BAND, :] = kdk[s * L:(s + 1) * L]
    vtf = lax.dot_general(wvt_ref[...], hv, _NT, preferred_element_type=f32).astype(bf16)
    if is_prompt:
        for g in range(n_grp):
            vt_s[1 + g] = vtf[:, g * LANES:(g + 1) * LANES]
    else:
        for s in range(nb):
            vt_s[s, :, WINDOW:BAND] = vtf[:, s * L:(s + 1) * L]

    lane_gq = lax.broadcasted_iota(jnp.int32, (1, GQ), 1)
    sink_rows = [_head_row(lambda h: sink_ref[h], k, lane_gq) for k in range(KV_HEADS)]

    def attn_block(c2):
        chunks = []
        if is_prompt:
            v_win = jnp.concatenate([vt_s[c2], vt_s[c2 + 1]], axis=1)
            for par in range(2):
                r = c2 * PAIR + par * CHUNK
                pos = t * T + r - WINDOW + lax.broadcasted_iota(jnp.int32, (BAND, 1), 0)
                chunks.append((q_s[r:r + CHUNK, :],
                               [kd_s[k, r:r + BAND, :] for k in range(KV_HEADS)],
                               v_win, par == 1, pos >= 0))
        else:
            for par in range(2):
                s = 2 * c2 + par
                chunks.append((q_s[s * CHUNK:(s + 1) * CHUNK, :],
                               [kd_s[s, k] for k in range(KV_HEADS)],
                               vt_s[s], False, None))
        _attn_pair(chunks, bias_ref, sink_rows, st_s, pb_s, (c2 % 2) * 2 * KV_HEADS, attT_s, c2)
        att_s[c2 * PAIR:(c2 + 1) * PAIR, :] = attT_s[c2].T.astype(bf16)

    def ga_piece(c):
        ga_s[:, _cols(c)] = _sigmoid(_mm(hv, wga_ref, c * COL_BLK, COL_BLK))

    _interleave([functools.partial(ga_piece, c) for c in range(N_CC)],
                [functools.partial(attn_block, c2) for c2 in range(n_grp)], blocks_first=True)

    av = att_s[...]
    for c in range(N_CC):
        a1c = ga_s[:, _cols(c)] * _mm(av, wao_ref, c * COL_BLK, COL_BLK)
        for s in range(nb):
            a1_ref[s, :, _cols(c)] = a1c[s * L:(s + 1) * L]

    if is_prompt:
        @pl.when(t == n_tiles - 1)
        def _():
            kvf = _mm(h_s[T - WINDOW:T, :], wkv_ref, 0, 2 * KV_W)
            nk_ref[0] = kvf[:, 0:KV_W]
            nv_ref[0] = kvf[:, KV_W:2 * KV_W]

        kd_s[:, 0:WINDOW, :] = kd_s[:, T:T + WINDOW, :]
        vt_s[0] = vt_s[n_grp]


def _attn_call(x, mod, caches, weights, nb, L, is_prompt, name):
    B, S, _ = x.shape
    T = nb * L
    if is_prompt:
        n_tiles = S // L
        grid = (B, n_tiles)
        xmap = lambda b, t: (b, t, 0)
        smap = lambda b, t: (b, 0, 0)
        kd_shape = (KV_HEADS, WINDOW + T, LANES)
        vt_shape = (1 + T // PAIR, KV_W, LANES)
    else:
        n_tiles = B // nb
        grid = (n_tiles,)
        xmap = smap = lambda i: (i, 0, 0)
        kd_shape = (nb, KV_HEADS, BAND, LANES)
        vt_shape = (nb, KV_W, 2 * LANES)
    body = functools.partial(_attn_body, is_prompt=is_prompt, nb=nb, L=L, n_tiles=n_tiles)
    cache_specs = [pl.BlockSpec((nb, WINDOW, KV_W), smap)] * len(caches)
    weight_specs = [
        _const_spec((KV_HEADS, BAND, GQ)),
        pl.BlockSpec(memory_space=pltpu.SMEM),
        _const_spec((1, D_MODEL)),
        _const_spec((D_MODEL, Q_W)),
        _const_spec((D_MODEL, KV_HEADS * LANES)),
        _const_spec((KV_W, D_MODEL)),
        _const_spec((D_MODEL, 2 * KV_W)),
        _const_spec((D_MODEL, D_MODEL)),
        _const_spec((Q_W, D_MODEL)),
    ]
    return pl.pallas_call(
        body,
        grid=grid,
        in_specs=[pl.BlockSpec((nb, L, D_MODEL), xmap), pl.BlockSpec((nb, 6, D_MODEL), smap)]
        + cache_specs + weight_specs,
        out_specs=[pl.BlockSpec((nb, L, D_MODEL), xmap),
                   pl.BlockSpec((nb, WINDOW, KV_W), smap),
                   pl.BlockSpec((nb, WINDOW, KV_W), smap)],
        out_shape=[jax.ShapeDtypeStruct((B, S, D_MODEL), f32),
                   jax.ShapeDtypeStruct((B, WINDOW, KV_W), f32),
                   jax.ShapeDtypeStruct((B, WINDOW, KV_W), f32)],
        scratch_shapes=[
            pltpu.VMEM((T, D_MODEL), bf16),
            pltpu.VMEM((T, Q_W), bf16),
            pltpu.VMEM(kd_shape, bf16),
            pltpu.VMEM(vt_shape, bf16),
            pltpu.VMEM((4 * KV_HEADS, BAND, GQ), f32),
            pltpu.VMEM((4 * KV_HEADS, 2 * LANES, GQ), bf16),
            pltpu.VMEM((T // PAIR, Q_W, PAIR), f32),
            pltpu.VMEM((T, Q_W), bf16),
            pltpu.VMEM((T, D_MODEL), f32),
        ],
        compiler_params=pltpu.CompilerParams(
            dimension_semantics=("arbitrary",) * len(grid), vmem_limit_bytes=VMEM_LIMIT),
        name=name,
    )(x, mod, *caches, *weights)


def _conv_block(zb_ref, s, j, i, dww_ref, cb_ref, out_row0):
    r0 = i * 32
    accs = [jnp.zeros((8, LANES), f32) for _ in range(4)]
    for k in range(CONV_W):
        wk = dww_ref[k:k + 1, j * LANES:(j + 1) * LANES]
        for a in range(4):
            start = r0 + (a // 2) * 16 + (a % 2) + (HIST - (CONV_W - 1)) + k
            accs[a] = accs[a] + zb_ref[s, j, pl.ds(start, 8, stride=2), :] * wk
    for a in range(4):
        start = out_row0 + r0 + (a // 2) * 16 + (a % 2)
        cb_ref[j, pl.ds(start, 8, stride=2), :] = accs[a]


def _mix_body(*refs, is_prompt, nb, L, n_tiles, tiles_per_seq):
    if is_prompt:
        (xa_ref, xb_ref, a1_ref, moda_ref, modb_ref, modc_ref, g1_ref, wglu_ref, wgb_ref, dww_ref, dwb_ref,
         lng_ref, lnb_ref, wco_ref, wo_ref, g2_ref, fg_ref, wup_ref, wdn_ref,
         y_ref, nc_ref,
         h_s, zb_s, cb_s, cbf_s, gb_s, acc_s, x1_s, h2_s, act_s, d_s) = refs
        cc_ref = None
    else:
        (xa_ref, xb_ref, a1_ref, moda_ref, modb_ref, modc_ref, cc_ref, g1_ref, wglu_ref, wgb_ref, dww_ref, dwb_ref,
         lng_ref, lnb_ref, wco_ref, wo_ref, g2_ref, fg_ref, wup_ref, wdn_ref,
         y_ref, nc_ref,
         h_s, zb_s, cb_s, cbf_s, gb_s, acc_s, x1_s, h2_s, act_s, d_s) = refs
    T = nb * L
    step = pl.program_id(0)
    t_in_seq = jnp.minimum(step, n_tiles - 1) % tiles_per_seq

    @pl.when(step == 0)
    def _():
        x1_s[...] = jnp.zeros((T, D_MODEL), f32)
        h_s[...] = jnp.zeros((T, D_MODEL), bf16)
        cb_s[...] = jnp.zeros((N_SLABS, T, LANES), f32)

    if is_prompt:
        @pl.when(t_in_seq == 0)
        def _():
            zb_s[0, :, 0:HIST, :] = jnp.zeros((N_SLABS, HIST, LANES), f32)

    def norm2_block(i):
        s = i * ROW_BLK // L
        rows = slice(i * ROW_BLK, (i + 1) * ROW_BLK)
        hh = _rms(x1_s[rows, :], g2_ref[...]) * (1.0 + modc_ref[0, s, 4:5, :]) + modc_ref[0, s, 3:4, :]
        h2_s[rows, :] = hh.astype(bf16)

    def norm1_block(i):
        s, r = divmod(i * ROW_BLK, L)
        rows = slice(i * ROW_BLK, (i + 1) * ROW_BLK)
        hh = _rms(xa_ref[0, s, r:r + ROW_BLK, :], g1_ref[...]) * (1.0 + moda_ref[0, s, 1:2, :]) + moda_ref[0, s, 0:1, :]
        h_s[rows, :] = hh.astype(bf16)

    def glu_piece(c):
        hv = h_s[...]
        za = _mm(hv, wglu_ref, c * COL_BLK, COL_BLK)
        zg = _mm(hv, wglu_ref, D_MODEL + c * COL_BLK, COL_BLK)
        zc = za * _sigmoid(zg)
        for jj in range(COL_BLK // LANES):
            j = c * (COL_BLK // LANES) + jj
            z = zc[:, jj * LANES:(jj + 1) * LANES]
            for s in range(nb):
                zb_s[s, j, HIST:HIST + L, :] = z[s * L:(s + 1) * L]
                if not is_prompt:
                    nc_ref[0, s, :, j * LANES:(j + 1) * LANES] = z[s * L + L - (CONV_W - 1):(s + 1) * L]

    def gb_piece(c):
        gb_s[:, _cols(c)] = _sigmoid(_mm(h_s[...], wgb_ref, c * COL_BLK, COL_BLK))

    def up_piece(c):
        h2v = h2_s[...]
        gate = _mm(h2v, wup_ref, c * COL_BLK, COL_BLK)
        up = _mm(h2v, wup_ref, D_FF + c * COL_BLK, COL_BLK)
        act_s[:, _cols(c)] = (gate * _sigmoid(gate) * up).astype(bf16)

    def down_piece(c):
        d_s[:, _cols(c)] = _mm(act_s[...], wdn_ref, c * COL_BLK, COL_BLK)

    def ln_block(i):
        rows = slice(i * ROW_BLK, (i + 1) * ROW_BLK)
        ys = [cb_s[j, rows, :] + dwb_ref[:, j * LANES:(j + 1) * LANES] for j in range(N_SLABS)]
        tot = ys[0]
        for j in range(1, N_SLABS):
            tot = tot + ys[j]
        mu = jnp.sum(tot, axis=-1, keepdims=True) * (1.0 / D_MODEL)
        ds = [y - mu for y in ys]
        sq = ds[0] * ds[0]
        for j in range(1, N_SLABS):
            sq = sq + ds[j] * ds[j]
        var = jnp.sum(sq, axis=-1, keepdims=True) * (1.0 / D_MODEL)
        rstd = lax.rsqrt(var + EPS)
        for j in range(N_SLABS):
            y = ds[j] * rstd * lng_ref[:, j * LANES:(j + 1) * LANES] + lnb_ref[:, j * LANES:(j + 1) * LANES]
            cbf_s[rows, j * LANES:(j + 1) * LANES] = (y * _sigmoid(y)).astype(bf16)

    def yb_piece(c):
        yb = _mm(cbf_s[...], wco_ref, c * COL_BLK, COL_BLK)
        for s in range(nb):
            rows = slice(s * L, (s + 1) * L)
            acc_s[rows, _cols(c)] = a1_ref[0, s, :, _cols(c)] + gb_s[rows, _cols(c)] * yb[rows]

    def final_block(i):
        s, r = divmod(i * ROW_BLK, L)
        rows = slice(i * ROW_BLK, (i + 1) * ROW_BLK)
        x2 = x1_s[rows, :] + modc_ref[0, s, 5:6, :] * d_s[rows, :]
        y_ref[0, s, r:r + ROW_BLK, :] = _rms(x2, fg_ref[...])

    def out_piece(c):
        m = _mm(acc_s[...].astype(bf16), wo_ref, c * COL_BLK, COL_BLK)
        for s in range(nb):
            rows = slice(s * L, (s + 1) * L)
            x1_s[rows, _cols(c)] = xb_ref[0, s, :, _cols(c)] + modb_ref[0, s, 2:3, _cols(c)] * m[rows]

    n_blk = T // ROW_BLK
    _interleave([functools.partial(gb_piece, c) for c in range(N_CC)],
                [functools.partial(ln_block, i) for i in range(n_blk)])
    for i in range(n_blk):
        norm1_block(i)
    if not is_prompt:
        for s in range(nb):
            for j in range(N_SLABS):
                zb_s[s, j, HIST - (CONV_W - 1):HIST, :] = cc_ref[0, s, :, j * LANES:(j + 1) * LANES]
    _interleave([functools.partial(glu_piece, c) for c in range(N_CC)],
                [functools.partial(norm2_block, i) for i in range(n_blk)])
    pieces = [functools.partial(up_piece, c) for c in range(N_FF)]
    pieces += [functools.partial(yb_piece, c) for c in range(N_CC)]
    pieces += [functools.partial(down_piece, c) for c in range(N_CC)]
    conv_blocks = [functools.partial(_conv_block, zb_s, s, j, i, dww_ref, cb_s, s * L)
                   for j in range(N_SLABS) for s in range(nb) for i in range(L // 32)]
    _interleave(pieces, conv_blocks)
    for i in range(n_blk):
        final_block(i)
    for c in range(N_CC):
        out_piece(c)

    if is_prompt:
        @pl.when(t_in_seq == tiles_per_seq - 1)
        def _():
            for j in range(N_SLABS):
                nc_ref[0, 0, :, j * LANES:(j + 1) * LANES] = zb_s[0, j, HIST + L - (CONV_W - 1):HIST + L, :]

        zb_s[0, :, 0:HIST, :] = zb_s[0, :, L:L + HIST, :]


def _mix_call(x, a1, mod, cc, weights, nb, L, is_prompt, name):
    n_tiles = x.shape[0]
    tiles_per_seq = n_tiles // mod.shape[0]
    T = nb * L

    def lagged(lag, per_group):
        def index_map(i):
            t = jnp.clip(i - lag, 0, n_tiles - 1)
            return (t // per_group, 0, 0, 0)
        return index_map

    body = functools.partial(_mix_body, is_prompt=is_prompt, nb=nb, L=L, n_tiles=n_tiles,
                             tiles_per_seq=tiles_per_seq)
    row = _const_spec((1, D_MODEL))
    tile_blk = (1, nb, L, D_MODEL)
    mod_blk = (1, nb, 6, D_MODEL)
    in_specs = [pl.BlockSpec(tile_blk, lagged(0, 1)),
                pl.BlockSpec(tile_blk, lagged(1, 1)),
                pl.BlockSpec(tile_blk, lagged(1, 1)),
                pl.BlockSpec(mod_blk, lagged(0, tiles_per_seq)),
                pl.BlockSpec(mod_blk, lagged(1, tiles_per_seq)),
                pl.BlockSpec(mod_blk, lagged(2, tiles_per_seq))]
    args = [x, x, a1, mod, mod, mod]
    if not is_prompt:
        in_specs.append(pl.BlockSpec((1, nb, CONV_W - 1, D_MODEL), lagged(0, 1)))
        args.append(cc)
    in_specs += [
        row,
        _const_spec((D_MODEL, 2 * D_MODEL)),
        _const_spec((D_MODEL, D_MODEL)),
        _const_spec((CONV_W, D_MODEL)),
        row, row, row,
        _const_spec((D_MODEL, D_MODEL)),
        _const_spec((D_MODEL, D_MODEL)),
        row, row,
        _const_spec((D_MODEL, 2 * D_FF)),
        _const_spec((D_FF, D_MODEL)),
    ]
    n_groups = mod.shape[0]
    return pl.pallas_call(
        body,
        grid=(n_tiles + 2,),
        in_specs=in_specs,
        out_specs=[pl.BlockSpec(tile_blk, lagged(2, 1)),
                   pl.BlockSpec((1, nb, CONV_W - 1, D_MODEL), lagged(0, tiles_per_seq))],
        out_shape=[jax.ShapeDtypeStruct(x.shape, f32),
                   jax.ShapeDtypeStruct((n_groups, nb, CONV_W - 1, D_MODEL), f32)],
        scratch_shapes=[
            pltpu.VMEM((T, D_MODEL), bf16),
            pltpu.VMEM((nb, N_SLABS, HIST + L, LANES), f32),
            pltpu.VMEM((N_SLABS, T, LANES), f32),
            pltpu.VMEM((T, D_MODEL), bf16),
            pltpu.VMEM((T, D_MODEL), f32),
            pltpu.VMEM((T, D_MODEL), f32),
            pltpu.VMEM((T, D_MODEL), f32),
            pltpu.VMEM((T, D_MODEL), bf16),
            pltpu.VMEM((T, D_FF), bf16),
            pltpu.VMEM((T, D_MODEL), f32),
        ],
        compiler_params=pltpu.CompilerParams(
            dimension_semantics=("arbitrary",), vmem_limit_bytes=VMEM_LIMIT),
        name=name,
    )(*args, *weights)


def kernel(x_prompt, x_sample, cache_k, cache_v, cache_conv, c_prompt, c_sample, rel_table, w_ada, b_ada, norm1_g, norm2_g, w_in, sink, w_attn_out, dw_w, dw_b, conv_ln_g, conv_ln_b, w_conv_out, w_out, w_ffn_up, w_ffn_down, final_g):
    assert w_ada.shape[0] == 1, "single-layer kernel"
    B, S, _ = x_prompt.shape
    DB, DS, _ = x_sample.shape
    nb_attn = ATTN_SAMPLE_TOKENS // DS
    nb_mix = MIX_TOKENS // DS

    bias = _bias_table(rel_table)
    n_seq = B + DB
    pad = (-n_seq) % 8
    c_all = jnp.concatenate([c_prompt, c_sample, jnp.zeros((pad, D_MODEL), f32)], axis=0)
    mod = _modulation(c_all, w_ada[0], b_ada).reshape(n_seq + pad, 6, D_MODEL)
    mod_p, mod_s = mod[:B], mod[B:B + DB]

    row = lambda v: v.reshape(1, D_MODEL)
    w = w_in[0]
    w_k = w[:, O_K:O_V].astype(bf16).reshape(D_MODEL, KV_HEADS, 1, HEAD_DIM)
    w_kd = jnp.broadcast_to(w_k, (D_MODEL, KV_HEADS, 2, HEAD_DIM)).reshape(D_MODEL, KV_HEADS * LANES)
    attn_w = (bias, sink[0], row(norm1_g[0]), w[:, :Q_W].astype(bf16), w_kd, w[:, O_V:O_GLU].T.astype(bf16),
              w[:, O_K:O_GLU].astype(bf16), w[:, O_GA:O_GB].astype(bf16), w_attn_out[0].astype(bf16))
    mix_w = (row(norm1_g[0]), w[:, O_GLU:O_GA].astype(bf16), w[:, O_GB:].astype(bf16), dw_w[0], row(dw_b[0]),
             row(conv_ln_g[0]), row(conv_ln_b[0]), w_conv_out[0].astype(bf16), w_out[0].astype(bf16),
             row(norm2_g[0]), row(final_g), w_ffn_up[0].astype(bf16), w_ffn_down[0].astype(bf16))

    a1p, nkp, nvp = _attn_call(x_prompt, mod_p, (), attn_w, 1, ATTN_TOKENS, True, "attn_prompt")
    ck = cache_k[0].reshape(DB, WINDOW, KV_W)
    cv = cache_v[0].reshape(DB, WINDOW, KV_W)
    a1s, nks, nvs = _attn_call(x_sample, mod_s, (ck, cv), attn_w, nb_attn, DS, False, "attn_sample")

    tiles = lambda a, n, l: a.reshape(-1, n, l, D_MODEL)
    y_p, ncp = _mix_call(tiles(x_prompt, 1, MIX_TOKENS), tiles(a1p, 1, MIX_TOKENS), mod_p[:, None], None,
                         mix_w, 1, MIX_TOKENS, True, "mix_prompt")
    y_s, ncs = _mix_call(tiles(x_sample, nb_mix, DS), tiles(a1s, nb_mix, DS), tiles(mod_s, nb_mix, 6),
                         tiles(cache_conv[0], nb_mix, CONV_W - 1), mix_w, nb_mix, DS, False, "mix_sample")

    kv5 = lambda a: a.reshape(1, a.shape[0], WINDOW, KV_HEADS, HEAD_DIM)
    return (y_p.reshape(B, S, D_MODEL), y_s.reshape(DB, DS, D_MODEL), kv5(nkp), kv5(nvp),
            ncp.reshape(1, B, CONV_W - 1, D_MODEL), kv5(nks), kv5(nvs), ncs.reshape(1, DB, CONV_W - 1, D_MODEL))
```

```python
import functools
import math

import jax
import jax.numpy as jnp
from jax import lax
from jax.experimental import pallas as pl
from jax.experimental.pallas import tpu as pltpu

f32 = jnp.float32
bf16 = jnp.bfloat16

D_MODEL = 1024
N_HEADS = 16
KV_HEADS = 4
HEAD_DIM = 64
GROUP = N_HEADS // KV_HEADS
WINDOW = 128
CHUNK = 64
BAND = WINDOW + CHUNK
CONV_W = 31
D_FF = 2816
NUM_BUCKETS = 32
MAX_DISTANCE = 128
EPS = 1e-6
NEG = -1e30
Q_W = N_HEADS * HEAD_DIM
KV_W = KV_HEADS * HEAD_DIM
O_K = Q_W
O_V = O_K + KV_W
O_GLU = O_V + KV_W
O_GA = O_GLU + 2 * D_MODEL
O_GB = O_GA + D_MODEL
IN_W = O_GB + D_MODEL

LANES = 128
N_SLABS = D_MODEL // LANES
PAIR = 2 * CHUNK
GQ = GROUP * CHUNK
HIST = 32
ROW_BLK = 32
COL_BLK = 256
N_CC = D_MODEL // COL_BLK
N_FF = D_FF // COL_BLK
VMEM_LIMIT = 56 * 1024 * 1024
ATTN_TOKENS = 1024
ATTN_SAMPLE_TOKENS = 512
MIX_TOKENS = 256

_NT = (((1,), (1,)), ((), ()))


def _sigmoid(x):
    return 1.0 / (1.0 + jnp.exp(-x))


def _rms(x, g):
    ms = jnp.mean(x * x, axis=-1, keepdims=True)
    return x * lax.rsqrt(ms + EPS) * g


def _mm(a, w_ref, c0, width):
    return jnp.dot(a, w_ref[:, c0:c0 + width], preferred_element_type=f32)


def _cols(c):
    return slice(c * COL_BLK, (c + 1) * COL_BLK)


def _head_row(vals, k, lane):
    row = jnp.full((1, GQ), vals(GROUP * k + GROUP - 1), f32)
    for g in range(GROUP - 2, -1, -1):
        row = jnp.where(lane < CHUNK * (g + 1), vals(GROUP * k + g), row)
    return row


def _interleave(pieces, blocks, blocks_first=False):
    order = [((n + 0.5) / len(pieces), int(blocks_first), fn) for n, fn in enumerate(pieces)]
    order += [((n + 0.5) / len(blocks), int(not blocks_first), fn) for n, fn in enumerate(blocks)]
    for _, _, fn in sorted(order, key=lambda item: item[:2]):
        fn()


def _const_spec(shape):
    nd = len(shape)
    return pl.BlockSpec(shape, lambda *_: (0,) * nd, pipeline_mode=pl.Buffered(1))


def _rel_bucket(rel):
    nb = NUM_BUCKETS // 2
    max_exact = nb // 2
    ret = (rel > 0).astype(jnp.int32) * nb
    n = jnp.abs(rel)
    nf = jnp.maximum(n, 1).astype(f32)
    large = max_exact + (jnp.log(nf / max_exact) / math.log(MAX_DISTANCE / max_exact)
                         * (nb - max_exact)).astype(jnp.int32)
    large = jnp.minimum(large, nb - 1)
    return ret + jnp.where(n < max_exact, n, large)


def _bias_body(idx_ref, tab_ref, o_ref):
    idx = idx_ref[...]
    lane = lax.broadcasted_iota(jnp.int32, (1, GQ), 1)
    for k in range(KV_HEADS):
        acc = jnp.zeros((BAND, GQ), f32)
        for b in range(NUM_BUCKETS):
            acc = jnp.where(idx == b, _head_row(lambda h: tab_ref[b, h], k, lane), acc)
        o_ref[k] = acc


def _bias_table(rel_table):
    kj = jnp.arange(BAND, dtype=jnp.int32)
    rel = kj[None, :] - WINDOW - jnp.arange(CHUNK, dtype=jnp.int32)[:, None]
    idx_t = jnp.tile(_rel_bucket(rel).T, (1, GROUP))
    return pl.pallas_call(
        _bias_body,
        in_specs=[pl.BlockSpec(memory_space=pltpu.VMEM), pl.BlockSpec(memory_space=pltpu.SMEM)],
        out_specs=pl.BlockSpec(memory_space=pltpu.VMEM),
        out_shape=jax.ShapeDtypeStruct((KV_HEADS, BAND, GQ), f32),
        name="rel_bias",
    )(idx_t, rel_table)


def _mod_body(c_ref, w_ref, b_ref, o_ref):
    c = c_ref[...]
    s = c * _sigmoid(c)
    o_ref[...] = jnp.dot(s.astype(bf16), w_ref[...].astype(bf16), preferred_element_type=f32) + b_ref[...]


def _modulation(c_all, w_ada, b_ada):
    rows = c_all.shape[0]
    blk = 512
    return pl.pallas_call(
        _mod_body,
        grid=(6 * D_MODEL // blk,),
        in_specs=[pl.BlockSpec((rows, D_MODEL), lambda j: (0, 0)),
                  pl.BlockSpec((D_MODEL, blk), lambda j: (0, j)),
                  pl.BlockSpec((1, blk), lambda j: (0, j))],
        out_specs=pl.BlockSpec((rows, blk), lambda j: (0, j)),
        out_shape=jax.ShapeDtypeStruct((rows, 6 * D_MODEL), f32),
        name="adaln_mod",
    )(c_all, w_ada, b_ada)


def _attn_pair(chunks, bias_ref, sink_rows, st_s, pb_s, slot0, attT_ref, grp):
    lane_lo = lax.broadcasted_iota(jnp.int32, (CHUNK, LANES), 1) < HEAD_DIM
    n_pad = 2 * LANES - BAND
    for par, (q_c, kd_bands, v_win, pad_top, valid) in enumerate(chunks):
        for k in range(KV_HEADS):
            blocks = []
            for p in range(2):
                qc = q_c[:, (2 * k + p) * LANES:(2 * k + p + 1) * LANES]
                blocks.append(jnp.where(lane_lo, qc, jnp.zeros_like(qc)))
                blocks.append(jnp.where(lane_lo, jnp.zeros_like(qc), qc))
            qm = jnp.concatenate(blocks, axis=0)
            st_s[slot0 + par * KV_HEADS + k] = lax.dot_general(kd_bands[k], qm, _NT, preferred_element_type=f32)
    for par, (q_c, kd_bands, v_win, pad_top, valid) in enumerate(chunks):
        for k in range(KV_HEADS):
            i = slot0 + par * KV_HEADS + k
            st = st_s[i] + bias_ref[k]
            if valid is not None:
                st = jnp.where(valid, st, NEG)
            m = jnp.maximum(jnp.max(st, axis=0, keepdims=True), sink_rows[k])
            p_ = jnp.exp(st - m)
            den = jnp.sum(p_, axis=0, keepdims=True) + jnp.exp(sink_rows[k] - m)
            r0 = n_pad if pad_top else 0
            pb_s[i, r0:r0 + BAND, :] = (p_ / den).astype(bf16)
            z0 = 0 if pad_top else BAND
            pb_s[i, z0:z0 + n_pad, :] = jnp.zeros((n_pad, GQ), bf16)
    for par, (q_c, kd_bands, v_win, pad_top, valid) in enumerate(chunks):
        for k in range(KV_HEADS):
            i = slot0 + par * KV_HEADS + k
            ot = jnp.dot(v_win[k * HEAD_DIM:(k + 1) * HEAD_DIM, :], pb_s[i], preferred_element_type=f32)
            for g in range(GROUP):
                h = GROUP * k + g
                attT_ref[grp, h * HEAD_DIM:(h + 1) * HEAD_DIM, par * CHUNK:(par + 1) * CHUNK] = (
                    ot[:, g * CHUNK:(g + 1) * CHUNK])


def _attn_body(*refs, is_prompt, nb, L, n_tiles):
    if is_prompt:
        (x_ref, mod_ref, bias_ref, sink_ref, g1_ref, win_ref, wkd_ref, wvt_ref, wao_ref,
         a1_ref, nk_ref, nv_ref,
         h_s, q_s, kd_s, vt_s, st_s, pb_s, attT_s, att_s, ga_s) = refs
        ck_ref = cv_ref = None
    else:
        (x_ref, mod_ref, ck_ref, cv_ref, bias_ref, sink_ref, g1_ref, win_ref, wkd_ref, wvt_ref, wao_ref,
         a1_ref, nk_ref, nv_ref,
         h_s, q_s, kd_s, vt_s, st_s, pb_s, attT_s, att_s, ga_s) = refs
    T = nb * L
    n_grp = T // PAIR

    if is_prompt:
        t = pl.program_id(1)

        @pl.when(t == 0)
        def _():
            kd_s[:, 0:WINDOW, :] = jnp.zeros((KV_HEADS, WINDOW, LANES), bf16)
            vt_s[0] = jnp.zeros((KV_W, LANES), bf16)

    for s in range(nb):
        for i in range(L // ROW_BLK):
            rows = slice(i * ROW_BLK, (i + 1) * ROW_BLK)
            hh = _rms(x_ref[s, rows, :], g1_ref[...]) * (1.0 + mod_ref[s, 1:2, :]) + mod_ref[s, 0:1, :]
            h_s[s * L + i * ROW_BLK:s * L + (i + 1) * ROW_BLK, :] = hh.astype(bf16)
    hv = h_s[...]

    if not is_prompt:
        for s in range(nb):
            ck = ck_ref[s]
            for k in range(KV_HEADS):
                ckk = ck[:, k * HEAD_DIM:(k + 1) * HEAD_DIM].astype(bf16)
                kd_s[s, k, 0:WINDOW, :] = jnp.concatenate([ckk, ckk], axis=1)
            vt_s[s, :, 0:WINDOW] = cv_ref[s].T.astype(bf16)
            vt_s[s, :, BAND:2 * LANES] = jnp.zeros((KV_W, 2 * LANES - BAND), bf16)
            nk_ref[s, 0:WINDOW - L, :] = ck[L:WINDOW, :]
            nv_ref[s, 0:WINDOW - L, :] = cv_ref[s, L:WINDOW, :]
        kvf = _mm(hv, win_ref, O_K, 2 * KV_W)
        for s in range(nb):
            nk_ref[s, WINDOW - L:WINDOW, :] = kvf[s * L:(s + 1) * L, 0:KV_W]
            nv_ref[s, WINDOW - L:WINDOW, :] = kvf[s * L:(s + 1) * L, KV_W:2 * KV_W]

    for c in range(Q_W // COL_BLK):
        qf = _mm(hv, win_ref, c * COL_BLK, COL_BLK)
        q_s[:, _cols(c)] = (qf * (HEAD_DIM ** -0.5)).astype(bf16)
    for c in range(KV_HEADS * LANES // COL_BLK):
        kdc = _mm(hv, wkd_ref, c * COL_BLK, COL_BLK)
        for kk in range(COL_BLK // LANES):
            k = c * (COL_BLK // LANES) + kk
            kdk = kdc[:, kk * LANES:(kk + 1) * LANES].astype(bf16)
            if is_prompt:
                kd_s[k, WINDOW:WINDOW + T, :] = kdk
            else:
                for s in range(nb):
                    kd_s[s, k, WINDOW:BAND, :] = kdk[s * L:(s + 1) * L]
    vtf = lax.dot_general(wvt_ref[...], hv, _NT, preferred_element_type=f32).astype(bf16)
    if is_prompt:
        for g in range(n_grp):
            vt_s[1 + g] = vtf[:, g * LANES:(g + 1) * LANES]
    else:
        for s in range(nb):
            vt_s[s, :, WINDOW:BAND] = vtf[:, s * L:(s + 1) * L]

    lane_gq = lax.broadcasted_iota(jnp.int32, (1, GQ), 1)
    sink_rows = [_head_row(lambda h: sink_ref[h], k, lane_gq) for k in range(KV_HEADS)]

    def attn_block(c2):
        chunks = []
        if is_prompt:
            v_win = jnp.concatenate([vt_s[c2], vt_s[c2 + 1]], axis=1)
            for par in range(2):
                r = c2 * PAIR + par * CHUNK
                pos = t * T + r - WINDOW + lax.broadcasted_iota(jnp.int32, (BAND, 1), 0)
                chunks.append((q_s[r:r + CHUNK, :],
                               [kd_s[k, r:r + BAND, :] for k in range(KV_HEADS)],
                               v_win, par == 1, pos >= 0))
        else:
            for par in range(2):
                s = 2 * c2 + par
                chunks.append((q_s[s * CHUNK:(s + 1) * CHUNK, :],
                               [kd_s[s, k] for k in range(KV_HEADS)],
                               vt_s[s], False, None))
        _attn_pair(chunks, bias_ref, sink_rows, st_s, pb_s, (c2 % 2) * 2 * KV_HEADS, attT_s, c2)
        att_s[c2 * PAIR:(c2 + 1) * PAIR, :] = attT_s[c2].T.astype(bf16)

    def ga_piece(c):
        ga_s[:, _cols(c)] = _sigmoid(_mm(hv, win_ref, O_GA + c * COL_BLK, COL_BLK))

    _interleave([functools.partial(ga_piece, c) for c in range(N_CC)],
                [functools.partial(attn_block, c2) for c2 in range(n_grp)], blocks_first=True)

    av = att_s[...]
    for c in range(N_CC):
        a1c = ga_s[:, _cols(c)] * _mm(av, wao_ref, c * COL_BLK, COL_BLK)
        for s in range(nb):
            a1_ref[s, :, _cols(c)] = a1c[s * L:(s + 1) * L]

    if is_prompt:
        @pl.when(t == n_tiles - 1)
        def _():
            kvf = _mm(h_s[T - WINDOW:T, :], win_ref, O_K, 2 * KV_W)
            nk_ref[0] = kvf[:, 0:KV_W]
            nv_ref[0] = kvf[:, KV_W:2 * KV_W]

        kd_s[:, 0:WINDOW, :] = kd_s[:, T:T + WINDOW, :]
        vt_s[0] = vt_s[n_grp]


def _attn_call(x, mod, caches, weights, nb, L, is_prompt, name):
    B, S, _ = x.shape
    T = nb * L
    if is_prompt:
        n_tiles = S // L
        grid = (B, n_tiles)
        xmap = lambda b, t: (b, t, 0)
        smap = lambda b, t: (b, 0, 0)
        kd_shape = (KV_HEADS, WINDOW + T, LANES)
        vt_shape = (1 + T // PAIR, KV_W, LANES)
    else:
        n_tiles = B // nb
        grid = (n_tiles,)
        xmap = smap = lambda i: (i, 0, 0)
        kd_shape = (nb, KV_HEADS, BAND, LANES)
        vt_shape = (nb, KV_W, 2 * LANES)
    body = functools.partial(_attn_body, is_prompt=is_prompt, nb=nb, L=L, n_tiles=n_tiles)
    cache_specs = [pl.BlockSpec((nb, WINDOW, KV_W), smap)] * len(caches)
    weight_specs = [
        _const_spec((KV_HEADS, BAND, GQ)),
        pl.BlockSpec(memory_space=pltpu.SMEM),
        _const_spec((1, D_MODEL)),
        _const_spec((D_MODEL, IN_W)),
        _const_spec((D_MODEL, KV_HEADS * LANES)),
        _const_spec((KV_W, D_MODEL)),
        _const_spec((Q_W, D_MODEL)),
    ]
    return pl.pallas_call(
        body,
        grid=grid,
        in_specs=[pl.BlockSpec((nb, L, D_MODEL), xmap), pl.BlockSpec((nb, 6, D_MODEL), smap)]
        + cache_specs + weight_specs,
        out_specs=[pl.BlockSpec((nb, L, D_MODEL), xmap),
                   pl.BlockSpec((nb, WINDOW, KV_W), smap),
                   pl.BlockSpec((nb, WINDOW, KV_W), smap)],
        out_shape=[jax.ShapeDtypeStruct((B, S, D_MODEL), f32),
                   jax.ShapeDtypeStruct((B, WINDOW, KV_W), f32),
                   jax.ShapeDtypeStruct((B, WINDOW, KV_W), f32)],
        scratch_shapes=[
            pltpu.VMEM((T, D_MODEL), bf16),
            pltpu.VMEM((T, Q_W), bf16),
            pltpu.VMEM(kd_shape, bf16),
            pltpu.VMEM(vt_shape, bf16),
            pltpu.VMEM((4 * KV_HEADS, BAND, GQ), f32),
            pltpu.VMEM((4 * KV_HEADS, 2 * LANES, GQ), bf16),
            pltpu.VMEM((T // PAIR, Q_W, PAIR), f32),
            pltpu.VMEM((T, Q_W), bf16),
            pltpu.VMEM((T, D_MODEL), f32),
        ],
        compiler_params=pltpu.CompilerParams(
            dimension_semantics=("arbitrary",) * len(grid), vmem_limit_bytes=VMEM_LIMIT),
        name=name,
    )(x, mod, *caches, *weights)


def _conv_block(zb_ref, s, j, i, dww_ref, cb_ref, out_row0):
    r0 = i * 32
    accs = [jnp.zeros((8, LANES), f32) for _ in range(4)]
    for k in range(CONV_W):
        wk = dww_ref[k:k + 1, j * LANES:(j + 1) * LANES]
        for a in range(4):
            start = r0 + (a // 2) * 16 + (a % 2) + (HIST - (CONV_W - 1)) + k
            accs[a] = accs[a] + zb_ref[s, j, pl.ds(start, 8, stride=2), :] * wk
    for a in range(4):
        start = out_row0 + r0 + (a // 2) * 16 + (a % 2)
        cb_ref[j, pl.ds(start, 8, stride=2), :] = accs[a]


def _mix_body(*refs, is_prompt, nb, L, n_tiles, tiles_per_seq):
    if is_prompt:
        (xa_ref, xb_ref, a1_ref, moda_ref, modb_ref, modc_ref, g1_ref, win_ref, dww_ref, dwb_ref,
         lng_ref, lnb_ref, wco_ref, wo_ref, g2_ref, fg_ref, wup_ref, wdn_ref,
         y_ref, nc_ref,
         h_s, zb_s, cb_s, cbf_s, gb_s, acc_s, x1_s, h2_s, act_s, d_s) = refs
        cc_ref = None
    else:
        (xa_ref, xb_ref, a1_ref, moda_ref, modb_ref, modc_ref, cc_ref, g1_ref, win_ref, dww_ref, dwb_ref,
         lng_ref, lnb_ref, wco_ref, wo_ref, g2_ref, fg_ref, wup_ref, wdn_ref,
         y_ref, nc_ref,
         h_s, zb_s, cb_s, cbf_s, gb_s, acc_s, x1_s, h2_s, act_s, d_s) = refs
    T = nb * L
    step = pl.program_id(0)
    t_in_seq = jnp.minimum(step, n_tiles - 1) % tiles_per_seq

    @pl.when(step == 0)
    def _():
        x1_s[...] = jnp.zeros((T, D_MODEL), f32)
        h_s[...] = jnp.zeros((T, D_MODEL), bf16)
        cb_s[...] = jnp.zeros((N_SLABS, T, LANES), f32)

    if is_prompt:
        @pl.when(t_in_seq == 0)
        def _():
            zb_s[0, :, 0:HIST, :] = jnp.zeros((N_SLABS, HIST, LANES), f32)

    def norm2_block(i):
        s = i * ROW_BLK // L
        rows = slice(i * ROW_BLK, (i + 1) * ROW_BLK)
        hh = _rms(x1_s[rows, :], g2_ref[...]) * (1.0 + modc_ref[0, s, 4:5, :]) + modc_ref[0, s, 3:4, :]
        h2_s[rows, :] = hh.astype(bf16)

    def norm1_block(i):
        s, r = divmod(i * ROW_BLK, L)
        rows = slice(i * ROW_BLK, (i + 1) * ROW_BLK)
        hh = _rms(xa_ref[0, s, r:r + ROW_BLK, :], g1_ref[...]) * (1.0 + moda_ref[0, s, 1:2, :]) + moda_ref[0, s, 0:1, :]
        h_s[rows, :] = hh.astype(bf16)

    def glu_piece(c):
        hv = h_s[...]
        za = _mm(hv, win_ref, O_GLU + c * COL_BLK, COL_BLK)
        zg = _mm(hv, win_ref, O_GLU + D_MODEL + c * COL_BLK, COL_BLK)
        zc = za * _sigmoid(zg)
        for jj in range(COL_BLK // LANES):
            j = c * (COL_BLK // LANES) + jj
            z = zc[:, jj * LANES:(jj + 1) * LANES]
            for s in range(nb):
                zb_s[s, j, HIST:HIST + L, :] = z[s * L:(s + 1) * L]
                if not is_prompt:
                    nc_ref[0, s, :, j * LANES:(j + 1) * LANES] = z[s * L + L - (CONV_W - 1):(s + 1) * L]

    def gb_piece(c):
        gb_s[:, _cols(c)] = _sigmoid(_mm(h_s[...], win_ref, O_GB + c * COL_BLK, COL_BLK))

    def up_piece(c):
        h2v = h2_s[...]
        gate = _mm(h2v, wup_ref, c * COL_BLK, COL_BLK)
        up = _mm(h2v, wup_ref, D_FF + c * COL_BLK, COL_BLK)
        act_s[c % 2] = (gate * _sigmoid(gate) * up).astype(bf16)

    def down_piece(c):
        d_s[...] += jnp.dot(act_s[c % 2], wdn_ref[c * COL_BLK:(c + 1) * COL_BLK, :], preferred_element_type=f32)

    def ln_block(i):
        rows = slice(i * ROW_BLK, (i + 1) * ROW_BLK)
        ys = [cb_s[j, rows, :] + dwb_ref[:, j * LANES:(j + 1) * LANES] for j in range(N_SLABS)]
        tot = ys[0]
        for j in range(1, N_SLABS):
            tot = tot + ys[j]
        mu = jnp.sum(tot, axis=-1, keepdims=True) * (1.0 / D_MODEL)
        ds = [y - mu for y in ys]
        sq = ds[0] * ds[0]
        for j in range(1, N_SLABS):
            sq = sq + ds[j] * ds[j]
        var = jnp.sum(sq, axis=-1, keepdims=True) * (1.0 / D_MODEL)
        rstd = lax.rsqrt(var + EPS)
        for j in range(N_SLABS):
            y = ds[j] * rstd * lng_ref[:, j * LANES:(j + 1) * LANES] + lnb_ref[:, j * LANES:(j + 1) * LANES]
            cbf_s[rows, j * LANES:(j + 1) * LANES] = (y * _sigmoid(y)).astype(bf16)

    def yb_piece(c):
        yb = _mm(cbf_s[...], wco_ref, c * COL_BLK, COL_BLK)
        for s in range(nb):
            rows = slice(s * L, (s + 1) * L)
            acc_s[rows, _cols(c)] = a1_ref[0, s, :, _cols(c)] + gb_s[rows, _cols(c)] * yb[rows]

    def final_block(i):
        s, r = divmod(i * ROW_BLK, L)
        rows = slice(i * ROW_BLK, (i + 1) * ROW_BLK)
        x2 = x1_s[rows, :] + modc_ref[0, s, 5:6, :] * d_s[rows, :]
        y_ref[0, s, r:r + ROW_BLK, :] = _rms(x2, fg_ref[...])

    def out_piece(c):
        m = _mm(acc_s[...].astype(bf16), wo_ref, c * COL_BLK, COL_BLK)
        for s in range(nb):
            rows = slice(s * L, (s + 1) * L)
            x1_s[rows, _cols(c)] = xb_ref[0, s, :, _cols(c)] + modb_ref[0, s, 2:3, _cols(c)] * m[rows]

    n_blk = T // ROW_BLK
    _interleave([functools.partial(gb_piece, c) for c in range(N_CC)],
                [functools.partial(ln_block, i) for i in range(n_blk)])
    for i in range(n_blk):
        norm1_block(i)
    if not is_prompt:
        for s in range(nb):
            for j in range(N_SLABS):
                zb_s[s, j, HIST - (CONV_W - 1):HIST, :] = cc_ref[0, s, :, j * LANES:(j + 1) * LANES]
    _interleave([functools.partial(glu_piece, c) for c in range(N_CC)],
                [functools.partial(norm2_block, i) for i in range(n_blk)])
    d_s[...] = jnp.zeros((T, D_MODEL), f32)
    up_piece(0)
    pieces = []
    for c in range(N_FF):
        if c + 1 < N_FF:
            pieces.append(functools.partial(up_piece, c + 1))
        pieces.append(functools.partial(down_piece, c))
    pieces += [functools.partial(yb_piece, c) for c in range(N_CC)]
    conv_blocks = [functools.partial(_conv_block, zb_s, s, j, i, dww_ref, cb_s, s * L)
                   for j in range(N_SLABS) for s in range(nb) for i in range(L // 32)]
    _interleave(pieces, conv_blocks)
    for i in range(n_blk):
        final_block(i)
    for c in range(N_CC):
        out_piece(c)

    if is_prompt:
        @pl.when(t_in_seq == tiles_per_seq - 1)
        def _():
            for j in range(N_SLABS):
                nc_ref[0, 0, :, j * LANES:(j + 1) * LANES] = zb_s[0, j, HIST + L - (CONV_W - 1):HIST + L, :]

        zb_s[0, :, 0:HIST, :] = zb_s[0, :, L:L + HIST, :]


def _mix_call(x, a1, mod, cc, weights, nb, L, is_prompt, name):
    n_tiles = x.shape[0]
    tiles_per_seq = n_tiles // mod.shape[0]
    T = nb * L

    def lagged(lag, per_group):
        def index_map(i):
            t = jnp.clip(i - lag, 0, n_tiles - 1)
            return (t // per_group, 0, 0, 0)
        return index_map

    body = functools.partial(_mix_body, is_prompt=is_prompt, nb=nb, L=L, n_tiles=n_tiles,
                             tiles_per_seq=tiles_per_seq)
    row = _const_spec((1, D_MODEL))
    tile_blk = (1, nb, L, D_MODEL)
    mod_blk = (1, nb, 6, D_MODEL)
    in_specs = [pl.BlockSpec(tile_blk, lagged(0, 1)),
                pl.BlockSpec(tile_blk, lagged(1, 1)),
                pl.BlockSpec(tile_blk, lagged(1, 1)),
                pl.BlockSpec(mod_blk, lagged(0, tiles_per_seq)),
                pl.BlockSpec(mod_blk, lagged(1, tiles_per_seq)),
                pl.BlockSpec(mod_blk, lagged(2, tiles_per_seq))]
    args = [x, x, a1, mod, mod, mod]
    if not is_prompt:
        in_specs.append(pl.BlockSpec((1, nb, CONV_W - 1, D_MODEL), lagged(0, 1)))
        args.append(cc)
    in_specs += [
        row,
        _const_spec((D_MODEL, IN_W)),
        _const_spec((CONV_W, D_MODEL)),
        row, row, row,
        _const_spec((D_MODEL, D_MODEL)),
        _const_spec((D_MODEL, D_MODEL)),
        row, row,
        _const_spec((D_MODEL, 2 * D_FF)),
        _const_spec((D_FF, D_MODEL)),
    ]
    n_groups = mod.shape[0]
    return pl.pallas_call(
        body,
        grid=(n_tiles + 2,),
        in_specs=in_specs,
        out_specs=[pl.BlockSpec(tile_blk, lagged(2, 1)),
                   pl.BlockSpec((1, nb, CONV_W - 1, D_MODEL), lagged(0, tiles_per_seq))],
        out_shape=[jax.ShapeDtypeStruct(x.shape, f32),
                   jax.ShapeDtypeStruct((n_groups, nb, CONV_W - 1, D_MODEL), f32)],
        scratch_shapes=[
            pltpu.VMEM((T, D_MODEL), bf16),
            pltpu.VMEM((nb, N_SLABS, HIST + L, LANES), f32),
            pltpu.VMEM((N_SLABS, T, LANES), f32),
            pltpu.VMEM((T, D_MODEL), bf16),
            pltpu.VMEM((T, D_MODEL), f32),
            pltpu.VMEM((T, D_MODEL), f32),
            pltpu.VMEM((T, D_MODEL), f32),
            pltpu.VMEM((T, D_MODEL), bf16),
            pltpu.VMEM((2, T, COL_BLK), bf16),
            pltpu.VMEM((T, D_MODEL), f32),
        ],
        compiler_params=pltpu.CompilerParams(
            dimension_semantics=("arbitrary",), vmem_limit_bytes=VMEM_LIMIT),
        name=name,
    )(*args, *weights)


def kernel(x_prompt, x_sample, cache_k, cache_v, cache_conv, c_prompt, c_sample, rel_table, w_ada, b_ada, norm1_g, norm2_g, w_in, sink, w_attn_out, dw_w, dw_b, conv_ln_g, conv_ln_b, w_conv_out, w_out, w_ffn_up, w_ffn_down, final_g):
    assert w_ada.shape[0] == 1, "single-layer kernel"
    B, S, _ = x_prompt.shape
    DB, DS, _ = x_sample.shape
    nb_attn = ATTN_SAMPLE_TOKENS // DS
    nb_mix = MIX_TOKENS // DS

    bias = _bias_table(rel_table)
    n_seq = B + DB
    pad = (-n_seq) % 8
    c_all = jnp.concatenate([c_prompt, c_sample, jnp.zeros((pad, D_MODEL), f32)], axis=0)
    mod = _modulation(c_all, w_ada[0], b_ada).reshape(n_seq + pad, 6, D_MODEL)
    mod_p, mod_s = mod[:B], mod[B:B + DB]

    row = lambda v: v.reshape(1, D_MODEL)
    w = w_in[0]
    w_k = w[:, O_K:O_V].astype(bf16).reshape(D_MODEL, KV_HEADS, 1, HEAD_DIM)
    w_kd = jnp.broadcast_to(w_k, (D_MODEL, KV_HEADS, 2, HEAD_DIM)).reshape(D_MODEL, KV_HEADS * LANES)
    w_in_b = w.astype(bf16)
    attn_w = (bias, sink[0], row(norm1_g[0]), w_in_b, w_kd, w[:, O_V:O_GLU].T.astype(bf16),
              w_attn_out[0].astype(bf16))
    mix_w = (row(norm1_g[0]), w_in_b, dw_w[0], row(dw_b[0]),
             row(conv_ln_g[0]), row(conv_ln_b[0]), w_conv_out[0].astype(bf16), w_out[0].astype(bf16),
             row(norm2_g[0]), row(final_g), w_ffn_up[0].astype(bf16), w_ffn_down[0].astype(bf16))

    a1p, nkp, nvp = _attn_call(x_prompt, mod_p, (), attn_w, 1, ATTN_TOKENS, True, "attn_prompt")
    ck = cache_k[0].reshape(DB, WINDOW, KV_W)
    cv = cache_v[0].reshape(DB, WINDOW, KV_W)
    a1s, nks, nvs = _attn_call(x_sample, mod_s, (ck, cv), attn_w, nb_attn, DS, False, "attn_sample")

    tiles = lambda a, n, l: a.reshape(-1, n, l, D_MODEL)
    y_p, ncp = _mix_call(tiles(x_prompt, 1, MIX_TOKENS), tiles(a1p, 1, MIX_TOKENS), mod_p[:, None], None,
                         mix_w, 1, MIX_TOKENS, True, "mix_prompt")
    y_s, ncs = _mix_call(tiles(x_sample, nb_mix, DS), tiles(a1s, nb_mix, DS), tiles(mod_s, nb_mix, 6),
                         tiles(cache_conv[0], nb_mix, CONV_W - 1), mix_w, nb_mix, DS, False, "mix_sample")

    kv5 = lambda a: a.reshape(1, a.shape[0], WINDOW, KV_HEADS, HEAD_DIM)
    return (y_p.reshape(B, S, D_MODEL), y_s.reshape(DB, DS, D_MODEL), kv5(nkp), kv5(nvp),
            ncp.reshape(1, B, CONV_W - 1, D_MODEL), kv5(nks), kv5(nvs), ncs.reshape(1, DB, CONV_W - 1, D_MODEL))
```

```python
import functools
import math

import jax
import jax.numpy as jnp
from jax import lax
from jax.experimental import pallas as pl
from jax.experimental.pallas import tpu as pltpu

f32 = jnp.float32
bf16 = jnp.bfloat16

D_MODEL = 1024
N_HEADS = 16
KV_HEADS = 4
HEAD_DIM = 64
GROUP = N_HEADS // KV_HEADS
WINDOW = 128
CHUNK = 64
BAND = WINDOW + CHUNK
CONV_W = 31
D_FF = 2816
NUM_BUCKETS = 32
MAX_DISTANCE = 128
EPS = 1e-6
NEG = -1e30
Q_W = N_HEADS * HEAD_DIM
KV_W = KV_HEADS * HEAD_DIM
O_K = Q_W
O_V = O_K + KV_W
O_GLU = O_V + KV_W
O_GA = O_GLU + 2 * D_MODEL
O_GB = O_GA + D_MODEL
IN_W = O_GB + D_MODEL

LANES = 128
N_SLABS = D_MODEL // LANES
PAIR = 2 * CHUNK
GQ = GROUP * CHUNK
HIST = 32
ROW_BLK = 32
COL_BLK = 256
N_CC = D_MODEL // COL_BLK
N_FF = D_FF // COL_BLK
VMEM_LIMIT = 56 * 1024 * 1024
ATTN_TOKENS = 1024
ATTN_SAMPLE_TOKENS = 512
MIX_TOKENS = 256

_NT = (((1,), (1,)), ((), ()))


def _sigmoid(x):
    return 1.0 / (1.0 + jnp.exp(-x))


def _rms(x, g):
    ms = jnp.mean(x * x, axis=-1, keepdims=True)
    return x * lax.rsqrt(ms + EPS) * g


def _mm(a, w_ref, c0, width):
    return jnp.dot(a, w_ref[:, c0:c0 + width], preferred_element_type=f32)


def _cols(c):
    return slice(c * COL_BLK, (c + 1) * COL_BLK)


def _head_row(vals, k, lane):
    row = jnp.full((1, GQ), vals(GROUP * k + GROUP - 1), f32)
    for g in range(GROUP - 2, -1, -1):
        row = jnp.where(lane < CHUNK * (g + 1), vals(GROUP * k + g), row)
    return row


def _interleave(pieces, blocks, blocks_first=False):
    order = [((n + 0.5) / len(pieces), int(blocks_first), fn) for n, fn in enumerate(pieces)]
    order += [((n + 0.5) / len(blocks), int(not blocks_first), fn) for n, fn in enumerate(blocks)]
    for _, _, fn in sorted(order, key=lambda item: item[:2]):
        fn()


def _const_spec(shape):
    nd = len(shape)
    return pl.BlockSpec(shape, lambda *_: (0,) * nd, pipeline_mode=pl.Buffered(1))


def _rel_bucket(rel):
    nb = NUM_BUCKETS // 2
    max_exact = nb // 2
    ret = (rel > 0).astype(jnp.int32) * nb
    n = jnp.abs(rel)
    nf = jnp.maximum(n, 1).astype(f32)
    large = max_exact + (jnp.log(nf / max_exact) / math.log(MAX_DISTANCE / max_exact)
                         * (nb - max_exact)).astype(jnp.int32)
    large = jnp.minimum(large, nb - 1)
    return ret + jnp.where(n < max_exact, n, large)


def _bias_body(idx_ref, tab_ref, o_ref):
    idx = idx_ref[...]
    lane = lax.broadcasted_iota(jnp.int32, (1, GQ), 1)
    for k in range(KV_HEADS):
        acc = jnp.zeros((BAND, GQ), f32)
        for b in range(NUM_BUCKETS):
            acc = jnp.where(idx == b, _head_row(lambda h: tab_ref[b, h], k, lane), acc)
        o_ref[k] = acc


def _bias_table(rel_table):
    kj = jnp.arange(BAND, dtype=jnp.int32)
    rel = kj[None, :] - WINDOW - jnp.arange(CHUNK, dtype=jnp.int32)[:, None]
    idx_t = jnp.tile(_rel_bucket(rel).T, (1, GROUP))
    return pl.pallas_call(
        _bias_body,
        in_specs=[pl.BlockSpec(memory_space=pltpu.VMEM), pl.BlockSpec(memory_space=pltpu.SMEM)],
        out_specs=pl.BlockSpec(memory_space=pltpu.VMEM),
        out_shape=jax.ShapeDtypeStruct((KV_HEADS, BAND, GQ), f32),
        name="rel_bias",
    )(idx_t, rel_table)


def _mod_body(c_ref, w_ref, b_ref, o_ref):
    c = c_ref[...]
    s = c * _sigmoid(c)
    o_ref[...] = jnp.dot(s.astype(bf16), w_ref[...].astype(bf16), preferred_element_type=f32) + b_ref[...]


def _modulation(c_all, w_ada, b_ada):
    rows = c_all.shape[0]
    blk = 512
    return pl.pallas_call(
        _mod_body,
        grid=(6 * D_MODEL // blk,),
        in_specs=[pl.BlockSpec((rows, D_MODEL), lambda j: (0, 0)),
                  pl.BlockSpec((D_MODEL, blk), lambda j: (0, j)),
                  pl.BlockSpec((1, blk), lambda j: (0, j))],
        out_specs=pl.BlockSpec((rows, blk), lambda j: (0, j)),
        out_shape=jax.ShapeDtypeStruct((rows, 6 * D_MODEL), f32),
        name="adaln_mod",
    )(c_all, w_ada, b_ada)


def _attn_pair(chunks, bias_ref, sink_rows, st_s, pb_s, slot0, attT_ref, grp):
    lane_lo = lax.broadcasted_iota(jnp.int32, (CHUNK, LANES), 1) < HEAD_DIM
    n_pad = 2 * LANES - BAND
    for par, (q_c, kd_bands, v_win, pad_top, valid) in enumerate(chunks):
        for k in range(KV_HEADS):
            blocks = []
            for p in range(2):
                qc = q_c[:, (2 * k + p) * LANES:(2 * k + p + 1) * LANES]
                blocks.append(jnp.where(lane_lo, qc, jnp.zeros_like(qc)))
                blocks.append(jnp.where(lane_lo, jnp.zeros_like(qc), qc))
            qm = jnp.concatenate(blocks, axis=0)
            st_s[slot0 + par * KV_HEADS + k] = lax.dot_general(kd_bands[k], qm, _NT, preferred_element_type=f32)
    for par, (q_c, kd_bands, v_win, pad_top, valid) in enumerate(chunks):
        for k in range(KV_HEADS):
            i = slot0 + par * KV_HEADS + k
            st = st_s[i] + bias_ref[k]
            if valid is not None:
                st = jnp.where(valid, st, NEG)
            m = jnp.maximum(jnp.max(st, axis=0, keepdims=True), sink_rows[k])
            p_ = jnp.exp(st - m)
            den = jnp.sum(p_, axis=0, keepdims=True) + jnp.exp(sink_rows[k] - m)
            r0 = n_pad if pad_top else 0
            pb_s[i, r0:r0 + BAND, :] = (p_ / den).astype(bf16)
            z0 = 0 if pad_top else BAND
            pb_s[i, z0:z0 + n_pad, :] = jnp.zeros((n_pad, GQ), bf16)
    for par, (q_c, kd_bands, v_win, pad_top, valid) in enumerate(chunks):
        for k in range(KV_HEADS):
            i = slot0 + par * KV_HEADS + k
            ot = jnp.dot(v_win[k * HEAD_DIM:(k + 1) * HEAD_DIM, :], pb_s[i], preferred_element_type=f32)
            for g in range(GROUP):
                h = GROUP * k + g
                attT_ref[grp, h * HEAD_DIM:(h + 1) * HEAD_DIM, par * CHUNK:(par + 1) * CHUNK] = (
                    ot[:, g * CHUNK:(g + 1) * CHUNK])


def _attn_body(*refs, is_prompt, nb, L, n_tiles):
    if is_prompt:
        (x_ref, mod_ref, bias_ref, sink_ref, g1_ref, win_ref, wkd_ref, wvt_ref, wao_ref,
         a1_ref, nk_ref, nv_ref,
         h_s, q_s, kd_s, vt_s, st_s, pb_s, attT_s, att_s, ga_s) = refs
        ck_ref = cv_ref = None
    else:
        (x_ref, mod_ref, ck_ref, cv_ref, bias_ref, sink_ref, g1_ref, win_ref, wkd_ref, wvt_ref, wao_ref,
         a1_ref, nk_ref, nv_ref,
         h_s, q_s, kd_s, vt_s, st_s, pb_s, attT_s, att_s, ga_s) = refs
    T = nb * L
    n_grp = T // PAIR

    if is_prompt:
        t = pl.program_id(1)

        @pl.when(t == 0)
        def _():
            kd_s[:, 0:WINDOW, :] = jnp.zeros((KV_HEADS, WINDOW, LANES), bf16)
            vt_s[0] = jnp.zeros((KV_W, LANES), bf16)

    for s in range(nb):
        for i in range(L // ROW_BLK):
            rows = slice(i * ROW_BLK, (i + 1) * ROW_BLK)
            hh = _rms(x_ref[s, rows, :], g1_ref[...]) * (1.0 + mod_ref[s, 1:2, :]) + mod_ref[s, 0:1, :]
            h_s[s * L + i * ROW_BLK:s * L + (i + 1) * ROW_BLK, :] = hh.astype(bf16)
    hv = h_s[...]

    if not is_prompt:
        for s in range(nb):
            ck = ck_ref[s]
            for k in range(KV_HEADS):
                ckk = ck[:, k * HEAD_DIM:(k + 1) * HEAD_DIM].astype(bf16)
                kd_s[s, k, 0:WINDOW, :] = jnp.concatenate([ckk, ckk], axis=1)
            vt_s[s, :, 0:WINDOW] = cv_ref[s].T.astype(bf16)
            vt_s[s, :, BAND:2 * LANES] = jnp.zeros((KV_W, 2 * LANES - BAND), bf16)
            nk_ref[s, 0:WINDOW - L, :] = ck[L:WINDOW, :]
            nv_ref[s, 0:WINDOW - L, :] = cv_ref[s, L:WINDOW, :]
        kvf = _mm(hv, win_ref, O_K, 2 * KV_W)
        for s in range(nb):
            nk_ref[s, WINDOW - L:WINDOW, :] = kvf[s * L:(s + 1) * L, 0:KV_W]
            nv_ref[s, WINDOW - L:WINDOW, :] = kvf[s * L:(s + 1) * L, KV_W:2 * KV_W]

    for c in range(Q_W // COL_BLK):
        qf = _mm(hv, win_ref, c * COL_BLK, COL_BLK)
        q_s[:, _cols(c)] = (qf * (HEAD_DIM ** -0.5)).astype(bf16)
    for c in range(KV_HEADS * LANES // COL_BLK):
        kdc = _mm(hv, wkd_ref, c * COL_BLK, COL_BLK)
        for kk in range(COL_BLK // LANES):
            k = c * (COL_BLK // LANES) + kk
            kdk = kdc[:, kk * LANES:(kk + 1) * LANES].astype(bf16)
            if is_prompt:
                kd_s[k, WINDOW:WINDOW + T, :] = kdk
            else:
                for s in range(nb):
                    kd_s[s, k, WINDOW:BAND, :] = kdk[s * L:(s + 1) * L]
    vtf = lax.dot_general(wvt_ref[...], hv, _NT, preferred_element_type=f32).astype(bf16)
    if is_prompt:
        for g in range(n_grp):
            vt_s[1 + g] = vtf[:, g * LANES:(g + 1) * LANES]
    else:
        for s in range(nb):
            vt_s[s, :, WINDOW:BAND] = vtf[:, s * L:(s + 1) * L]

    lane_gq = lax.broadcasted_iota(jnp.int32, (1, GQ), 1)
    sink_rows = [_head_row(lambda h: sink_ref[h], k, lane_gq) for k in range(KV_HEADS)]

    def attn_block(c2):
        chunks = []
        if is_prompt:
            v_win = jnp.concatenate([vt_s[c2], vt_s[c2 + 1]], axis=1)
            for par in range(2):
                r = c2 * PAIR + par * CHUNK
                pos = t * T + r - WINDOW + lax.broadcasted_iota(jnp.int32, (BAND, 1), 0)
                chunks.append((q_s[r:r + CHUNK, :],
                               [kd_s[k, r:r + BAND, :] for k in range(KV_HEADS)],
                               v_win, par == 1, pos >= 0))
        else:
            for par in range(2):
                s = 2 * c2 + par
                chunks.append((q_s[s * CHUNK:(s + 1) * CHUNK, :],
                               [kd_s[s, k] for k in range(KV_HEADS)],
                               vt_s[s], False, None))
        _attn_pair(chunks, bias_ref, sink_rows, st_s, pb_s, (c2 % 2) * 2 * KV_HEADS, attT_s, c2)
        att_s[c2 * PAIR:(c2 + 1) * PAIR, :] = attT_s[c2].T.astype(bf16)

    def ga_piece(c):
        ga_s[:, _cols(c)] = _sigmoid(_mm(hv, win_ref, O_GA + c * COL_BLK, COL_BLK))

    _interleave([functools.partial(ga_piece, c) for c in range(N_CC)],
                [functools.partial(attn_block, c2) for c2 in range(n_grp)], blocks_first=True)

    av = att_s[...]
    for c in range(N_CC):
        a1c = ga_s[:, _cols(c)] * _mm(av, wao_ref, c * COL_BLK, COL_BLK)
        for s in range(nb):
            a1_ref[s, :, _cols(c)] = a1c[s * L:(s + 1) * L]

    if is_prompt:
        @pl.when(t == n_tiles - 1)
        def _():
            kvf = _mm(h_s[T - WINDOW:T, :], win_ref, O_K, 2 * KV_W)
            nk_ref[0] = kvf[:, 0:KV_W]
            nv_ref[0] = kvf[:, KV_W:2 * KV_W]

        kd_s[:, 0:WINDOW, :] = kd_s[:, T:T + WINDOW, :]
        vt_s[0] = vt_s[n_grp]


def _attn_call(x, mod, caches, weights, nb, L, is_prompt, name):
    B, S, _ = x.shape
    T = nb * L
    if is_prompt:
        n_tiles = S // L
        grid = (B, n_tiles)
        xmap = lambda b, t: (b, t, 0)
        smap = lambda b, t: (b, 0, 0)
        kd_shape = (KV_HEADS, WINDOW + T, LANES)
        vt_shape = (1 + T // PAIR, KV_W, LANES)
    else:
        n_tiles = B // nb
        grid = (n_tiles,)
        xmap = smap = lambda i: (i, 0, 0)
        kd_shape = (nb, KV_HEADS, BAND, LANES)
        vt_shape = (nb, KV_W, 2 * LANES)
    body = functools.partial(_attn_body, is_prompt=is_prompt, nb=nb, L=L, n_tiles=n_tiles)
    cache_specs = [pl.BlockSpec((nb, WINDOW, KV_W), smap)] * len(caches)
    weight_specs = [
        _const_spec((KV_HEADS, BAND, GQ)),
        pl.BlockSpec(memory_space=pltpu.SMEM),
        _const_spec((1, D_MODEL)),
        _const_spec((D_MODEL, IN_W)),
        _const_spec((D_MODEL, KV_HEADS * LANES)),
        _const_spec((KV_W, D_MODEL)),
        _const_spec((Q_W, D_MODEL)),
    ]
    return pl.pallas_call(
        body,
        grid=grid,
        in_specs=[pl.BlockSpec((nb, L, D_MODEL), xmap), pl.BlockSpec((nb, 6, D_MODEL), smap)]
        + cache_specs + weight_specs,
        out_specs=[pl.BlockSpec((nb, L, D_MODEL), xmap),
                   pl.BlockSpec((nb, WINDOW, KV_W), smap),
                   pl.BlockSpec((nb, WINDOW, KV_W), smap)],
        out_shape=[jax.ShapeDtypeStruct((B, S, D_MODEL), f32),
                   jax.ShapeDtypeStruct((B, WINDOW, KV_W), f32),
                   jax.ShapeDtypeStruct((B, WINDOW, KV_W), f32)],
        scratch_shapes=[
            pltpu.VMEM((T, D_MODEL), bf16),
            pltpu.VMEM((T, Q_W), bf16),
            pltpu.VMEM(kd_shape, bf16),
            pltpu.VMEM(vt_shape, bf16),
            pltpu.VMEM((4 * KV_HEADS, BAND, GQ), f32),
            pltpu.VMEM((4 * KV_HEADS, 2 * LANES, GQ), bf16),
            pltpu.VMEM((T // PAIR, Q_W, PAIR), f32),
            pltpu.VMEM((T, Q_W), bf16),
            pltpu.VMEM((T, D_MODEL), f32),
        ],
        compiler_params=pltpu.CompilerParams(
            dimension_semantics=("arbitrary",) * len(grid), vmem_limit_bytes=VMEM_LIMIT),
        name=name,
    )(x, mod, *caches, *weights)


def _conv_block(zb_ref, s, j, i, dww_ref, cb_ref, out_row0):
    r0 = i * 32
    accs = [jnp.zeros((8, LANES), f32) for _ in range(4)]
    for k in range(CONV_W):
        wk = dww_ref[k:k + 1, j * LANES:(j + 1) * LANES]
        for a in range(4):
            start = r0 + (a // 2) * 16 + (a % 2) + (HIST - (CONV_W - 1)) + k
            accs[a] = accs[a] + zb_ref[s, j, pl.ds(start, 8, stride=2), :] * wk
    for a in range(4):
        start = out_row0 + r0 + (a // 2) * 16 + (a % 2)
        cb_ref[j, pl.ds(start, 8, stride=2), :] = accs[a]


def _mix_body(*refs, is_prompt, nb, L, n_tiles, tiles_per_seq):
    if is_prompt:
        (xa_ref, xb_ref, a1_ref, moda_ref, modb_ref, modc_ref, g1_ref, win_ref, dww_ref, dwb_ref,
         lng_ref, lnb_ref, wco_ref, wo_ref, g2_ref, fg_ref, wup_ref, wdn_ref,
         y_ref, nc_ref,
         h_s, zb_s, cb_s, cbf_s, gb_s, acc_s, x1_s, h2_s, act_s, d_s) = refs
        cc_ref = None
    else:
        (xa_ref, xb_ref, a1_ref, moda_ref, modb_ref, modc_ref, cc_ref, g1_ref, win_ref, dww_ref, dwb_ref,
         lng_ref, lnb_ref, wco_ref, wo_ref, g2_ref, fg_ref, wup_ref, wdn_ref,
         y_ref, nc_ref,
         h_s, zb_s, cb_s, cbf_s, gb_s, acc_s, x1_s, h2_s, act_s, d_s) = refs
    T = nb * L
    step = pl.program_id(0)
    t_in_seq = jnp.minimum(step, n_tiles - 1) % tiles_per_seq

    if is_prompt:
        @pl.when(t_in_seq == 0)
        def _():
            zb_s[0, :, 0:HIST, :] = jnp.zeros((N_SLABS, HIST, LANES), f32)

    def norm2_block(i):
        s = i * ROW_BLK // L
        rows = slice(i * ROW_BLK, (i + 1) * ROW_BLK)
        hh = _rms(x1_s[rows, :], g2_ref[...]) * (1.0 + modc_ref[0, s, 4:5, :]) + modc_ref[0, s, 3:4, :]
        h2_s[rows, :] = hh.astype(bf16)

    def norm1_block(i):
        s, r = divmod(i * ROW_BLK, L)
        rows = slice(i * ROW_BLK, (i + 1) * ROW_BLK)
        hh = _rms(xa_ref[0, s, r:r + ROW_BLK, :], g1_ref[...]) * (1.0 + moda_ref[0, s, 1:2, :]) + moda_ref[0, s, 0:1, :]
        h_s[rows, :] = hh.astype(bf16)

    def glu_piece(c):
        hv = h_s[...]
        za = _mm(hv, win_ref, O_GLU + c * COL_BLK, COL_BLK)
        zg = _mm(hv, win_ref, O_GLU + D_MODEL + c * COL_BLK, COL_BLK)
        zc = za * _sigmoid(zg)
        for jj in range(COL_BLK // LANES):
            j = c * (COL_BLK // LANES) + jj
            z = zc[:, jj * LANES:(jj + 1) * LANES]
            for s in range(nb):
                zb_s[s, j, HIST:HIST + L, :] = z[s * L:(s + 1) * L]
                if not is_prompt:
                    nc_ref[0, s, :, j * LANES:(j + 1) * LANES] = z[s * L + L - (CONV_W - 1):(s + 1) * L]

    def gb_piece(c):
        gb_s[:, _cols(c)] = _sigmoid(_mm(h_s[...], win_ref, O_GB + c * COL_BLK, COL_BLK))

    def up_piece(c):
        h2v = h2_s[...]
        gate = _mm(h2v, wup_ref, c * COL_BLK, COL_BLK)
        up = _mm(h2v, wup_ref, D_FF + c * COL_BLK, COL_BLK)
        act_s[c % 2] = (gate * _sigmoid(gate) * up).astype(bf16)

    def down_piece(c):
        d_s[...] += jnp.dot(act_s[c % 2], wdn_ref[c * COL_BLK:(c + 1) * COL_BLK, :], preferred_element_type=f32)

    def ln_block(i):
        rows = slice(i * ROW_BLK, (i + 1) * ROW_BLK)
        ys = [cb_s[j, rows, :] + dwb_ref[:, j * LANES:(j + 1) * LANES] for j in range(N_SLABS)]
        tot = ys[0]
        for j in range(1, N_SLABS):
            tot = tot + ys[j]
        mu = jnp.sum(tot, axis=-1, keepdims=True) * (1.0 / D_MODEL)
        ds = [y - mu for y in ys]
        sq = ds[0] * ds[0]
        for j in range(1, N_SLABS):
            sq = sq + ds[j] * ds[j]
        var = jnp.sum(sq, axis=-1, keepdims=True) * (1.0 / D_MODEL)
        rstd = lax.rsqrt(var + EPS)
        for j in range(N_SLABS):
            y = ds[j] * rstd * lng_ref[:, j * LANES:(j + 1) * LANES] + lnb_ref[:, j * LANES:(j + 1) * LANES]
            cbf_s[rows, j * LANES:(j + 1) * LANES] = (y * _sigmoid(y)).astype(bf16)

    def yb_piece(c):
        yb = _mm(cbf_s[...], wco_ref, c * COL_BLK, COL_BLK)
        for s in range(nb):
            rows = slice(s * L, (s + 1) * L)
            acc_s[rows, _cols(c)] = a1_ref[0, s, :, _cols(c)] + gb_s[rows, _cols(c)] * yb[rows]

    def final_block(i):
        s, r = divmod(i * ROW_BLK, L)
        rows = slice(i * ROW_BLK, (i + 1) * ROW_BLK)
        x2 = x1_s[rows, :] + modc_ref[0, s, 5:6, :] * d_s[rows, :]
        y_ref[0, s, r:r + ROW_BLK, :] = _rms(x2, fg_ref[...])

    def out_piece(c):
        m = _mm(acc_s[...].astype(bf16), wo_ref, c * COL_BLK, COL_BLK)
        for s in range(nb):
            rows = slice(s * L, (s + 1) * L)
            x1_s[rows, _cols(c)] = xb_ref[0, s, :, _cols(c)] + modb_ref[0, s, 2:3, _cols(c)] * m[rows]

    n_blk = T // ROW_BLK

    def emit(run_a, run_b, run_c):
        if run_b:
            _interleave([functools.partial(gb_piece, c) for c in range(N_CC)],
                        [functools.partial(ln_block, i) for i in range(n_blk)])
        if run_a:
            for i in range(n_blk):
                norm1_block(i)
            if not is_prompt:
                for s in range(nb):
                    for j in range(N_SLABS):
                        zb_s[s, j, HIST - (CONV_W - 1):HIST, :] = cc_ref[0, s, :, j * LANES:(j + 1) * LANES]
        _interleave([functools.partial(glu_piece, c) for c in range(N_CC)] if run_a else [],
                    [functools.partial(norm2_block, i) for i in range(n_blk)] if run_c else [])
        pieces = []
        if run_c:
            d_s[...] = jnp.zeros((T, D_MODEL), f32)
            up_piece(0)
            for c in range(N_FF):
                if c + 1 < N_FF:
                    pieces.append(functools.partial(up_piece, c + 1))
                pieces.append(functools.partial(down_piece, c))
        if run_b:
            pieces += [functools.partial(yb_piece, c) for c in range(N_CC)]
        conv_blocks = [functools.partial(_conv_block, zb_s, s, j, i, dww_ref, cb_s, s * L)
                       for j in range(N_SLABS) for s in range(nb) for i in range(L // 32)] if run_a else []
        _interleave(pieces, conv_blocks)
        if run_c:
            for i in range(n_blk):
                final_block(i)
        if run_b:
            for c in range(N_CC):
                out_piece(c)

    pl.when(step == 0)(functools.partial(emit, True, False, False))
    pl.when(step == 1)(functools.partial(emit, True, True, False))
    pl.when(jnp.logical_and(step >= 2, step < n_tiles))(functools.partial(emit, True, True, True))
    pl.when(step == n_tiles)(functools.partial(emit, False, True, True))
    pl.when(step == n_tiles + 1)(functools.partial(emit, False, False, True))

    if is_prompt:
        @pl.when(jnp.logical_and(t_in_seq == tiles_per_seq - 1, step < n_tiles))
        def _():
            for j in range(N_SLABS):
                nc_ref[0, 0, :, j * LANES:(j + 1) * LANES] = zb_s[0, j, HIST + L - (CONV_W - 1):HIST + L, :]

        zb_s[0, :, 0:HIST, :] = zb_s[0, :, L:L + HIST, :]


def _mix_call(x, a1, mod, cc, weights, nb, L, is_prompt, name):
    n_tiles = x.shape[0]
    tiles_per_seq = n_tiles // mod.shape[0]
    T = nb * L

    def lagged(lag, per_group):
        def index_map(i):
            t = jnp.clip(i - lag, 0, n_tiles - 1)
            return (t // per_group, 0, 0, 0)
        return index_map

    body = functools.partial(_mix_body, is_prompt=is_prompt, nb=nb, L=L, n_tiles=n_tiles,
                             tiles_per_seq=tiles_per_seq)
    row = _const_spec((1, D_MODEL))
    tile_blk = (1, nb, L, D_MODEL)
    mod_blk = (1, nb, 6, D_MODEL)
    in_specs = [pl.BlockSpec(tile_blk, lagged(0, 1)),
                pl.BlockSpec(tile_blk, lagged(1, 1)),
                pl.BlockSpec(tile_blk, lagged(1, 1)),
                pl.BlockSpec(mod_blk, lagged(0, tiles_per_seq)),
                pl.BlockSpec(mod_blk, lagged(1, tiles_per_seq)),
                pl.BlockSpec(mod_blk, lagged(2, tiles_per_seq))]
    args = [x, x, a1, mod, mod, mod]
    if not is_prompt:
        in_specs.append(pl.BlockSpec((1, nb, CONV_W - 1, D_MODEL), lagged(0, 1)))
        args.append(cc)
    in_specs += [
        row,
        _const_spec((D_MODEL, IN_W)),
        _const_spec((CONV_W, D_MODEL)),
        row, row, row,
        _const_spec((D_MODEL, D_MODEL)),
        _const_spec((D_MODEL, D_MODEL)),
        row, row,
        _const_spec((D_MODEL, 2 * D_FF)),
        _const_spec((D_FF, D_MODEL)),
    ]
    n_groups = mod.shape[0]
    return pl.pallas_call(
        body,
        grid=(n_tiles + 2,),
        in_specs=in_specs,
        out_specs=[pl.BlockSpec(tile_blk, lagged(2, 1)),
                   pl.BlockSpec((1, nb, CONV_W - 1, D_MODEL), lagged(0, tiles_per_seq))],
        out_shape=[jax.ShapeDtypeStruct(x.shape, f32),
                   jax.ShapeDtypeStruct((n_groups, nb, CONV_W - 1, D_MODEL), f32)],
        scratch_shapes=[
            pltpu.VMEM((T, D_MODEL), bf16),
            pltpu.VMEM((nb, N_SLABS, HIST + L, LANES), f32),
            pltpu.VMEM((N_SLABS, T, LANES), f32),
            pltpu.VMEM((T, D_MODEL), bf16),
            pltpu.VMEM((T, D_MODEL), f32),
            pltpu.VMEM((T, D_MODEL), f32),
            pltpu.VMEM((T, D_MODEL), f32),
            pltpu.VMEM((T, D_MODEL), bf16),
            pltpu.VMEM((2, T, COL_BLK), bf16),
            pltpu.VMEM((T, D_MODEL), f32),
        ],
        compiler_params=pltpu.CompilerParams(
            dimension_semantics=("arbitrary",), vmem_limit_bytes=VMEM_LIMIT),
        name=name,
    )(*args, *weights)


def kernel(x_prompt, x_sample, cache_k, cache_v, cache_conv, c_prompt, c_sample, rel_table, w_ada, b_ada, norm1_g, norm2_g, w_in, sink, w_attn_out, dw_w, dw_b, conv_ln_g, conv_ln_b, w_conv_out, w_out, w_ffn_up, w_ffn_down, final_g):
    assert w_ada.shape[0] == 1, "single-layer kernel"
    B, S, _ = x_prompt.shape
    DB, DS, _ = x_sample.shape
    nb_attn = ATTN_SAMPLE_TOKENS // DS
    nb_mix = MIX_TOKENS // DS

    bias = _bias_table(rel_table)
    n_seq = B + DB
    pad = (-n_seq) % 8
    c_all = jnp.concatenate([c_prompt, c_sample, jnp.zeros((pad, D_MODEL), f32)], axis=0)
    mod = _modulation(c_all, w_ada[0], b_ada).reshape(n_seq + pad, 6, D_MODEL)
    mod_p, mod_s = mod[:B], mod[B:B + DB]

    row = lambda v: v.reshape(1, D_MODEL)
    w = w_in[0]
    w_k = w[:, O_K:O_V].astype(bf16).reshape(D_MODEL, KV_HEADS, 1, HEAD_DIM)
    w_kd = jnp.broadcast_to(w_k, (D_MODEL, KV_HEADS, 2, HEAD_DIM)).reshape(D_MODEL, KV_HEADS * LANES)
    w_in_b = w.astype(bf16)
    attn_w = (bias, sink[0], row(norm1_g[0]), w_in_b, w_kd, w[:, O_V:O_GLU].T.astype(bf16),
              w_attn_out[0].astype(bf16))
    mix_w = (row(norm1_g[0]), w_in_b, dw_w[0], row(dw_b[0]),
             row(conv_ln_g[0]), row(conv_ln_b[0]), w_conv_out[0].astype(bf16), w_out[0].astype(bf16),
             row(norm2_g[0]), row(final_g), w_ffn_up[0].astype(bf16), w_ffn_down[0].astype(bf16))

    a1p, nkp, nvp = _attn_call(x_prompt, mod_p, (), attn_w, 1, ATTN_TOKENS, True, "attn_prompt")
    ck = cache_k[0].reshape(DB, WINDOW, KV_W)
    cv = cache_v[0].reshape(DB, WINDOW, KV_W)
    a1s, nks, nvs = _attn_call(x_sample, mod_s, (ck, cv), attn_w, nb_attn, DS, False, "attn_sample")

    tiles = lambda a, n, l: a.reshape(-1, n, l, D_MODEL)
    y_p, ncp = _mix_call(tiles(x_prompt, 1, MIX_TOKENS), tiles(a1p, 1, MIX_TOKENS), mod_p[:, None], None,
                         mix_w, 1, MIX_TOKENS, True, "mix_prompt")
    y_s, ncs = _mix_call(tiles(x_sample, nb_mix, DS), tiles(a1s, nb_mix, DS), tiles(mod_s, nb_mix, 6),
                         tiles(cache_conv[0], nb_mix, CONV_W - 1), mix_w, nb_mix, DS, False, "mix_sample")

    kv5 = lambda a: a.reshape(1, a.shape[0], WINDOW, KV_HEADS, HEAD_DIM)
    return (y_p.reshape(B, S, D_MODEL), y_s.reshape(DB, DS, D_MODEL), kv5(nkp), kv5(nvp),
            ncp.reshape(1, B, CONV_W - 1, D_MODEL), kv5(nks), kv5(nvs), ncs.reshape(1, DB, CONV_W - 1, D_MODEL))
```

```python
import functools
import math

import jax
import jax.numpy as jnp
from jax import lax
from jax.experimental import pallas as pl
from jax.experimental.pallas import tpu as pltpu

f32 = jnp.float32
bf16 = jnp.bfloat16

D_MODEL = 1024
N_HEADS = 16
KV_HEADS = 4
HEAD_DIM = 64
GROUP = N_HEADS // KV_HEADS
WINDOW = 128
CHUNK = 64
BAND = WINDOW + CHUNK
CONV_W = 31
D_FF = 2816
NUM_BUCKETS = 32
MAX_DISTANCE = 128
EPS = 1e-6
NEG = -1e30
Q_W = N_HEADS * HEAD_DIM
KV_W = KV_HEADS * HEAD_DIM
O_K = Q_W
O_V = O_K + KV_W
O_GLU = O_V + KV_W
O_GA = O_GLU + 2 * D_MODEL
O_GB = O_GA + D_MODEL
IN_W = O_GB + D_MODEL

LANES = 128
N_SLABS = D_MODEL // LANES
PAIR = 2 * CHUNK
GQ = GROUP * CHUNK
HIST = 32
ROW_BLK = 32
COL_BLK = 256
N_CC = D_MODEL // COL_BLK
N_FF = D_FF // COL_BLK
VMEM_LIMIT = 56 * 1024 * 1024
ATTN_TOKENS = 1024
ATTN_SAMPLE_TOKENS = 512
MIX_TOKENS = 256

_NT = (((1,), (1,)), ((), ()))


def _sigmoid(x):
    return 1.0 / (1.0 + jnp.exp(-x))


def _rms(x, g):
    ms = jnp.mean(x * x, axis=-1, keepdims=True)
    return x * lax.rsqrt(ms + EPS) * g


def _mm(a, w_ref, c0, width):
    return jnp.dot(a, w_ref[:, c0:c0 + width], preferred_element_type=f32)


def _cols(c):
    return slice(c * COL_BLK, (c + 1) * COL_BLK)


def _head_row(vals, k, lane):
    row = jnp.full((1, GQ), vals(GROUP * k + GROUP - 1), f32)
    for g in range(GROUP - 2, -1, -1):
        row = jnp.where(lane < CHUNK * (g + 1), vals(GROUP * k + g), row)
    return row


def _interleave(pieces, blocks, blocks_first=False):
    order = [((n + 0.5) / len(pieces), int(blocks_first), fn) for n, fn in enumerate(pieces)]
    order += [((n + 0.5) / len(blocks), int(not blocks_first), fn) for n, fn in enumerate(blocks)]
    for _, _, fn in sorted(order, key=lambda item: item[:2]):
        fn()


def _const_spec(shape):
    nd = len(shape)
    return pl.BlockSpec(shape, lambda *_: (0,) * nd, pipeline_mode=pl.Buffered(1))


def _rel_bucket(rel):
    nb = NUM_BUCKETS // 2
    max_exact = nb // 2
    ret = (rel > 0).astype(jnp.int32) * nb
    n = jnp.abs(rel)
    nf = jnp.maximum(n, 1).astype(f32)
    large = max_exact + (jnp.log(nf / max_exact) / math.log(MAX_DISTANCE / max_exact)
                         * (nb - max_exact)).astype(jnp.int32)
    large = jnp.minimum(large, nb - 1)
    return ret + jnp.where(n < max_exact, n, large)


def _bias_body(idx_ref, tab_ref, o_ref):
    idx = idx_ref[...]
    lane = lax.broadcasted_iota(jnp.int32, (1, GQ), 1)
    for k in range(KV_HEADS):
        acc = jnp.zeros((BAND, GQ), f32)
        for b in range(NUM_BUCKETS):
            acc = jnp.where(idx == b, _head_row(lambda h: tab_ref[b, h], k, lane), acc)
        o_ref[k] = acc


def _bias_table(rel_table):
    kj = jnp.arange(BAND, dtype=jnp.int32)
    rel = kj[None, :] - WINDOW - jnp.arange(CHUNK, dtype=jnp.int32)[:, None]
    idx_t = jnp.tile(_rel_bucket(rel).T, (1, GROUP))
    return pl.pallas_call(
        _bias_body,
        in_specs=[pl.BlockSpec(memory_space=pltpu.VMEM), pl.BlockSpec(memory_space=pltpu.SMEM)],
        out_specs=pl.BlockSpec(memory_space=pltpu.VMEM),
        out_shape=jax.ShapeDtypeStruct((KV_HEADS, BAND, GQ), f32),
        name="rel_bias",
    )(idx_t, rel_table)


def _mod_body(c_ref, w_ref, b_ref, o_ref):
    c = c_ref[...]
    s = c * _sigmoid(c)
    o_ref[...] = jnp.dot(s.astype(bf16), w_ref[...].astype(bf16), preferred_element_type=f32) + b_ref[...]


def _modulation(c_all, w_ada, b_ada):
    rows = c_all.shape[0]
    blk = 2048
    return pl.pallas_call(
        _mod_body,
        grid=(6 * D_MODEL // blk,),
        in_specs=[pl.BlockSpec((rows, D_MODEL), lambda j: (0, 0)),
                  pl.BlockSpec((D_MODEL, blk), lambda j: (0, j)),
                  pl.BlockSpec((1, blk), lambda j: (0, j))],
        out_specs=pl.BlockSpec((rows, blk), lambda j: (0, j)),
        out_shape=jax.ShapeDtypeStruct((rows, 6 * D_MODEL), f32),
        name="adaln_mod",
    )(c_all, w_ada, b_ada)


def _attn_pair(chunks, bias_ref, sink_rows, st_s, pb_s, slot0, attT_ref, grp):
    lane_lo = lax.broadcasted_iota(jnp.int32, (CHUNK, LANES), 1) < HEAD_DIM
    n_pad = 2 * LANES - BAND
    for par, (q_c, kd_bands, v_win, pad_top, valid) in enumerate(chunks):
        for k in range(KV_HEADS):
            blocks = []
            for p in range(2):
                qc = q_c[:, (2 * k + p) * LANES:(2 * k + p + 1) * LANES]
                blocks.append(jnp.where(lane_lo, qc, jnp.zeros_like(qc)))
                blocks.append(jnp.where(lane_lo, jnp.zeros_like(qc), qc))
            qm = jnp.concatenate(blocks, axis=0)
            st_s[slot0 + par * KV_HEADS + k] = lax.dot_general(kd_bands[k], qm, _NT, preferred_element_type=f32)
    for par, (q_c, kd_bands, v_win, pad_top, valid) in enumerate(chunks):
        for k in range(KV_HEADS):
            i = slot0 + par * KV_HEADS + k
            st = st_s[i] + bias_ref[k]
            if valid is not None:
                st = jnp.where(valid, st, NEG)
            m = jnp.maximum(jnp.max(st, axis=0, keepdims=True), sink_rows[k])
            p_ = jnp.exp(st - m)
            den = jnp.sum(p_, axis=0, keepdims=True) + jnp.exp(sink_rows[k] - m)
            r0 = n_pad if pad_top else 0
            pb_s[i, r0:r0 + BAND, :] = (p_ / den).astype(bf16)
            z0 = 0 if pad_top else BAND
            pb_s[i, z0:z0 + n_pad, :] = jnp.zeros((n_pad, GQ), bf16)
    for par, (q_c, kd_bands, v_win, pad_top, valid) in enumerate(chunks):
        for k in range(KV_HEADS):
            i = slot0 + par * KV_HEADS + k
            ot = jnp.dot(v_win[k * HEAD_DIM:(k + 1) * HEAD_DIM, :], pb_s[i], preferred_element_type=f32)
            for g in range(GROUP):
                h = GROUP * k + g
                attT_ref[grp, h * HEAD_DIM:(h + 1) * HEAD_DIM, par * CHUNK:(par + 1) * CHUNK] = (
                    ot[:, g * CHUNK:(g + 1) * CHUNK])


def _attn_body(*refs, is_prompt, nb, L, n_tiles):
    if is_prompt:
        (x_ref, mod_ref, bias_ref, sink_ref, g1_ref, win_ref, wkd_ref, wvt_ref, wao_ref,
         a1_ref, nk_ref, nv_ref,
         h_s, q_s, kd_s, vt_s, st_s, pb_s, attT_s, att_s, ga_s) = refs
        ck_ref = cv_ref = None
    else:
        (x_ref, mod_ref, ck_ref, cv_ref, bias_ref, sink_ref, g1_ref, win_ref, wkd_ref, wvt_ref, wao_ref,
         a1_ref, nk_ref, nv_ref,
         h_s, q_s, kd_s, vt_s, st_s, pb_s, attT_s, att_s, ga_s) = refs
    T = nb * L
    n_grp = T // PAIR

    if is_prompt:
        t = pl.program_id(1)

        @pl.when(t == 0)
        def _():
            kd_s[:, 0:WINDOW, :] = jnp.zeros((KV_HEADS, WINDOW, LANES), bf16)
            vt_s[0] = jnp.zeros((KV_W, LANES), bf16)

    for s in range(nb):
        for i in range(L // ROW_BLK):
            rows = slice(i * ROW_BLK, (i + 1) * ROW_BLK)
            hh = _rms(x_ref[s, rows, :], g1_ref[...]) * (1.0 + mod_ref[s, 1:2, :]) + mod_ref[s, 0:1, :]
            h_s[s * L + i * ROW_BLK:s * L + (i + 1) * ROW_BLK, :] = hh.astype(bf16)
    hv = h_s[...]

    if not is_prompt:
        for s in range(nb):
            ck = ck_ref[s]
            for k in range(KV_HEADS):
                ckk = ck[:, k * HEAD_DIM:(k + 1) * HEAD_DIM].astype(bf16)
                kd_s[s, k, 0:WINDOW, :] = jnp.concatenate([ckk, ckk], axis=1)
            vt_s[s, :, 0:WINDOW] = cv_ref[s].T.astype(bf16)
            vt_s[s, :, BAND:2 * LANES] = jnp.zeros((KV_W, 2 * LANES - BAND), bf16)
            nk_ref[s, 0:WINDOW - L, :] = ck[L:WINDOW, :]
            nv_ref[s, 0:WINDOW - L, :] = cv_ref[s, L:WINDOW, :]
        kvf = _mm(hv, win_ref, O_K, 2 * KV_W)
        for s in range(nb):
            nk_ref[s, WINDOW - L:WINDOW, :] = kvf[s * L:(s + 1) * L, 0:KV_W]
            nv_ref[s, WINDOW - L:WINDOW, :] = kvf[s * L:(s + 1) * L, KV_W:2 * KV_W]

    for c in range(Q_W // COL_BLK):
        qf = _mm(hv, win_ref, c * COL_BLK, COL_BLK)
        q_s[:, _cols(c)] = (qf * (HEAD_DIM ** -0.5)).astype(bf16)
    for c in range(KV_HEADS * LANES // COL_BLK):
        kdc = _mm(hv, wkd_ref, c * COL_BLK, COL_BLK)
        for kk in range(COL_BLK // LANES):
            k = c * (COL_BLK // LANES) + kk
            kdk = kdc[:, kk * LANES:(kk + 1) * LANES].astype(bf16)
            if is_prompt:
                kd_s[k, WINDOW:WINDOW + T, :] = kdk
            else:
                for s in range(nb):
                    kd_s[s, k, WINDOW:BAND, :] = kdk[s * L:(s + 1) * L]
    vtf = lax.dot_general(wvt_ref[...], hv, _NT, preferred_element_type=f32).astype(bf16)
    if is_prompt:
        for g in range(n_grp):
            vt_s[1 + g] = vtf[:, g * LANES:(g + 1) * LANES]
    else:
        for s in range(nb):
            vt_s[s, :, WINDOW:BAND] = vtf[:, s * L:(s + 1) * L]

    lane_gq = lax.broadcasted_iota(jnp.int32, (1, GQ), 1)
    sink_rows = [_head_row(lambda h: sink_ref[h], k, lane_gq) for k in range(KV_HEADS)]

    def attn_block(c2):
        chunks = []
        if is_prompt:
            v_win = jnp.concatenate([vt_s[c2], vt_s[c2 + 1]], axis=1)
            for par in range(2):
                r = c2 * PAIR + par * CHUNK
                pos = t * T + r - WINDOW + lax.broadcasted_iota(jnp.int32, (BAND, 1), 0)
                chunks.append((q_s[r:r + CHUNK, :],
                               [kd_s[k, r:r + BAND, :] for k in range(KV_HEADS)],
                               v_win, par == 1, pos >= 0))
        else:
            for par in range(2):
                s = 2 * c2 + par
                chunks.append((q_s[s * CHUNK:(s + 1) * CHUNK, :],
                               [kd_s[s, k] for k in range(KV_HEADS)],
                               vt_s[s], False, None))
        _attn_pair(chunks, bias_ref, sink_rows, st_s, pb_s, (c2 % 2) * 2 * KV_HEADS, attT_s, c2)
        att_s[c2 * PAIR:(c2 + 1) * PAIR, :] = attT_s[c2].T.astype(bf16)

    def ga_piece(c):
        ga_s[:, _cols(c)] = _sigmoid(_mm(hv, win_ref, O_GA + c * COL_BLK, COL_BLK))

    _interleave([functools.partial(ga_piece, c) for c in range(N_CC)],
                [functools.partial(attn_block, c2) for c2 in range(n_grp)], blocks_first=True)

    av = att_s[...]
    for c in range(N_CC):
        a1c = ga_s[:, _cols(c)] * _mm(av, wao_ref, c * COL_BLK, COL_BLK)
        for s in range(nb):
            a1_ref[s, :, _cols(c)] = a1c[s * L:(s + 1) * L]

    if is_prompt:
        @pl.when(t == n_tiles - 1)
        def _():
            kvf = _mm(h_s[T - WINDOW:T, :], win_ref, O_K, 2 * KV_W)
            nk_ref[0] = kvf[:, 0:KV_W]
            nv_ref[0] = kvf[:, KV_W:2 * KV_W]

        kd_s[:, 0:WINDOW, :] = kd_s[:, T:T + WINDOW, :]
        vt_s[0] = vt_s[n_grp]


def _attn_call(x, mod, caches, weights, nb, L, is_prompt, name):
    B, S, _ = x.shape
    T = nb * L
    if is_prompt:
        n_tiles = S // L
        grid = (B, n_tiles)
        xmap = lambda b, t: (b, t, 0)
        smap = lambda b, t: (b, 0, 0)
        kd_shape = (KV_HEADS, WINDOW + T, LANES)
        vt_shape = (1 + T // PAIR, KV_W, LANES)
    else:
        n_tiles = B // nb
        grid = (n_tiles,)
        xmap = smap = lambda i: (i, 0, 0)
        kd_shape = (nb, KV_HEADS, BAND, LANES)
        vt_shape = (nb, KV_W, 2 * LANES)
    body = functools.partial(_attn_body, is_prompt=is_prompt, nb=nb, L=L, n_tiles=n_tiles)
    cache_specs = [pl.BlockSpec((nb, WINDOW, KV_W), smap)] * len(caches)
    weight_specs = [
        _const_spec((KV_HEADS, BAND, GQ)),
        pl.BlockSpec(memory_space=pltpu.SMEM),
        _const_spec((1, D_MODEL)),
        _const_spec((D_MODEL, IN_W)),
        _const_spec((D_MODEL, KV_HEADS * LANES)),
        _const_spec((KV_W, D_MODEL)),
        _const_spec((Q_W, D_MODEL)),
    ]
    return pl.pallas_call(
        body,
        grid=grid,
        in_specs=[pl.BlockSpec((nb, L, D_MODEL), xmap), pl.BlockSpec((nb, 6, D_MODEL), smap)]
        + cache_specs + weight_specs,
        out_specs=[pl.BlockSpec((nb, L, D_MODEL), xmap),
                   pl.BlockSpec((nb, WINDOW, KV_W), smap),
                   pl.BlockSpec((nb, WINDOW, KV_W), smap)],
        out_shape=[jax.ShapeDtypeStruct((B, S, D_MODEL), f32),
                   jax.ShapeDtypeStruct((B, WINDOW, KV_W), f32),
                   jax.ShapeDtypeStruct((B, WINDOW, KV_W), f32)],
        scratch_shapes=[
            pltpu.VMEM((T, D_MODEL), bf16),
            pltpu.VMEM((T, Q_W), bf16),
            pltpu.VMEM(kd_shape, bf16),
            pltpu.VMEM(vt_shape, bf16),
            pltpu.VMEM((4 * KV_HEADS, BAND, GQ), f32),
            pltpu.VMEM((4 * KV_HEADS, 2 * LANES, GQ), bf16),
            pltpu.VMEM((T // PAIR, Q_W, PAIR), f32),
            pltpu.VMEM((T, Q_W), bf16),
            pltpu.VMEM((T, D_MODEL), f32),
        ],
        compiler_params=pltpu.CompilerParams(
            dimension_semantics=("arbitrary",) * len(grid), vmem_limit_bytes=VMEM_LIMIT),
        name=name,
    )(x, mod, *caches, *weights)


def _conv_block(zb_ref, s, j, i, dww_ref, cb_ref, out_row0):
    r0 = i * 32
    accs = [jnp.zeros((8, LANES), f32) for _ in range(4)]
    for k in range(CONV_W):
        wk = dww_ref[k:k + 1, j * LANES:(j + 1) * LANES]
        for a in range(4):
            start = r0 + (a // 2) * 16 + (a % 2) + (HIST - (CONV_W - 1)) + k
            accs[a] = accs[a] + zb_ref[s, j, pl.ds(start, 8, stride=2), :] * wk
    for a in range(4):
        start = out_row0 + r0 + (a // 2) * 16 + (a % 2)
        cb_ref[j, pl.ds(start, 8, stride=2), :] = accs[a]


def _mix_body(*refs, is_prompt, nb, L, n_tiles, tiles_per_seq):
    if is_prompt:
        (xa_ref, xb_ref, a1_ref, mod_ref, g1_ref, win_ref, dww_ref, dwb_ref,
         lng_ref, lnb_ref, wco_ref, wo_ref, g2_ref, fg_ref, wup_ref, wdn_ref,
         y_ref, nc_ref,
         h_s, zb_s, cb_s, cbf_s, gb_s, acc_s, x1_s, h2_s, act_s, d_s) = refs
        cc_ref = None
    else:
        (xa_ref, xb_ref, a1_ref, mod_ref, cc_ref, g1_ref, win_ref, dww_ref, dwb_ref,
         lng_ref, lnb_ref, wco_ref, wo_ref, g2_ref, fg_ref, wup_ref, wdn_ref,
         y_ref, nc_ref,
         h_s, zb_s, cb_s, cbf_s, gb_s, acc_s, x1_s, h2_s, act_s, d_s) = refs
    T = nb * L
    step = pl.program_id(0)
    t_in_seq = jnp.minimum(step, n_tiles - 1) % tiles_per_seq
    grp_a, grp_b, grp_c = [jnp.clip(step - lag, 0, n_tiles - 1) // tiles_per_seq for lag in range(3)]

    @pl.when(step == 0)
    def _():
        x1_s[...] = jnp.zeros((T, D_MODEL), f32)
        h_s[...] = jnp.zeros((T, D_MODEL), bf16)
        cb_s[...] = jnp.zeros((N_SLABS, T, LANES), f32)

    if is_prompt:
        @pl.when(t_in_seq == 0)
        def _():
            zb_s[0, :, 0:HIST, :] = jnp.zeros((N_SLABS, HIST, LANES), f32)

    def norm2_block(i):
        s = i * ROW_BLK // L
        rows = slice(i * ROW_BLK, (i + 1) * ROW_BLK)
        hh = _rms(x1_s[rows, :], g2_ref[...]) * (1.0 + mod_ref[grp_c, s, 4:5, :]) + mod_ref[grp_c, s, 3:4, :]
        h2_s[rows, :] = hh.astype(bf16)

    def norm1_block(i):
        s, r = divmod(i * ROW_BLK, L)
        rows = slice(i * ROW_BLK, (i + 1) * ROW_BLK)
        hh = _rms(xa_ref[0, s, r:r + ROW_BLK, :], g1_ref[...]) * (1.0 + mod_ref[grp_a, s, 1:2, :]) + mod_ref[grp_a, s, 0:1, :]
        h_s[rows, :] = hh.astype(bf16)

    def glu_piece(c):
        hv = h_s[...]
        za = _mm(hv, win_ref, O_GLU + c * COL_BLK, COL_BLK)
        zg = _mm(hv, win_ref, O_GLU + D_MODEL + c * COL_BLK, COL_BLK)
        zc = za * _sigmoid(zg)
        for jj in range(COL_BLK // LANES):
            j = c * (COL_BLK // LANES) + jj
            z = zc[:, jj * LANES:(jj + 1) * LANES]
            for s in range(nb):
                zb_s[s, j, HIST:HIST + L, :] = z[s * L:(s + 1) * L]
                if not is_prompt:
                    nc_ref[0, s, :, j * LANES:(j + 1) * LANES] = z[s * L + L - (CONV_W - 1):(s + 1) * L]

    def gb_piece(c):
        gb_s[:, _cols(c)] = _sigmoid(_mm(h_s[...], win_ref, O_GB + c * COL_BLK, COL_BLK))

    def up_piece(c):
        h2v = h2_s[...]
        gate = _mm(h2v, wup_ref, c * COL_BLK, COL_BLK)
        up = _mm(h2v, wup_ref, D_FF + c * COL_BLK, COL_BLK)
        act_s[c % 2] = (gate * _sigmoid(gate) * up).astype(bf16)

    def down_piece(c):
        d_s[...] += jnp.dot(act_s[c % 2], wdn_ref[c * COL_BLK:(c + 1) * COL_BLK, :], preferred_element_type=f32)

    def ln_block(i):
        rows = slice(i * ROW_BLK, (i + 1) * ROW_BLK)
        ys = [cb_s[j, rows, :] + dwb_ref[:, j * LANES:(j + 1) * LANES] for j in range(N_SLABS)]
        tot = ys[0]
        for j in range(1, N_SLABS):
            tot = tot + ys[j]
        mu = jnp.sum(tot, axis=-1, keepdims=True) * (1.0 / D_MODEL)
        ds = [y - mu for y in ys]
        sq = ds[0] * ds[0]
        for j in range(1, N_SLABS):
            sq = sq + ds[j] * ds[j]
        var = jnp.sum(sq, axis=-1, keepdims=True) * (1.0 / D_MODEL)
        rstd = lax.rsqrt(var + EPS)
        for j in range(N_SLABS):
            y = ds[j] * rstd * lng_ref[:, j * LANES:(j + 1) * LANES] + lnb_ref[:, j * LANES:(j + 1) * LANES]
            cbf_s[rows, j * LANES:(j + 1) * LANES] = (y * _sigmoid(y)).astype(bf16)

    def yb_piece(c):
        yb = _mm(cbf_s[...], wco_ref, c * COL_BLK, COL_BLK)
        for s in range(nb):
            rows = slice(s * L, (s + 1) * L)
            acc_s[rows, _cols(c)] = a1_ref[0, s, :, _cols(c)] + gb_s[rows, _cols(c)] * yb[rows]

    def final_block(i):
        s, r = divmod(i * ROW_BLK, L)
        rows = slice(i * ROW_BLK, (i + 1) * ROW_BLK)
        x2 = x1_s[rows, :] + mod_ref[grp_c, s, 5:6, :] * d_s[rows, :]
        y_ref[0, s, r:r + ROW_BLK, :] = _rms(x2, fg_ref[...])

    def out_piece(c):
        m = _mm(acc_s[...].astype(bf16), wo_ref, c * COL_BLK, COL_BLK)
        for s in range(nb):
            rows = slice(s * L, (s + 1) * L)
            x1_s[rows, _cols(c)] = xb_ref[0, s, :, _cols(c)] + mod_ref[grp_b, s, 2:3, _cols(c)] * m[rows]

    n_blk = T // ROW_BLK
    _interleave([functools.partial(gb_piece, c) for c in range(N_CC)],
                [functools.partial(ln_block, i) for i in range(n_blk)])
    for i in range(n_blk):
        norm1_block(i)
    if not is_prompt:
        for s in range(nb):
            for j in range(N_SLABS):
                zb_s[s, j, HIST - (CONV_W - 1):HIST, :] = cc_ref[0, s, :, j * LANES:(j + 1) * LANES]
    _interleave([functools.partial(glu_piece, c) for c in range(N_CC)],
                [functools.partial(norm2_block, i) for i in range(n_blk)])
    d_s[...] = jnp.zeros((T, D_MODEL), f32)
    up_piece(0)
    pieces = []
    for c in range(N_FF):
        if c + 1 < N_FF:
            pieces.append(functools.partial(up_piece, c + 1))
        pieces.append(functools.partial(down_piece, c))
    pieces += [functools.partial(yb_piece, c) for c in range(N_CC)]
    conv_blocks = [functools.partial(_conv_block, zb_s, s, j, i, dww_ref, cb_s, s * L)
                   for j in range(N_SLABS) for s in range(nb) for i in range(L // 32)]
    _interleave(pieces, conv_blocks)
    for i in range(n_blk):
        final_block(i)
    for c in range(N_CC):
        out_piece(c)

    if is_prompt:
        @pl.when(t_in_seq == tiles_per_seq - 1)
        def _():
            for j in range(N_SLABS):
                nc_ref[0, 0, :, j * LANES:(j + 1) * LANES] = zb_s[0, j, HIST + L - (CONV_W - 1):HIST + L, :]

        zb_s[0, :, 0:HIST, :] = zb_s[0, :, L:L + HIST, :]


def _mix_call(x, a1, mod, cc, weights, nb, L, is_prompt, name):
    n_tiles = x.shape[0]
    tiles_per_seq = n_tiles // mod.shape[0]
    T = nb * L

    def lagged(lag, per_group):
        def index_map(i):
            t = jnp.clip(i - lag, 0, n_tiles - 1)
            return (t // per_group, 0, 0, 0)
        return index_map

    body = functools.partial(_mix_body, is_prompt=is_prompt, nb=nb, L=L, n_tiles=n_tiles,
                             tiles_per_seq=tiles_per_seq)
    row = _const_spec((1, D_MODEL))
    tile_blk = (1, nb, L, D_MODEL)
    in_specs = [pl.BlockSpec(tile_blk, lagged(0, 1)),
                pl.BlockSpec(tile_blk, lagged(1, 1)),
                pl.BlockSpec(tile_blk, lagged(1, 1)),
                _const_spec(mod.shape)]
    args = [x, x, a1, mod]
    if not is_prompt:
        in_specs.append(pl.BlockSpec((1, nb, CONV_W - 1, D_MODEL), lagged(0, 1)))
        args.append(cc)
    in_specs += [
        row,
        _const_spec((D_MODEL, IN_W)),
        _const_spec((CONV_W, D_MODEL)),
        row, row, row,
        _const_spec((D_MODEL, D_MODEL)),
        _const_spec((D_MODEL, D_MODEL)),
        row, row,
        _const_spec((D_MODEL, 2 * D_FF)),
        _const_spec((D_FF, D_MODEL)),
    ]
    n_groups = mod.shape[0]
    return pl.pallas_call(
        body,
        grid=(n_tiles + 2,),
        in_specs=in_specs,
        out_specs=[pl.BlockSpec(tile_blk, lagged(2, 1)),
                   pl.BlockSpec((1, nb, CONV_W - 1, D_MODEL), lagged(0, tiles_per_seq))],
        out_shape=[jax.ShapeDtypeStruct(x.shape, f32),
                   jax.ShapeDtypeStruct((n_groups, nb, CONV_W - 1, D_MODEL), f32)],
        scratch_shapes=[
            pltpu.VMEM((T, D_MODEL), bf16),
            pltpu.VMEM((nb, N_SLABS, HIST + L, LANES), f32),
            pltpu.VMEM((N_SLABS, T, LANES), f32),
            pltpu.VMEM((T, D_MODEL), bf16),
            pltpu.VMEM((T, D_MODEL), f32),
            pltpu.VMEM((T, D_MODEL), f32),
            pltpu.VMEM((T, D_MODEL), f32),
            pltpu.VMEM((T, D_MODEL), bf16),
            pltpu.VMEM((2, T, COL_BLK), bf16),
            pltpu.VMEM((T, D_MODEL), f32),
        ],
        compiler_params=pltpu.CompilerParams(
            dimension_semantics=("arbitrary",), vmem_limit_bytes=VMEM_LIMIT),
        name=name,
    )(*args, *weights)


def kernel(x_prompt, x_sample, cache_k, cache_v, cache_conv, c_prompt, c_sample, rel_table, w_ada, b_ada, norm1_g, norm2_g, w_in, sink, w_attn_out, dw_w, dw_b, conv_ln_g, conv_ln_b, w_conv_out, w_out, w_ffn_up, w_ffn_down, final_g):
    assert w_ada.shape[0] == 1, "single-layer kernel"
    B, S, _ = x_prompt.shape
    DB, DS, _ = x_sample.shape
    nb_attn = ATTN_SAMPLE_TOKENS // DS
    nb_mix = MIX_TOKENS // DS

    bias = _bias_table(rel_table)
    n_seq = B + DB
    pad = (-n_seq) % 8
    c_all = jnp.concatenate([c_prompt, c_sample, jnp.zeros((pad, D_MODEL), f32)], axis=0)
    mod = _modulation(c_all, w_ada[0], b_ada).reshape(n_seq + pad, 6, D_MODEL)
    mod_p, mod_s = mod[:B], mod[B:B + DB]

    row = lambda v: v.reshape(1, D_MODEL)
    w = w_in[0]
    w_k = w[:, O_K:O_V].astype(bf16).reshape(D_MODEL, KV_HEADS, 1, HEAD_DIM)
    w_kd = jnp.broadcast_to(w_k, (D_MODEL, KV_HEADS, 2, HEAD_DIM)).reshape(D_MODEL, KV_HEADS * LANES)
    w_in_b = w.astype(bf16)
    attn_w = (bias, sink[0], row(norm1_g[0]), w_in_b, w_kd, w[:, O_V:O_GLU].T.astype(bf16),
              w_attn_out[0].astype(bf16))
    mix_w = (row(norm1_g[0]), w_in_b, dw_w[0], row(dw_b[0]),
             row(conv_ln_g[0]), row(conv_ln_b[0]), w_conv_out[0].astype(bf16), w_out[0].astype(bf16),
             row(norm2_g[0]), row(final_g), w_ffn_up[0].astype(bf16), w_ffn_down[0].astype(bf16))

    a1p, nkp, nvp = _attn_call(x_prompt, mod_p, (), attn_w, 1, ATTN_TOKENS, True, "attn_prompt")
    ck = cache_k[0].reshape(DB, WINDOW, KV_W)
    cv = cache_v[0].reshape(DB, WINDOW, KV_W)
    a1s, nks, nvs = _attn_call(x_sample, mod_s, (ck, cv), attn_w, nb_attn, DS, False, "attn_sample")

    tiles = lambda a, n, l: a.reshape(-1, n, l, D_MODEL)
    y_p, ncp = _mix_call(tiles(x_prompt, 1, MIX_TOKENS), tiles(a1p, 1, MIX_TOKENS), mod_p[:, None], None,
                         mix_w, 1, MIX_TOKENS, True, "mix_prompt")
    y_s, ncs = _mix_call(tiles(x_sample, nb_mix, DS), tiles(a1s, nb_mix, DS), tiles(mod_s, nb_mix, 6),
                         tiles(cache_conv[0], nb_mix, CONV_W - 1), mix_w, nb_mix, DS, False, "mix_sample")

    kv5 = lambda a: a.reshape(1, a.shape[0], WINDOW, KV_HEADS, HEAD_DIM)
    return (y_p.reshape(B, S, D_MODEL), y_s.reshape(DB, DS, D_MODEL), kv5(nkp), kv5(nvp),
            ncp.reshape(1, B, CONV_W - 1, D_MODEL), kv5(nks), kv5(nvs), ncs.reshape(1, DB, CONV_W - 1, D_MODEL))
```

```python
import functools
import math

import jax
import jax.numpy as jnp
from jax import lax
from jax.experimental import pallas as pl
from jax.experimental.pallas import tpu as pltpu

f32 = jnp.float32
bf16 = jnp.bfloat16

D_MODEL = 1024
N_HEADS = 16
KV_HEADS = 4
HEAD_DIM = 64
GROUP = N_HEADS // KV_HEADS
WINDOW = 128
CHUNK = 64
BAND = WINDOW + CHUNK
CONV_W = 31
D_FF = 2816
NUM_BUCKETS = 32
MAX_DISTANCE = 128
EPS = 1e-6
NEG = -1e30
Q_W = N_HEADS * HEAD_DIM
KV_W = KV_HEADS * HEAD_DIM
O_K = Q_W
O_V = O_K + KV_W
O_GLU = O_V + KV_W
O_GA = O_GLU + 2 * D_MODEL
O_GB = O_GA + D_MODEL
IN_W = O_GB + D_MODEL

LANES = 128
N_SLABS = D_MODEL // LANES
PAIR = 2 * CHUNK
GQ = GROUP * CHUNK
HIST = 32
ROW_BLK = 32
COL_BLK = 256
N_CC = D_MODEL // COL_BLK
N_FF = D_FF // COL_BLK
VMEM_LIMIT = 56 * 1024 * 1024
ATTN_TOKENS = 1024
ATTN_SAMPLE_TOKENS = 512
MIX_TOKENS = 256

_NT = (((1,), (1,)), ((), ()))


def _sigmoid(x):
    return 1.0 / (1.0 + jnp.exp(-x))


def _rms(x, g):
    ms = jnp.mean(x * x, axis=-1, keepdims=True)
    return x * lax.rsqrt(ms + EPS) * g


def _mm(a, w_ref, c0, width):
    return jnp.dot(a, w_ref[:, c0:c0 + width], preferred_element_type=f32)


def _cols(c):
    return slice(c * COL_BLK, (c + 1) * COL_BLK)


def _head_row(vals, k, lane):
    row = jnp.full((1, GQ), vals(GROUP * k + GROUP - 1), f32)
    for g in range(GROUP - 2, -1, -1):
        row = jnp.where(lane < CHUNK * (g + 1), vals(GROUP * k + g), row)
    return row


def _interleave(pieces, blocks, blocks_first=False):
    order = [((n + 0.5) / len(pieces), int(blocks_first), fn) for n, fn in enumerate(pieces)]
    order += [((n + 0.5) / len(blocks), int(not blocks_first), fn) for n, fn in enumerate(blocks)]
    for _, _, fn in sorted(order, key=lambda item: item[:2]):
        fn()


def _const_spec(shape):
    nd = len(shape)
    return pl.BlockSpec(shape, lambda *_: (0,) * nd, pipeline_mode=pl.Buffered(1))


def _rel_bucket(rel):
    nb = NUM_BUCKETS // 2
    max_exact = nb // 2
    ret = (rel > 0).astype(jnp.int32) * nb
    n = jnp.abs(rel)
    nf = jnp.maximum(n, 1).astype(f32)
    large = max_exact + (jnp.log(nf / max_exact) / math.log(MAX_DISTANCE / max_exact)
                         * (nb - max_exact)).astype(jnp.int32)
    large = jnp.minimum(large, nb - 1)
    return ret + jnp.where(n < max_exact, n, large)


def _bias_body(idx_ref, tab_ref, o_ref):
    idx = idx_ref[...]
    lane = lax.broadcasted_iota(jnp.int32, (1, GQ), 1)
    for k in range(KV_HEADS):
        acc = jnp.zeros((BAND, GQ), f32)
        for b in range(NUM_BUCKETS):
            acc = jnp.where(idx == b, _head_row(lambda h: tab_ref[b, h], k, lane), acc)
        o_ref[k] = acc


def _bias_table(rel_table):
    kj = jnp.arange(BAND, dtype=jnp.int32)
    rel = kj[None, :] - WINDOW - jnp.arange(CHUNK, dtype=jnp.int32)[:, None]
    idx_t = jnp.tile(_rel_bucket(rel).T, (1, GROUP))
    return pl.pallas_call(
        _bias_body,
        in_specs=[pl.BlockSpec(memory_space=pltpu.VMEM), pl.BlockSpec(memory_space=pltpu.SMEM)],
        out_specs=pl.BlockSpec(memory_space=pltpu.VMEM),
        out_shape=jax.ShapeDtypeStruct((KV_HEADS, BAND, GQ), f32),
        name="rel_bias",
    )(idx_t, rel_table)


def _mod_body(c_ref, w_ref, b_ref, o_ref):
    c = c_ref[...]
    s = c * _sigmoid(c)
    o_ref[...] = jnp.dot(s.astype(bf16), w_ref[...].astype(bf16), preferred_element_type=f32) + b_ref[...]


def _modulation(c_all, w_ada, b_ada):
    rows = c_all.shape[0]
    blk = 2048
    return pl.pallas_call(
        _mod_body,
        grid=(6 * D_MODEL // blk,),
        in_specs=[pl.BlockSpec((rows, D_MODEL), lambda j: (0, 0)),
                  pl.BlockSpec((D_MODEL, blk), lambda j: (0, j)),
                  pl.BlockSpec((1, blk), lambda j: (0, j))],
        out_specs=pl.BlockSpec((rows, blk), lambda j: (0, j)),
        out_shape=jax.ShapeDtypeStruct((rows, 6 * D_MODEL), f32),
        name="adaln_mod",
    )(c_all, w_ada, b_ada)


def _attn_pair(chunks, bias_ref, sink_rows, st_s, pb_s, slot0, attT_ref, grp):
    lane_lo = lax.broadcasted_iota(jnp.int32, (CHUNK, LANES), 1) < HEAD_DIM
    n_pad = 2 * LANES - BAND
    rden = {}
    for par, (q_c, kd_bands, v_win, pad_top, valid) in enumerate(chunks):
        for k in range(KV_HEADS):
            blocks = []
            for p in range(2):
                qc = q_c[:, (2 * k + p) * LANES:(2 * k + p + 1) * LANES]
                blocks.append(jnp.where(lane_lo, qc, jnp.zeros_like(qc)))
                blocks.append(jnp.where(lane_lo, jnp.zeros_like(qc), qc))
            qm = jnp.concatenate(blocks, axis=0)
            st_s[slot0 + par * KV_HEADS + k] = lax.dot_general(kd_bands[k], qm, _NT, preferred_element_type=f32)
    for par, (q_c, kd_bands, v_win, pad_top, valid) in enumerate(chunks):
        for k in range(KV_HEADS):
            i = slot0 + par * KV_HEADS + k
            st = st_s[i] + bias_ref[k]
            if valid is not None:
                st = jnp.where(valid, st, NEG)
            m = jnp.maximum(jnp.max(st, axis=0, keepdims=True), sink_rows[k])
            p_ = jnp.exp(st - m)
            den = jnp.sum(p_, axis=0, keepdims=True) + jnp.exp(sink_rows[k] - m)
            r0 = n_pad if pad_top else 0
            pb_s[i, r0:r0 + BAND, :] = p_.astype(bf16)
            rden[i] = 1.0 / den
            z0 = 0 if pad_top else BAND
            pb_s[i, z0:z0 + n_pad, :] = jnp.zeros((n_pad, GQ), bf16)
    for par, (q_c, kd_bands, v_win, pad_top, valid) in enumerate(chunks):
        for k in range(KV_HEADS):
            i = slot0 + par * KV_HEADS + k
            ot = jnp.dot(v_win[k * HEAD_DIM:(k + 1) * HEAD_DIM, :], pb_s[i], preferred_element_type=f32) * rden[i]
            for g in range(GROUP):
                h = GROUP * k + g
                attT_ref[grp, h * HEAD_DIM:(h + 1) * HEAD_DIM, par * CHUNK:(par + 1) * CHUNK] = (
                    ot[:, g * CHUNK:(g + 1) * CHUNK])


def _attn_body(*refs, is_prompt, nb, L, n_tiles):
    if is_prompt:
        (x_ref, mod_ref, bias_ref, sink_ref, g1_ref, win_ref, wkd_ref, wvt_ref, wao_ref,
         a1_ref, nk_ref, nv_ref,
         h_s, q_s, kd_s, vt_s, st_s, pb_s, attT_s, att_s, ga_s) = refs
        ck_ref = cv_ref = None
    else:
        (x_ref, mod_ref, ck_ref, cv_ref, bias_ref, sink_ref, g1_ref, win_ref, wkd_ref, wvt_ref, wao_ref,
         a1_ref, nk_ref, nv_ref,
         h_s, q_s, kd_s, vt_s, st_s, pb_s, attT_s, att_s, ga_s) = refs
    T = nb * L
    n_grp = T // PAIR

    if is_prompt:
        t = pl.program_id(1)

        @pl.when(t == 0)
        def _():
            kd_s[:, 0:WINDOW, :] = jnp.zeros((KV_HEADS, WINDOW, LANES), bf16)
            vt_s[0] = jnp.zeros((KV_W, LANES), bf16)

    for s in range(nb):
        for i in range(L // ROW_BLK):
            rows = slice(i * ROW_BLK, (i + 1) * ROW_BLK)
            hh = _rms(x_ref[s, rows, :], g1_ref[...]) * (1.0 + mod_ref[s, 1:2, :]) + mod_ref[s, 0:1, :]
            h_s[s * L + i * ROW_BLK:s * L + (i + 1) * ROW_BLK, :] = hh.astype(bf16)
    hv = h_s[...]

    if not is_prompt:
        for s in range(nb):
            ck = ck_ref[s]
            for k in range(KV_HEADS):
                ckk = ck[:, k * HEAD_DIM:(k + 1) * HEAD_DIM].astype(bf16)
                kd_s[s, k, 0:WINDOW, :] = jnp.concatenate([ckk, ckk], axis=1)
            vt_s[s, :, 0:WINDOW] = cv_ref[s].T.astype(bf16)
            vt_s[s, :, BAND:2 * LANES] = jnp.zeros((KV_W, 2 * LANES - BAND), bf16)
            nk_ref[s, 0:WINDOW - L, :] = ck[L:WINDOW, :]
            nv_ref[s, 0:WINDOW - L, :] = cv_ref[s, L:WINDOW, :]
        kvf = _mm(hv, win_ref, O_K, 2 * KV_W)
        for s in range(nb):
            nk_ref[s, WINDOW - L:WINDOW, :] = kvf[s * L:(s + 1) * L, 0:KV_W]
            nv_ref[s, WINDOW - L:WINDOW, :] = kvf[s * L:(s + 1) * L, KV_W:2 * KV_W]

    for c in range(Q_W // COL_BLK):
        qf = _mm(hv, win_ref, c * COL_BLK, COL_BLK)
        q_s[:, _cols(c)] = (qf * (HEAD_DIM ** -0.5)).astype(bf16)
    for c in range(KV_HEADS * LANES // COL_BLK):
        kdc = _mm(hv, wkd_ref, c * COL_BLK, COL_BLK)
        for kk in range(COL_BLK // LANES):
            k = c * (COL_BLK // LANES) + kk
            kdk = kdc[:, kk * LANES:(kk + 1) * LANES].astype(bf16)
            if is_prompt:
                kd_s[k, WINDOW:WINDOW + T, :] = kdk
            else:
                for s in range(nb):
                    kd_s[s, k, WINDOW:BAND, :] = kdk[s * L:(s + 1) * L]
    vtf = lax.dot_general(wvt_ref[...], hv, _NT, preferred_element_type=f32).astype(bf16)
    if is_prompt:
        for g in range(n_grp):
            vt_s[1 + g] = vtf[:, g * LANES:(g + 1) * LANES]
    else:
        for s in range(nb):
            vt_s[s, :, WINDOW:BAND] = vtf[:, s * L:(s + 1) * L]

    lane_gq = lax.broadcasted_iota(jnp.int32, (1, GQ), 1)
    sink_rows = [_head_row(lambda h: sink_ref[h], k, lane_gq) for k in range(KV_HEADS)]

    def attn_block(c2):
        chunks = []
        if is_prompt:
            v_win = jnp.concatenate([vt_s[c2], vt_s[c2 + 1]], axis=1)
            for par in range(2):
                r = c2 * PAIR + par * CHUNK
                pos = t * T + r - WINDOW + lax.broadcasted_iota(jnp.int32, (BAND, 1), 0)
                chunks.append((q_s[r:r + CHUNK, :],
                               [kd_s[k, r:r + BAND, :] for k in range(KV_HEADS)],
                               v_win, par == 1, pos >= 0))
        else:
            for par in range(2):
                s = 2 * c2 + par
                chunks.append((q_s[s * CHUNK:(s + 1) * CHUNK, :],
                               [kd_s[s, k] for k in range(KV_HEADS)],
                               vt_s[s], False, None))
        _attn_pair(chunks, bias_ref, sink_rows, st_s, pb_s, (c2 % 2) * 2 * KV_HEADS, attT_s, c2)
        att_s[c2 * PAIR:(c2 + 1) * PAIR, :] = attT_s[c2].T.astype(bf16)

    def ga_piece(c):
        ga_s[:, _cols(c)] = _sigmoid(_mm(hv, win_ref, O_GA + c * COL_BLK, COL_BLK))

    _interleave([functools.partial(ga_piece, c) for c in range(N_CC)],
                [functools.partial(attn_block, c2) for c2 in range(n_grp)], blocks_first=True)

    av = att_s[...]
    for c in range(N_CC):
        a1c = ga_s[:, _cols(c)] * _mm(av, wao_ref, c * COL_BLK, COL_BLK)
        for s in range(nb):
            a1_ref[s, :, _cols(c)] = a1c[s * L:(s + 1) * L]

    if is_prompt:
        @pl.when(t == n_tiles - 1)
        def _():
            kvf = _mm(h_s[T - WINDOW:T, :], win_ref, O_K, 2 * KV_W)
            nk_ref[0] = kvf[:, 0:KV_W]
            nv_ref[0] = kvf[:, KV_W:2 * KV_W]

        kd_s[:, 0:WINDOW, :] = kd_s[:, T:T + WINDOW, :]
        vt_s[0] = vt_s[n_grp]


def _attn_call(x, mod, caches, weights, nb, L, is_prompt, name):
    B, S, _ = x.shape
    T = nb * L
    if is_prompt:
        n_tiles = S // L
        grid = (B, n_tiles)
        xmap = lambda b, t: (b, t, 0)
        smap = lambda b, t: (b, 0, 0)
        kd_shape = (KV_HEADS, WINDOW + T, LANES)
        vt_shape = (1 + T // PAIR, KV_W, LANES)
    else:
        n_tiles = B // nb
        grid = (n_tiles,)
        xmap = smap = lambda i: (i, 0, 0)
        kd_shape = (nb, KV_HEADS, BAND, LANES)
        vt_shape = (nb, KV_W, 2 * LANES)
    body = functools.partial(_attn_body, is_prompt=is_prompt, nb=nb, L=L, n_tiles=n_tiles)
    cache_specs = [pl.BlockSpec((nb, WINDOW, KV_W), smap)] * len(caches)
    weight_specs = [
        _const_spec((KV_HEADS, BAND, GQ)),
        pl.BlockSpec(memory_space=pltpu.SMEM),
        _const_spec((1, D_MODEL)),
        _const_spec((D_MODEL, IN_W)),
        _const_spec((D_MODEL, KV_HEADS * LANES)),
        _const_spec((KV_W, D_MODEL)),
        _const_spec((Q_W, D_MODEL)),
    ]
    return pl.pallas_call(
        body,
        grid=grid,
        in_specs=[pl.BlockSpec((nb, L, D_MODEL), xmap), pl.BlockSpec((nb, 6, D_MODEL), smap)]
        + cache_specs + weight_specs,
        out_specs=[pl.BlockSpec((nb, L, D_MODEL), xmap),
                   pl.BlockSpec((nb, WINDOW, KV_W), smap),
                   pl.BlockSpec((nb, WINDOW, KV_W), smap)],
        out_shape=[jax.ShapeDtypeStruct((B, S, D_MODEL), f32),
                   jax.ShapeDtypeStruct((B, WINDOW, KV_W), f32),
                   jax.ShapeDtypeStruct((B, WINDOW, KV_W), f32)],
        scratch_shapes=[
            pltpu.VMEM((T, D_MODEL), bf16),
            pltpu.VMEM((T, Q_W), bf16),
            pltpu.VMEM(kd_shape, bf16),
            pltpu.VMEM(vt_shape, bf16),
            pltpu.VMEM((4 * KV_HEADS, BAND, GQ), f32),
            pltpu.VMEM((4 * KV_HEADS, 2 * LANES, GQ), bf16),
            pltpu.VMEM((T // PAIR, Q_W, PAIR), f32),
            pltpu.VMEM((T, Q_W), bf16),
            pltpu.VMEM((T, D_MODEL), f32),
        ],
        compiler_params=pltpu.CompilerParams(
            dimension_semantics=("arbitrary",) * len(grid), vmem_limit_bytes=VMEM_LIMIT),
        name=name,
    )(x, mod, *caches, *weights)


def _conv_block(zb_ref, s, j, i, dww_ref, cb_ref, out_row0):
    r0 = i * 32
    accs = [jnp.zeros((8, LANES), f32) for _ in range(4)]
    for k in range(CONV_W):
        wk = dww_ref[k:k + 1, j * LANES:(j + 1) * LANES]
        for a in range(4):
            start = r0 + (a // 2) * 16 + (a % 2) + (HIST - (CONV_W - 1)) + k
            accs[a] = accs[a] + zb_ref[s, j, pl.ds(start, 8, stride=2), :] * wk
    for a in range(4):
        start = out_row0 + r0 + (a // 2) * 16 + (a % 2)
        cb_ref[j, pl.ds(start, 8, stride=2), :] = accs[a]


def _mix_body(*refs, is_prompt, nb, L, n_tiles, tiles_per_seq):
    if is_prompt:
        (xa_ref, xb_ref, a1_ref, mod_ref, g1_ref, win_ref, dww_ref, dwb_ref,
         lng_ref, lnb_ref, wco_ref, wo_ref, g2_ref, fg_ref, wup_ref, wdn_ref,
         y_ref, nc_ref,
         h_s, zb_s, cb_s, cbf_s, gb_s, acc_s, x1_s, h2_s, act_s, d_s) = refs
        cc_ref = None
    else:
        (xa_ref, xb_ref, a1_ref, mod_ref, cc_ref, g1_ref, win_ref, dww_ref, dwb_ref,
         lng_ref, lnb_ref, wco_ref, wo_ref, g2_ref, fg_ref, wup_ref, wdn_ref,
         y_ref, nc_ref,
         h_s, zb_s, cb_s, cbf_s, gb_s, acc_s, x1_s, h2_s, act_s, d_s) = refs
    T = nb * L
    step = pl.program_id(0)
    t_in_seq = jnp.minimum(step, n_tiles - 1) % tiles_per_seq
    grp_a, grp_b, grp_c = [jnp.clip(step - lag, 0, n_tiles - 1) // tiles_per_seq for lag in range(3)]

    @pl.when(step == 0)
    def _():
        x1_s[...] = jnp.zeros((T, D_MODEL), f32)
        h_s[...] = jnp.zeros((T, D_MODEL), bf16)
        cb_s[...] = jnp.zeros((N_SLABS, T, LANES), f32)

    if is_prompt:
        @pl.when(t_in_seq == 0)
        def _():
            zb_s[0, :, 0:HIST, :] = jnp.zeros((N_SLABS, HIST, LANES), f32)

    def norm2_block(i):
        s = i * ROW_BLK // L
        rows = slice(i * ROW_BLK, (i + 1) * ROW_BLK)
        hh = _rms(x1_s[rows, :], g2_ref[...]) * (1.0 + mod_ref[grp_c, s, 4:5, :]) + mod_ref[grp_c, s, 3:4, :]
        h2_s[rows, :] = hh.astype(bf16)

    def norm1_block(i):
        s, r = divmod(i * ROW_BLK, L)
        rows = slice(i * ROW_BLK, (i + 1) * ROW_BLK)
        hh = _rms(xa_ref[0, s, r:r + ROW_BLK, :], g1_ref[...]) * (1.0 + mod_ref[grp_a, s, 1:2, :]) + mod_ref[grp_a, s, 0:1, :]
        h_s[rows, :] = hh.astype(bf16)

    def glu_piece(c):
        hv = h_s[...]
        za = _mm(hv, win_ref, O_GLU + c * COL_BLK, COL_BLK)
        zg = _mm(hv, win_ref, O_GLU + D_MODEL + c * COL_BLK, COL_BLK)
        zc = za * _sigmoid(zg)
        for jj in range(COL_BLK // LANES):
            j = c * (COL_BLK // LANES) + jj
            z = zc[:, jj * LANES:(jj + 1) * LANES]
            for s in range(nb):
                zb_s[s, j, HIST:HIST + L, :] = z[s * L:(s + 1) * L]
                if not is_prompt:
                    nc_ref[0, s, :, j * LANES:(j + 1) * LANES] = z[s * L + L - (CONV_W - 1):(s + 1) * L]

    def gb_piece(c):
        gb_s[:, _cols(c)] = _sigmoid(_mm(h_s[...], win_ref, O_GB + c * COL_BLK, COL_BLK))

    def up_piece(c):
        h2v = h2_s[...]
        gate = _mm(h2v, wup_ref, c * COL_BLK, COL_BLK)
        up = _mm(h2v, wup_ref, D_FF + c * COL_BLK, COL_BLK)
        act_s[c % 2] = (gate * _sigmoid(gate) * up).astype(bf16)

    def down_piece(c):
        d_s[...] += jnp.dot(act_s[c % 2], wdn_ref[c * COL_BLK:(c + 1) * COL_BLK, :], preferred_element_type=f32)

    def ln_block(i):
        rows = slice(i * ROW_BLK, (i + 1) * ROW_BLK)
        ys = [cb_s[j, rows, :] + dwb_ref[:, j * LANES:(j + 1) * LANES] for j in range(N_SLABS)]
        tot = ys[0]
        for j in range(1, N_SLABS):
            tot = tot + ys[j]
        mu = jnp.sum(tot, axis=-1, keepdims=True) * (1.0 / D_MODEL)
        ds = [y - mu for y in ys]
        sq = ds[0] * ds[0]
        for j in range(1, N_SLABS):
            sq = sq + ds[j] * ds[j]
        var = jnp.sum(sq, axis=-1, keepdims=True) * (1.0 / D_MODEL)
        rstd = lax.rsqrt(var + EPS)
        for j in range(N_SLABS):
            y = ds[j] * rstd * lng_ref[:, j * LANES:(j + 1) * LANES] + lnb_ref[:, j * LANES:(j + 1) * LANES]
            cbf_s[rows, j * LANES:(j + 1) * LANES] = (y * _sigmoid(y)).astype(bf16)

    def yb_piece(c):
        yb = _mm(cbf_s[...], wco_ref, c * COL_BLK, COL_BLK)
        for s in range(nb):
            rows = slice(s * L, (s + 1) * L)
            acc_s[rows, _cols(c)] = a1_ref[0, s, :, _cols(c)] + gb_s[rows, _cols(c)] * yb[rows]

    def final_block(i):
        s, r = divmod(i * ROW_BLK, L)
        rows = slice(i * ROW_BLK, (i + 1) * ROW_BLK)
        x2 = x1_s[rows, :] + mod_ref[grp_c, s, 5:6, :] * d_s[rows, :]
        y_ref[0, s, r:r + ROW_BLK, :] = _rms(x2, fg_ref[...])

    def out_piece(c):
        m = _mm(acc_s[...].astype(bf16), wo_ref, c * COL_BLK, COL_BLK)
        for s in range(nb):
            rows = slice(s * L, (s + 1) * L)
            x1_s[rows, _cols(c)] = xb_ref[0, s, :, _cols(c)] + mod_ref[grp_b, s, 2:3, _cols(c)] * m[rows]

    n_blk = T // ROW_BLK
    _interleave([functools.partial(gb_piece, c) for c in range(N_CC)],
                [functools.partial(ln_block, i) for i in range(n_blk)])
    for i in range(n_blk):
        norm1_block(i)
    if not is_prompt:
        for s in range(nb):
            for j in range(N_SLABS):
                zb_s[s, j, HIST - (CONV_W - 1):HIST, :] = cc_ref[0, s, :, j * LANES:(j + 1) * LANES]
    _interleave([functools.partial(glu_piece, c) for c in range(N_CC)],
                [functools.partial(norm2_block, i) for i in range(n_blk)])
    d_s[...] = jnp.zeros((T, D_MODEL), f32)
    up_piece(0)
    pieces = []
    for c in range(N_FF):
        if c + 1 < N_FF:
            pieces.append(functools.partial(up_piece, c + 1))
        pieces.append(functools.partial(down_piece, c))
    pieces += [functools.partial(yb_piece, c) for c in range(N_CC)]
    conv_blocks = [functools.partial(_conv_block, zb_s, s, j, i, dww_ref, cb_s, s * L)
                   for j in range(N_SLABS) for s in range(nb) for i in range(L // 32)]
    _interleave(pieces, conv_blocks)
    for i in range(n_blk):
        final_block(i)
    for c in range(N_CC):
        out_piece(c)

    if is_prompt:
        @pl.when(t_in_seq == tiles_per_seq - 1)
        def _():
            for j in range(N_SLABS):
                nc_ref[0, 0, :, j * LANES:(j + 1) * LANES] = zb_s[0, j, HIST + L - (CONV_W - 1):HIST + L, :]

        zb_s[0, :, 0:HIST, :] = zb_s[0, :, L:L + HIST, :]


def _mix_call(x, a1, mod, cc, weights, nb, L, is_prompt, name):
    n_tiles = x.shape[0]
    tiles_per_seq = n_tiles // mod.shape[0]
    T = nb * L

    def lagged(lag, per_group):
        def index_map(i):
            t = jnp.clip(i - lag, 0, n_tiles - 1)
            return (t // per_group, 0, 0, 0)
        return index_map

    body = functools.partial(_mix_body, is_prompt=is_prompt, nb=nb, L=L, n_tiles=n_tiles,
                             tiles_per_seq=tiles_per_seq)
    row = _const_spec((1, D_MODEL))
    tile_blk = (1, nb, L, D_MODEL)
    in_specs = [pl.BlockSpec(tile_blk, lagged(0, 1)),
                pl.BlockSpec(tile_blk, lagged(1, 1)),
                pl.BlockSpec(tile_blk, lagged(1, 1)),
                _const_spec(mod.shape)]
    args = [x, x, a1, mod]
    if not is_prompt:
        in_specs.append(pl.BlockSpec((1, nb, CONV_W - 1, D_MODEL), lagged(0, 1)))
        args.append(cc)
    in_specs += [
        row,
        _const_spec((D_MODEL, IN_W)),
        _const_spec((CONV_W, D_MODEL)),
        row, row, row,
        _const_spec((D_MODEL, D_MODEL)),
        _const_spec((D_MODEL, D_MODEL)),
        row, row,
        _const_spec((D_MODEL, 2 * D_FF)),
        _const_spec((D_FF, D_MODEL)),
    ]
    n_groups = mod.shape[0]
    return pl.pallas_call(
        body,
        grid=(n_tiles + 2,),
        in_specs=in_specs,
        out_specs=[pl.BlockSpec(tile_blk, lagged(2, 1)),
                   pl.BlockSpec((1, nb, CONV_W - 1, D_MODEL), lagged(0, tiles_per_seq))],
        out_shape=[jax.ShapeDtypeStruct(x.shape, f32),
                   jax.ShapeDtypeStruct((n_groups, nb, CONV_W - 1, D_MODEL), f32)],
        scratch_shapes=[
            pltpu.VMEM((T, D_MODEL), bf16),
            pltpu.VMEM((nb, N_SLABS, HIST + L, LANES), f32),
            pltpu.VMEM((N_SLABS, T, LANES), f32),
            pltpu.VMEM((T, D_MODEL), bf16),
            pltpu.VMEM((T, D_MODEL), f32),
            pltpu.VMEM((T, D_MODEL), f32),
            pltpu.VMEM((T, D_MODEL), f32),
            pltpu.VMEM((T, D_MODEL), bf16),
            pltpu.VMEM((2, T, COL_BLK), bf16),
            pltpu.VMEM((T, D_MODEL), f32),
        ],
        compiler_params=pltpu.CompilerParams(
            dimension_semantics=("arbitrary",), vmem_limit_bytes=VMEM_LIMIT),
        name=name,
    )(*args, *weights)


def kernel(x_prompt, x_sample, cache_k, cache_v, cache_conv, c_prompt, c_sample, rel_table, w_ada, b_ada, norm1_g, norm2_g, w_in, sink, w_attn_out, dw_w, dw_b, conv_ln_g, conv_ln_b, w_conv_out, w_out, w_ffn_up, w_ffn_down, final_g):
    assert w_ada.shape[0] == 1, "single-layer kernel"
    B, S, _ = x_prompt.shape
    DB, DS, _ = x_sample.shape
    nb_attn = ATTN_SAMPLE_TOKENS // DS
    nb_mix = MIX_TOKENS // DS

    bias = _bias_table(rel_table)
    n_seq = B + DB
    pad = (-n_seq) % 8
    c_all = jnp.concatenate([c_prompt, c_sample, jnp.zeros((pad, D_MODEL), f32)], axis=0)
    mod = _modulation(c_all, w_ada[0], b_ada).reshape(n_seq + pad, 6, D_MODEL)
    mod_p, mod_s = mod[:B], mod[B:B + DB]

    row = lambda v: v.reshape(1, D_MODEL)
    w = w_in[0]
    w_k = w[:, O_K:O_V].astype(bf16).reshape(D_MODEL, KV_HEADS, 1, HEAD_DIM)
    w_kd = jnp.broadcast_to(w_k, (D_MODEL, KV_HEADS, 2, HEAD_DIM)).reshape(D_MODEL, KV_HEADS * LANES)
    w_in_b = w.astype(bf16)
    attn_w = (bias, sink[0], row(norm1_g[0]), w_in_b, w_kd, w[:, O_V:O_GLU].T.astype(bf16),
              w_attn_out[0].astype(bf16))
    mix_w = (row(norm1_g[0]), w_in_b, dw_w[0], row(dw_b[0]),
             row(conv_ln_g[0]), row(conv_ln_b[0]), w_conv_out[0].astype(bf16), w_out[0].astype(bf16),
             row(norm2_g[0]), row(final_g), w_ffn_up[0].astype(bf16), w_ffn_down[0].astype(bf16))

    a1p, nkp, nvp = _attn_call(x_prompt, mod_p, (), attn_w, 1, ATTN_TOKENS, True, "attn_prompt")
    ck = cache_k[0].reshape(DB, WINDOW, KV_W)
    cv = cache_v[0].reshape(DB, WINDOW, KV_W)
    a1s, nks, nvs = _attn_call(x_sample, mod_s, (ck, cv), attn_w, nb_attn, DS, False, "attn_sample")

    tiles = lambda a, n, l: a.reshape(-1, n, l, D_MODEL)
    y_p, ncp = _mix_call(tiles(x_prompt, 1, MIX_TOKENS), tiles(a1p, 1, MIX_TOKENS), mod_p[:, None], None,
                         mix_w, 1, MIX_TOKENS, True, "mix_prompt")
    y_s, ncs = _mix_call(tiles(x_sample, nb_mix, DS), tiles(a1s, nb_mix, DS), tiles(mod_s, nb_mix, 6),
                         tiles(cache_conv[0], nb_mix, CONV_W - 1), mix_w, nb_mix, DS, False, "mix_sample")

    kv5 = lambda a: a.reshape(1, a.shape[0], WINDOW, KV_HEADS, HEAD_DIM)
    return (y_p.reshape(B, S, D_MODEL), y_s.reshape(DB, DS, D_MODEL), kv5(nkp), kv5(nvp),
            ncp.reshape(1, B, CONV_W - 1, D_MODEL), kv5(nks), kv5(nvs), ncs.reshape(1, DB, CONV_W - 1, D_MODEL))
```

```python
import functools
import math

import jax
import jax.numpy as jnp
from jax import lax
from jax.experimental import pallas as pl
from jax.experimental.pallas import tpu as pltpu

f32 = jnp.float32
bf16 = jnp.bfloat16

D_MODEL = 1024
N_HEADS = 16
KV_HEADS = 4
HEAD_DIM = 64
GROUP = N_HEADS // KV_HEADS
WINDOW = 128
CHUNK = 64
BAND = WINDOW + CHUNK
CONV_W = 31
D_FF = 2816
NUM_BUCKETS = 32
MAX_DISTANCE = 128
EPS = 1e-6
NEG = -1e30
Q_W = N_HEADS * HEAD_DIM
KV_W = KV_HEADS * HEAD_DIM
O_K = Q_W
O_V = O_K + KV_W
O_GLU = O_V + KV_W
O_GA = O_GLU + 2 * D_MODEL
O_GB = O_GA + D_MODEL
IN_W = O_GB + D_MODEL

LANES = 128
N_SLABS = D_MODEL // LANES
PAIR = 2 * CHUNK
GQ = GROUP * CHUNK
HIST = 32
ROW_BLK = 32
COL_BLK = 256
N_CC = D_MODEL // COL_BLK
N_FF = D_FF // COL_BLK
VMEM_LIMIT = 56 * 1024 * 1024
ATTN_TOKENS = 1024
ATTN_SAMPLE_TOKENS = 512
MIX_TOKENS = 256

_NT = (((1,), (1,)), ((), ()))


def _sigmoid(x):
    return 1.0 / (1.0 + jnp.exp(-x))


def _rms(x, g):
    ms = jnp.mean(x * x, axis=-1, keepdims=True)
    return x * lax.rsqrt(ms + EPS) * g


def _mm(a, w_ref, c0, width):
    return jnp.dot(a, w_ref[:, c0:c0 + width], preferred_element_type=f32)


def _cols(c):
    return slice(c * COL_BLK, (c + 1) * COL_BLK)


def _head_row(vals, k, lane):
    row = jnp.full((1, GQ), vals(GROUP * k + GROUP - 1), f32)
    for g in range(GROUP - 2, -1, -1):
        row = jnp.where(lane < CHUNK * (g + 1), vals(GROUP * k + g), row)
    return row


def _interleave(pieces, blocks, blocks_first=False):
    order = [((n + 0.5) / len(pieces), int(blocks_first), fn) for n, fn in enumerate(pieces)]
    order += [((n + 0.5) / len(blocks), int(not blocks_first), fn) for n, fn in enumerate(blocks)]
    for _, _, fn in sorted(order, key=lambda item: item[:2]):
        fn()


def _const_spec(shape):
    nd = len(shape)
    return pl.BlockSpec(shape, lambda *_: (0,) * nd, pipeline_mode=pl.Buffered(1))


def _rel_bucket(rel):
    nb = NUM_BUCKETS // 2
    max_exact = nb // 2
    ret = (rel > 0).astype(jnp.int32) * nb
    n = jnp.abs(rel)
    nf = jnp.maximum(n, 1).astype(f32)
    large = max_exact + (jnp.log(nf / max_exact) / math.log(MAX_DISTANCE / max_exact)
                         * (nb - max_exact)).astype(jnp.int32)
    large = jnp.minimum(large, nb - 1)
    return ret + jnp.where(n < max_exact, n, large)


def _bias_body(idx_ref, tab_ref, o_ref):
    idx = idx_ref[...]
    lane = lax.broadcasted_iota(jnp.int32, (1, GQ), 1)
    for k in range(KV_HEADS):
        acc = jnp.zeros((BAND, GQ), f32)
        for b in range(NUM_BUCKETS):
            acc = jnp.where(idx == b, _head_row(lambda h: tab_ref[b, h], k, lane), acc)
        o_ref[k] = acc


def _bias_table(rel_table):
    kj = jnp.arange(BAND, dtype=jnp.int32)
    rel = kj[None, :] - WINDOW - jnp.arange(CHUNK, dtype=jnp.int32)[:, None]
    idx_t = jnp.tile(_rel_bucket(rel).T, (1, GROUP))
    return pl.pallas_call(
        _bias_body,
        in_specs=[pl.BlockSpec(memory_space=pltpu.VMEM), pl.BlockSpec(memory_space=pltpu.SMEM)],
        out_specs=pl.BlockSpec(memory_space=pltpu.VMEM),
        out_shape=jax.ShapeDtypeStruct((KV_HEADS, BAND, GQ), f32),
        name="rel_bias",
    )(idx_t, rel_table)


def _mod_body(c_ref, w_ref, b_ref, o_ref):
    c = c_ref[...]
    s = c * _sigmoid(c)
    o_ref[...] = jnp.dot(s.astype(bf16), w_ref[...].astype(bf16), preferred_element_type=f32) + b_ref[...]


def _modulation(c_all, w_ada, b_ada):
    rows = c_all.shape[0]
    blk = 2048
    return pl.pallas_call(
        _mod_body,
        grid=(6 * D_MODEL // blk,),
        in_specs=[pl.BlockSpec((rows, D_MODEL), lambda j: (0, 0)),
                  pl.BlockSpec((D_MODEL, blk), lambda j: (0, j)),
                  pl.BlockSpec((1, blk), lambda j: (0, j))],
        out_specs=pl.BlockSpec((rows, blk), lambda j: (0, j)),
        out_shape=jax.ShapeDtypeStruct((rows, 6 * D_MODEL), f32),
        name="adaln_mod",
    )(c_all, w_ada, b_ada)


def _attn_pair(chunks, bias_ref, sink_rows, st_s, pb_s, slot0, attT_ref, grp):
    lane_lo = lax.broadcasted_iota(jnp.int32, (CHUNK, LANES), 1) < HEAD_DIM
    n_pad = 2 * LANES - BAND
    rden = {}
    for par, (q_c, kd_bands, v_win, pad_top, valid) in enumerate(chunks):
        for k in range(KV_HEADS):
            blocks = []
            for p in range(2):
                qc = q_c[:, (2 * k + p) * LANES:(2 * k + p + 1) * LANES]
                blocks.append(jnp.where(lane_lo, qc, jnp.zeros_like(qc)))
                blocks.append(jnp.where(lane_lo, jnp.zeros_like(qc), qc))
            qm = jnp.concatenate(blocks, axis=0)
            st_s[slot0 + par * KV_HEADS + k] = lax.dot_general(kd_bands[k], qm, _NT, preferred_element_type=f32)
    for par, (q_c, kd_bands, v_win, pad_top, valid) in enumerate(chunks):
        for k in range(KV_HEADS):
            i = slot0 + par * KV_HEADS + k
            st = st_s[i] + bias_ref[k]
            if valid is not None:
                st = jnp.where(valid, st, NEG)
            m = jnp.maximum(jnp.max(st, axis=0, keepdims=True), sink_rows[k])
            p_ = jnp.exp(st - m)
            den = jnp.sum(p_, axis=0, keepdims=True) + jnp.exp(sink_rows[k] - m)
            r0 = n_pad if pad_top else 0
            pb_s[i, r0:r0 + BAND, :] = p_.astype(bf16)
            rden[i] = 1.0 / den
            z0 = 0 if pad_top else BAND
            pb_s[i, z0:z0 + n_pad, :] = jnp.zeros((n_pad, GQ), bf16)
    for par, (q_c, kd_bands, v_win, pad_top, valid) in enumerate(chunks):
        for k in range(KV_HEADS):
            i = slot0 + par * KV_HEADS + k
            ot = jnp.dot(v_win[k * HEAD_DIM:(k + 1) * HEAD_DIM, :], pb_s[i], preferred_element_type=f32) * rden[i]
            for g in range(GROUP):
                h = GROUP * k + g
                attT_ref[grp, h * HEAD_DIM:(h + 1) * HEAD_DIM, par * CHUNK:(par + 1) * CHUNK] = (
                    ot[:, g * CHUNK:(g + 1) * CHUNK])


def _attn_body(*refs, is_prompt, nb, L, n_tiles):
    if is_prompt:
        (x_ref, mod_ref, bias_ref, sink_ref, g1_ref, win_ref, wkd_ref, wvt_ref, wao_ref,
         a1_ref, nk_ref, nv_ref,
         h_s, q_s, kd_s, vt_s, st_s, pb_s, attT_s, att_s, ga_s) = refs
        ck_ref = cv_ref = None
    else:
        (x_ref, mod_ref, ck_ref, cv_ref, bias_ref, sink_ref, g1_ref, win_ref, wkd_ref, wvt_ref, wao_ref,
         a1_ref, nk_ref, nv_ref,
         h_s, q_s, kd_s, vt_s, st_s, pb_s, attT_s, att_s, ga_s) = refs
    T = nb * L
    n_grp = T // PAIR

    if is_prompt:
        t = pl.program_id(1)

        @pl.when(t == 0)
        def _():
            kd_s[:, 0:WINDOW, :] = jnp.zeros((KV_HEADS, WINDOW, LANES), bf16)
            vt_s[0] = jnp.zeros((KV_W, LANES), bf16)

    for s in range(nb):
        for i in range(L // ROW_BLK):
            rows = slice(i * ROW_BLK, (i + 1) * ROW_BLK)
            hh = _rms(x_ref[s, rows, :], g1_ref[...]) * (1.0 + mod_ref[s, 1:2, :]) + mod_ref[s, 0:1, :]
            h_s[s * L + i * ROW_BLK:s * L + (i + 1) * ROW_BLK, :] = hh.astype(bf16)
    hv = h_s[...]

    if not is_prompt:
        for s in range(nb):
            ck = ck_ref[s]
            for k in range(KV_HEADS):
                ckk = ck[:, k * HEAD_DIM:(k + 1) * HEAD_DIM].astype(bf16)
                kd_s[s, k, 0:WINDOW, :] = jnp.concatenate([ckk, ckk], axis=1)
            vt_s[s, :, 0:WINDOW] = cv_ref[s].T.astype(bf16)
            vt_s[s, :, BAND:2 * LANES] = jnp.zeros((KV_W, 2 * LANES - BAND), bf16)
            nk_ref[s, 0:WINDOW - L, :] = ck[L:WINDOW, :]
            nv_ref[s, 0:WINDOW - L, :] = cv_ref[s, L:WINDOW, :]
        kvf = _mm(hv, win_ref, O_K, 2 * KV_W)
        for s in range(nb):
            nk_ref[s, WINDOW - L:WINDOW, :] = kvf[s * L:(s + 1) * L, 0:KV_W]
            nv_ref[s, WINDOW - L:WINDOW, :] = kvf[s * L:(s + 1) * L, KV_W:2 * KV_W]

    for c in range(Q_W // COL_BLK):
        qf = _mm(hv, win_ref, c * COL_BLK, COL_BLK)
        q_s[:, _cols(c)] = (qf * (HEAD_DIM ** -0.5)).astype(bf16)
    for c in range(KV_HEADS * LANES // COL_BLK):
        kdc = _mm(hv, wkd_ref, c * COL_BLK, COL_BLK)
        for kk in range(COL_BLK // LANES):
            k = c * (COL_BLK // LANES) + kk
            kdk = kdc[:, kk * LANES:(kk + 1) * LANES].astype(bf16)
            if is_prompt:
                kd_s[k, WINDOW:WINDOW + T, :] = kdk
            else:
                for s in range(nb):
                    kd_s[s, k, WINDOW:BAND, :] = kdk[s * L:(s + 1) * L]
    vtf = lax.dot_general(wvt_ref[...], hv, _NT, preferred_element_type=f32).astype(bf16)
    if is_prompt:
        for g in range(n_grp):
            vt_s[1 + g] = vtf[:, g * LANES:(g + 1) * LANES]
    else:
        for s in range(nb):
            vt_s[s, :, WINDOW:BAND] = vtf[:, s * L:(s + 1) * L]

    lane_gq = lax.broadcasted_iota(jnp.int32, (1, GQ), 1)
    sink_rows = [_head_row(lambda h: sink_ref[h], k, lane_gq) for k in range(KV_HEADS)]

    def attn_block(c2):
        chunks = []
        if is_prompt:
            v_win = jnp.concatenate([vt_s[c2], vt_s[c2 + 1]], axis=1)
            for par in range(2):
                r = c2 * PAIR + par * CHUNK
                pos = t * T + r - WINDOW + lax.broadcasted_iota(jnp.int32, (BAND, 1), 0)
                chunks.append((q_s[r:r + CHUNK, :],
                               [kd_s[k, r:r + BAND, :] for k in range(KV_HEADS)],
                               v_win, par == 1, pos >= 0 if r < WINDOW else None))
        else:
            for par in range(2):
                s = 2 * c2 + par
                chunks.append((q_s[s * CHUNK:(s + 1) * CHUNK, :],
                               [kd_s[s, k] for k in range(KV_HEADS)],
                               vt_s[s], False, None))
        _attn_pair(chunks, bias_ref, sink_rows, st_s, pb_s, (c2 % 2) * 2 * KV_HEADS, attT_s, c2)
        att_s[c2 * PAIR:(c2 + 1) * PAIR, :] = attT_s[c2].T.astype(bf16)

    def ga_piece(c):
        ga_s[:, _cols(c)] = _sigmoid(_mm(hv, win_ref, O_GA + c * COL_BLK, COL_BLK))

    _interleave([functools.partial(ga_piece, c) for c in range(N_CC)],
                [functools.partial(attn_block, c2) for c2 in range(n_grp)], blocks_first=True)

    av = att_s[...]
    for c in range(N_CC):
        a1c = ga_s[:, _cols(c)] * _mm(av, wao_ref, c * COL_BLK, COL_BLK)
        for s in range(nb):
            a1_ref[s, :, _cols(c)] = a1c[s * L:(s + 1) * L]

    if is_prompt:
        @pl.when(t == n_tiles - 1)
        def _():
            kvf = _mm(h_s[T - WINDOW:T, :], win_ref, O_K, 2 * KV_W)
            nk_ref[0] = kvf[:, 0:KV_W]
            nv_ref[0] = kvf[:, KV_W:2 * KV_W]

        kd_s[:, 0:WINDOW, :] = kd_s[:, T:T + WINDOW, :]
        vt_s[0] = vt_s[n_grp]


def _attn_call(x, mod, caches, weights, nb, L, is_prompt, name):
    B, S, _ = x.shape
    T = nb * L
    if is_prompt:
        n_tiles = S // L
        grid = (B, n_tiles)
        xmap = lambda b, t: (b, t, 0)
        smap = lambda b, t: (b, 0, 0)
        kd_shape = (KV_HEADS, WINDOW + T, LANES)
        vt_shape = (1 + T // PAIR, KV_W, LANES)
    else:
        n_tiles = B // nb
        grid = (n_tiles,)
        xmap = smap = lambda i: (i, 0, 0)
        kd_shape = (nb, KV_HEADS, BAND, LANES)
        vt_shape = (nb, KV_W, 2 * LANES)
    body = functools.partial(_attn_body, is_prompt=is_prompt, nb=nb, L=L, n_tiles=n_tiles)
    cache_specs = [pl.BlockSpec((nb, WINDOW, KV_W), smap)] * len(caches)
    weight_specs = [
        _const_spec((KV_HEADS, BAND, GQ)),
        pl.BlockSpec(memory_space=pltpu.SMEM),
        _const_spec((1, D_MODEL)),
        _const_spec((D_MODEL, IN_W)),
        _const_spec((D_MODEL, KV_HEADS * LANES)),
        _const_spec((KV_W, D_MODEL)),
        _const_spec((Q_W, D_MODEL)),
    ]
    return pl.pallas_call(
        body,
        grid=grid,
        in_specs=[pl.BlockSpec((nb, L, D_MODEL), xmap), pl.BlockSpec((nb, 6, D_MODEL), smap)]
        + cache_specs + weight_specs,
        out_specs=[pl.BlockSpec((nb, L, D_MODEL), xmap),
                   pl.BlockSpec((nb, WINDOW, KV_W), smap),
                   pl.BlockSpec((nb, WINDOW, KV_W), smap)],
        out_shape=[jax.ShapeDtypeStruct((B, S, D_MODEL), f32),
                   jax.ShapeDtypeStruct((B, WINDOW, KV_W), f32),
                   jax.ShapeDtypeStruct((B, WINDOW, KV_W), f32)],
        scratch_shapes=[
            pltpu.VMEM((T, D_MODEL), bf16),
            pltpu.VMEM((T, Q_W), bf16),
            pltpu.VMEM(kd_shape, bf16),
            pltpu.VMEM(vt_shape, bf16),
            pltpu.VMEM((4 * KV_HEADS, BAND, GQ), f32),
            pltpu.VMEM((4 * KV_HEADS, 2 * LANES, GQ), bf16),
            pltpu.VMEM((T // PAIR, Q_W, PAIR), f32),
            pltpu.VMEM((T, Q_W), bf16),
            pltpu.VMEM((T, D_MODEL), f32),
        ],
        compiler_params=pltpu.CompilerParams(
            dimension_semantics=("arbitrary",) * len(grid), vmem_limit_bytes=VMEM_LIMIT),
        name=name,
    )(x, mod, *caches, *weights)


def _conv_block(zb_ref, s, j, i, dww_ref, cb_ref, out_row0):
    r0 = i * 32
    accs = [jnp.zeros((8, LANES), f32) for _ in range(4)]
    for k in range(CONV_W):
        wk = dww_ref[k:k + 1, j * LANES:(j + 1) * LANES]
        for a in range(4):
            start = r0 + (a // 2) * 16 + (a % 2) + (HIST - (CONV_W - 1)) + k
            accs[a] = accs[a] + zb_ref[s, j, pl.ds(start, 8, stride=2), :] * wk
    for a in range(4):
        start = out_row0 + r0 + (a // 2) * 16 + (a % 2)
        cb_ref[j, pl.ds(start, 8, stride=2), :] = accs[a]


def _mix_body(*refs, is_prompt, nb, L, n_tiles, tiles_per_seq):
    if is_prompt:
        (xa_ref, xb_ref, a1_ref, mod_ref, g1_ref, win_ref, dww_ref, dwb_ref,
         lng_ref, lnb_ref, wco_ref, wo_ref, g2_ref, fg_ref, wup_ref, wdn_ref,
         y_ref, nc_ref,
         h_s, zb_s, cb_s, cbf_s, gb_s, acc_s, x1_s, h2_s, act_s, d_s) = refs
        cc_ref = None
    else:
        (xa_ref, xb_ref, a1_ref, mod_ref, cc_ref, g1_ref, win_ref, dww_ref, dwb_ref,
         lng_ref, lnb_ref, wco_ref, wo_ref, g2_ref, fg_ref, wup_ref, wdn_ref,
         y_ref, nc_ref,
         h_s, zb_s, cb_s, cbf_s, gb_s, acc_s, x1_s, h2_s, act_s, d_s) = refs
    T = nb * L
    step = pl.program_id(0)
    t_in_seq = jnp.minimum(step, n_tiles - 1) % tiles_per_seq
    grp_a, grp_b, grp_c = [jnp.clip(step - lag, 0, n_tiles - 1) // tiles_per_seq for lag in range(3)]

    @pl.when(step == 0)
    def _():
        x1_s[...] = jnp.zeros((T, D_MODEL), f32)
        h_s[...] = jnp.zeros((T, D_MODEL), bf16)
        cb_s[...] = jnp.zeros((N_SLABS, T, LANES), f32)

    if is_prompt:
        @pl.when(t_in_seq == 0)
        def _():
            zb_s[0, :, 0:HIST, :] = jnp.zeros((N_SLABS, HIST, LANES), f32)

    def norm2_block(i):
        s = i * ROW_BLK // L
        rows = slice(i * ROW_BLK, (i + 1) * ROW_BLK)
        hh = _rms(x1_s[rows, :], g2_ref[...]) * (1.0 + mod_ref[grp_c, s, 4:5, :]) + mod_ref[grp_c, s, 3:4, :]
        h2_s[rows, :] = hh.astype(bf16)

    def norm1_block(i):
        s, r = divmod(i * ROW_BLK, L)
        rows = slice(i * ROW_BLK, (i + 1) * ROW_BLK)
        hh = _rms(xa_ref[0, s, r:r + ROW_BLK, :], g1_ref[...]) * (1.0 + mod_ref[grp_a, s, 1:2, :]) + mod_ref[grp_a, s, 0:1, :]
        h_s[rows, :] = hh.astype(bf16)

    def glu_piece(c):
        hv = h_s[...]
        za = _mm(hv, win_ref, O_GLU + c * COL_BLK, COL_BLK)
        zg = _mm(hv, win_ref, O_GLU + D_MODEL + c * COL_BLK, COL_BLK)
        zc = za * _sigmoid(zg)
        for jj in range(COL_BLK // LANES):
            j = c * (COL_BLK // LANES) + jj
            z = zc[:, jj * LANES:(jj + 1) * LANES]
            for s in range(nb):
                zb_s[s, j, HIST:HIST + L, :] = z[s * L:(s + 1) * L]
                if not is_prompt:
                    nc_ref[0, s, :, j * LANES:(j + 1) * LANES] = z[s * L + L - (CONV_W - 1):(s + 1) * L]

    def gb_piece(c):
        gb_s[:, _cols(c)] = _sigmoid(_mm(h_s[...], win_ref, O_GB + c * COL_BLK, COL_BLK))

    def up_piece(c):
        h2v = h2_s[...]
        gate = _mm(h2v, wup_ref, c * COL_BLK, COL_BLK)
        up = _mm(h2v, wup_ref, D_FF + c * COL_BLK, COL_BLK)
        act_s[c % 2] = (gate * _sigmoid(gate) * up).astype(bf16)

    def down_piece(c):
        d_s[...] += jnp.dot(act_s[c % 2], wdn_ref[c * COL_BLK:(c + 1) * COL_BLK, :], preferred_element_type=f32)

    def ln_block(i):
        rows = slice(i * ROW_BLK, (i + 1) * ROW_BLK)
        ys = [cb_s[j, rows, :] + dwb_ref[:, j * LANES:(j + 1) * LANES] for j in range(N_SLABS)]
        tot = ys[0]
        for j in range(1, N_SLABS):
            tot = tot + ys[j]
        mu = jnp.sum(tot, axis=-1, keepdims=True) * (1.0 / D_MODEL)
        ds = [y - mu for y in ys]
        sq = ds[0] * ds[0]
        for j in range(1, N_SLABS):
            sq = sq + ds[j] * ds[j]
        var = jnp.sum(sq, axis=-1, keepdims=True) * (1.0 / D_MODEL)
        rstd = lax.rsqrt(var + EPS)
        for j in range(N_SLABS):
            y = ds[j] * rstd * lng_ref[:, j * LANES:(j + 1) * LANES] + lnb_ref[:, j * LANES:(j + 1) * LANES]
            cbf_s[rows, j * LANES:(j + 1) * LANES] = (y * _sigmoid(y)).astype(bf16)

    def yb_piece(c):
        yb = _mm(cbf_s[...], wco_ref, c * COL_BLK, COL_BLK)
        for s in range(nb):
            rows = slice(s * L, (s + 1) * L)
            acc_s[rows, _cols(c)] = a1_ref[0, s, :, _cols(c)] + gb_s[rows, _cols(c)] * yb[rows]

    def final_block(i):
        s, r = divmod(i * ROW_BLK, L)
        rows = slice(i * ROW_BLK, (i + 1) * ROW_BLK)
        x2 = x1_s[rows, :] + mod_ref[grp_c, s, 5:6, :] * d_s[rows, :]
        y_ref[0, s, r:r + ROW_BLK, :] = _rms(x2, fg_ref[...])

    def out_piece(c):
        m = _mm(acc_s[...].astype(bf16), wo_ref, c * COL_BLK, COL_BLK)
        for s in range(nb):
            rows = slice(s * L, (s + 1) * L)
            x1_s[rows, _cols(c)] = xb_ref[0, s, :, _cols(c)] + mod_ref[grp_b, s, 2:3, _cols(c)] * m[rows]

    n_blk = T // ROW_BLK
    _interleave([functools.partial(gb_piece, c) for c in range(N_CC)],
                [functools.partial(ln_block, i) for i in range(n_blk)])
    for i in range(n_blk):
        norm1_block(i)
    if not is_prompt:
        for s in range(nb):
            for j in range(N_SLABS):
                zb_s[s, j, HIST - (CONV_W - 1):HIST, :] = cc_ref[0, s, :, j * LANES:(j + 1) * LANES]
    _interleave([functools.partial(glu_piece, c) for c in range(N_CC)],
                [functools.partial(norm2_block, i) for i in range(n_blk)])
    d_s[...] = jnp.zeros((T, D_MODEL), f32)
    up_piece(0)
    pieces = []
    for c in range(N_FF):
        if c + 1 < N_FF:
            pieces.append(functools.partial(up_piece, c + 1))
        pieces.append(functools.partial(down_piece, c))
    pieces += [functools.partial(yb_piece, c) for c in range(N_CC)]
    conv_blocks = [functools.partial(_conv_block, zb_s, s, j, i, dww_ref, cb_s, s * L)
                   for j in range(N_SLABS) for s in range(nb) for i in range(L // 32)]
    _interleave(pieces, conv_blocks)
    for i in range(n_blk):
        final_block(i)
    for c in range(N_CC):
        out_piece(c)

    if is_prompt:
        @pl.when(t_in_seq == tiles_per_seq - 1)
        def _():
            for j in range(N_SLABS):
                nc_ref[0, 0, :, j * LANES:(j + 1) * LANES] = zb_s[0, j, HIST + L - (CONV_W - 1):HIST + L, :]

        zb_s[0, :, 0:HIST, :] = zb_s[0, :, L:L + HIST, :]


def _mix_call(x, a1, mod, cc, weights, nb, L, is_prompt, name):
    n_tiles = x.shape[0]
    tiles_per_seq = n_tiles // mod.shape[0]
    T = nb * L

    def lagged(lag, per_group):
        def index_map(i):
            t = jnp.clip(i - lag, 0, n_tiles - 1)
            return (t // per_group, 0, 0, 0)
        return index_map

    body = functools.partial(_mix_body, is_prompt=is_prompt, nb=nb, L=L, n_tiles=n_tiles,
                             tiles_per_seq=tiles_per_seq)
    row = _const_spec((1, D_MODEL))
    tile_blk = (1, nb, L, D_MODEL)
    in_specs = [pl.BlockSpec(tile_blk, lagged(0, 1)),
                pl.BlockSpec(tile_blk, lagged(1, 1)),
                pl.BlockSpec(tile_blk, lagged(1, 1)),
                _const_spec(mod.shape)]
    args = [x, x, a1, mod]
    if not is_prompt:
        in_specs.append(pl.BlockSpec((1, nb, CONV_W - 1, D_MODEL), lagged(0, 1)))
        args.append(cc)
    in_specs += [
        row,
        _const_spec((D_MODEL, IN_W)),
        _const_spec((CONV_W, D_MODEL)),
        row, row, row,
        _const_spec((D_MODEL, D_MODEL)),
        _const_spec((D_MODEL, D_MODEL)),
        row, row,
        _const_spec((D_MODEL, 2 * D_FF)),
        _const_spec((D_FF, D_MODEL)),
    ]
    n_groups = mod.shape[0]
    return pl.pallas_call(
        body,
        grid=(n_tiles + 2,),
        in_specs=in_specs,
        out_specs=[pl.BlockSpec(tile_blk, lagged(2, 1)),
                   pl.BlockSpec((1, nb, CONV_W - 1, D_MODEL), lagged(0, tiles_per_seq))],
        out_shape=[jax.ShapeDtypeStruct(x.shape, f32),
                   jax.ShapeDtypeStruct((n_groups, nb, CONV_W - 1, D_MODEL), f32)],
        scratch_shapes=[
            pltpu.VMEM((T, D_MODEL), bf16),
            pltpu.VMEM((nb, N_SLABS, HIST + L, LANES), f32),
            pltpu.VMEM((N_SLABS, T, LANES), f32),
            pltpu.VMEM((T, D_MODEL), bf16),
            pltpu.VMEM((T, D_MODEL), f32),
            pltpu.VMEM((T, D_MODEL), f32),
            pltpu.VMEM((T, D_MODEL), f32),
            pltpu.VMEM((T, D_MODEL), bf16),
            pltpu.VMEM((2, T, COL_BLK), bf16),
            pltpu.VMEM((T, D_MODEL), f32),
        ],
        compiler_params=pltpu.CompilerParams(
            dimension_semantics=("arbitrary",), vmem_limit_bytes=VMEM_LIMIT),
        name=name,
    )(*args, *weights)


def kernel(x_prompt, x_sample, cache_k, cache_v, cache_conv, c_prompt, c_sample, rel_table, w_ada, b_ada, norm1_g, norm2_g, w_in, sink, w_attn_out, dw_w, dw_b, conv_ln_g, conv_ln_b, w_conv_out, w_out, w_ffn_up, w_ffn_down, final_g):
    assert w_ada.shape[0] == 1, "single-layer kernel"
    B, S, _ = x_prompt.shape
    DB, DS, _ = x_sample.shape
    nb_attn = ATTN_SAMPLE_TOKENS // DS
    nb_mix = MIX_TOKENS // DS

    bias = _bias_table(rel_table)
    n_seq = B + DB
    pad = (-n_seq) % 8
    c_all = jnp.concatenate([c_prompt, c_sample, jnp.zeros((pad, D_MODEL), f32)], axis=0)
    mod = _modulation(c_all, w_ada[0], b_ada).reshape(n_seq + pad, 6, D_MODEL)
    mod_p, mod_s = mod[:B], mod[B:B + DB]

    row = lambda v: v.reshape(1, D_MODEL)
    w = w_in[0]
    w_k = w[:, O_K:O_V].astype(bf16).reshape(D_MODEL, KV_HEADS, 1, HEAD_DIM)
    w_kd = jnp.broadcast_to(w_k, (D_MODEL, KV_HEADS, 2, HEAD_DIM)).reshape(D_MODEL, KV_HEADS * LANES)
    w_in_b = w.astype(bf16)
    attn_w = (bias, sink[0], row(norm1_g[0]), w_in_b, w_kd, w[:, O_V:O_GLU].T.astype(bf16),
              w_attn_out[0].astype(bf16))
    mix_w = (row(norm1_g[0]), w_in_b, dw_w[0], row(dw_b[0]),
             row(conv_ln_g[0]), row(conv_ln_b[0]), w_conv_out[0].astype(bf16), w_out[0].astype(bf16),
             row(norm2_g[0]), row(final_g), w_ffn_up[0].astype(bf16), w_ffn_down[0].astype(bf16))

    a1p, nkp, nvp = _attn_call(x_prompt, mod_p, (), attn_w, 1, ATTN_TOKENS, True, "attn_prompt")
    ck = cache_k[0].reshape(DB, WINDOW, KV_W)
    cv = cache_v[0].reshape(DB, WINDOW, KV_W)
    a1s, nks, nvs = _attn_call(x_sample, mod_s, (ck, cv), attn_w, nb_attn, DS, False, "attn_sample")

    tiles = lambda a, n, l: a.reshape(-1, n, l, D_MODEL)
    y_p, ncp = _mix_call(tiles(x_prompt, 1, MIX_TOKENS), tiles(a1p, 1, MIX_TOKENS), mod_p[:, None], None,
                         mix_w, 1, MIX_TOKENS, True, "mix_prompt")
    y_s, ncs = _mix_call(tiles(x_sample, nb_mix, DS), tiles(a1s, nb_mix, DS), tiles(mod_s, nb_mix, 6),
                         tiles(cache_conv[0], nb_mix, CONV_W - 1), mix_w, nb_mix, DS, False, "mix_sample")

    kv5 = lambda a: a.reshape(1, a.shape[0], WINDOW, KV_HEADS, HEAD_DIM)
    return (y_p.reshape(B, S, D_MODEL), y_s.reshape(DB, DS, D_MODEL), kv5(nkp), kv5(nvp),
            ncp.reshape(1, B, CONV_W - 1, D_MODEL), kv5(nks), kv5(nvs), ncs.reshape(1, DB, CONV_W - 1, D_MODEL))
```

```python
import functools
import math

import jax
import jax.numpy as jnp
from jax import lax
from jax.experimental import pallas as pl
from jax.experimental.pallas import tpu as pltpu

f32 = jnp.float32
bf16 = jnp.bfloat16

D_MODEL = 1024
N_HEADS = 16
KV_HEADS = 4
HEAD_DIM = 64
GROUP = N_HEADS // KV_HEADS
WINDOW = 128
CHUNK = 64
BAND = WINDOW + CHUNK
CONV_W = 31
D_FF = 2816
NUM_BUCKETS = 32
MAX_DISTANCE = 128
EPS = 1e-6
NEG = -1e30
Q_W = N_HEADS * HEAD_DIM
KV_W = KV_HEADS * HEAD_DIM
O_K = Q_W
O_V = O_K + KV_W
O_GLU = O_V + KV_W
O_GA = O_GLU + 2 * D_MODEL
O_GB = O_GA + D_MODEL
IN_W = O_GB + D_MODEL

LANES = 128
N_SLABS = D_MODEL // LANES
PAIR = 2 * CHUNK
GQ = GROUP * CHUNK
HIST = 32
ROW_BLK = 32
COL_BLK = 256
N_CC = D_MODEL // COL_BLK
N_FF = D_FF // COL_BLK
VMEM_LIMIT = 56 * 1024 * 1024
ATTN_TOKENS = 1024
ATTN_SAMPLE_TOKENS = 512
MIX_TOKENS = 256

_NT = (((1,), (1,)), ((), ()))


def _sigmoid(x):
    return 1.0 / (1.0 + jnp.exp(-x))


def _rms(x, g):
    ms = jnp.mean(x * x, axis=-1, keepdims=True)
    return x * lax.rsqrt(ms + EPS) * g


def _mm(a, w_ref, c0, width):
    return jnp.dot(a, w_ref[:, c0:c0 + width], preferred_element_type=f32)


def _cols(c):
    return slice(c * COL_BLK, (c + 1) * COL_BLK)


def _head_row(vals, k, lane):
    row = jnp.full((1, GQ), vals(GROUP * k + GROUP - 1), f32)
    for g in range(GROUP - 2, -1, -1):
        row = jnp.where(lane < CHUNK * (g + 1), vals(GROUP * k + g), row)
    return row


def _interleave(pieces, blocks, blocks_first=False):
    order = [((n + 0.5) / len(pieces), int(blocks_first), fn) for n, fn in enumerate(pieces)]
    order += [((n + 0.5) / len(blocks), int(not blocks_first), fn) for n, fn in enumerate(blocks)]
    for _, _, fn in sorted(order, key=lambda item: item[:2]):
        fn()


def _const_spec(shape):
    nd = len(shape)
    return pl.BlockSpec(shape, lambda *_: (0,) * nd, pipeline_mode=pl.Buffered(1))


def _rel_bucket(rel):
    nb = NUM_BUCKETS // 2
    max_exact = nb // 2
    ret = (rel > 0).astype(jnp.int32) * nb
    n = jnp.abs(rel)
    nf = jnp.maximum(n, 1).astype(f32)
    large = max_exact + (jnp.log(nf / max_exact) / math.log(MAX_DISTANCE / max_exact)
                         * (nb - max_exact)).astype(jnp.int32)
    large = jnp.minimum(large, nb - 1)
    return ret + jnp.where(n < max_exact, n, large)


def _bias_body(idx_ref, tab_ref, o_ref):
    idx = idx_ref[...]
    lane = lax.broadcasted_iota(jnp.int32, (1, GQ), 1)
    for k in range(KV_HEADS):
        acc = jnp.zeros((BAND, GQ), f32)
        for b in range(NUM_BUCKETS):
            acc = jnp.where(idx == b, _head_row(lambda h: tab_ref[b, h], k, lane), acc)
        o_ref[k] = acc


def _bias_table(rel_table):
    kj = jnp.arange(BAND, dtype=jnp.int32)
    rel = kj[None, :] - WINDOW - jnp.arange(CHUNK, dtype=jnp.int32)[:, None]
    idx_t = jnp.tile(_rel_bucket(rel).T, (1, GROUP))
    return pl.pallas_call(
        _bias_body,
        in_specs=[pl.BlockSpec(memory_space=pltpu.VMEM), pl.BlockSpec(memory_space=pltpu.SMEM)],
        out_specs=pl.BlockSpec(memory_space=pltpu.VMEM),
        out_shape=jax.ShapeDtypeStruct((KV_HEADS, BAND, GQ), f32),
        name="rel_bias",
    )(idx_t, rel_table)


def _mod_body(c_ref, w_ref, b_ref, o_ref):
    c = c_ref[...]
    s = c * _sigmoid(c)
    o_ref[...] = jnp.dot(s.astype(bf16), w_ref[...].astype(bf16), preferred_element_type=f32) + b_ref[...]


def _modulation(c_all, w_ada, b_ada):
    rows = c_all.shape[0]
    blk = 2048
    return pl.pallas_call(
        _mod_body,
        grid=(6 * D_MODEL // blk,),
        in_specs=[pl.BlockSpec((rows, D_MODEL), lambda j: (0, 0)),
                  pl.BlockSpec((D_MODEL, blk), lambda j: (0, j)),
                  pl.BlockSpec((1, blk), lambda j: (0, j))],
        out_specs=pl.BlockSpec((rows, blk), lambda j: (0, j)),
        out_shape=jax.ShapeDtypeStruct((rows, 6 * D_MODEL), f32),
        name="adaln_mod",
    )(c_all, w_ada, b_ada)


def _attn_pair(chunks, bias_ref, sink_rows, st_s, pb_s, slot0, attT_ref, grp):
    lane_lo = lax.broadcasted_iota(jnp.int32, (CHUNK, LANES), 1) < HEAD_DIM
    n_pad = 2 * LANES - BAND
    rden = {}
    for par, (q_c, kd_bands, v_win, pad_top, valid) in enumerate(chunks):
        for k in range(KV_HEADS):
            blocks = []
            for p in range(2):
                qc = q_c[:, (2 * k + p) * LANES:(2 * k + p + 1) * LANES]
                blocks.append(jnp.where(lane_lo, qc, jnp.zeros_like(qc)))
                blocks.append(jnp.where(lane_lo, jnp.zeros_like(qc), qc))
            qm = jnp.concatenate(blocks, axis=0)
            st_s[slot0 + par * KV_HEADS + k] = lax.dot_general(kd_bands[k], qm, _NT, preferred_element_type=f32)
    for par, (q_c, kd_bands, v_win, pad_top, valid) in enumerate(chunks):
        for k in range(KV_HEADS):
            i = slot0 + par * KV_HEADS + k
            st = st_s[i] + bias_ref[k]
            if valid is not None:
                st = jnp.where(valid, st, NEG)
            m = jnp.maximum(jnp.max(st, axis=0, keepdims=True), sink_rows[k])
            p_ = jnp.exp(st - m)
            den = jnp.sum(p_, axis=0, keepdims=True) + jnp.exp(sink_rows[k] - m)
            r0 = n_pad if pad_top else 0
            pb_s[i, r0:r0 + BAND, :] = p_.astype(bf16)
            rden[i] = 1.0 / den
            z0 = 0 if pad_top else BAND
            pb_s[i, z0:z0 + n_pad, :] = jnp.zeros((n_pad, GQ), bf16)
    for par, (q_c, kd_bands, v_win, pad_top, valid) in enumerate(chunks):
        for k in range(KV_HEADS):
            i = slot0 + par * KV_HEADS + k
            ot = jnp.dot(v_win[k * HEAD_DIM:(k + 1) * HEAD_DIM, :], pb_s[i], preferred_element_type=f32) * rden[i]
            for g in range(GROUP):
                h = GROUP * k + g
                attT_ref[grp, h * HEAD_DIM:(h + 1) * HEAD_DIM, par * CHUNK:(par + 1) * CHUNK] = (
                    ot[:, g * CHUNK:(g + 1) * CHUNK])


def _attn_body(*refs, is_prompt, nb, L, n_tiles):
    if is_prompt:
        (x_ref, mod_ref, bias_ref, sink_ref, g1_ref, win_ref, wkd_ref, wvt_ref, wao_ref,
         a1_ref, nk_ref, nv_ref,
         h_s, q_s, kd_s, vt_s, st_s, pb_s, attT_s, att_s, ga_s) = refs
        ck_ref = cv_ref = None
    else:
        (x_ref, mod_ref, ck_ref, cv_ref, bias_ref, sink_ref, g1_ref, win_ref, wkd_ref, wvt_ref, wao_ref,
         a1_ref, nk_ref, nv_ref,
         h_s, q_s, kd_s, vt_s, st_s, pb_s, attT_s, att_s, ga_s) = refs
    T = nb * L
    n_grp = T // PAIR

    if is_prompt:
        t = pl.program_id(1)

        @pl.when(t == 0)
        def _():
            kd_s[:, 0:WINDOW, :] = jnp.zeros((KV_HEADS, WINDOW, LANES), bf16)
            vt_s[0] = jnp.zeros((KV_W, LANES), bf16)

    for s in range(nb):
        for i in range(L // ROW_BLK):
            rows = slice(i * ROW_BLK, (i + 1) * ROW_BLK)
            hh = _rms(x_ref[s, rows, :], g1_ref[...]) * (1.0 + mod_ref[s, 1:2, :]) + mod_ref[s, 0:1, :]
            h_s[s * L + i * ROW_BLK:s * L + (i + 1) * ROW_BLK, :] = hh.astype(bf16)

    if not is_prompt:
        for s in range(nb):
            ck = ck_ref[s]
            for k in range(KV_HEADS):
                ckk = ck[:, k * HEAD_DIM:(k + 1) * HEAD_DIM].astype(bf16)
                kd_s[s, k, 0:WINDOW, :] = jnp.concatenate([ckk, ckk], axis=1)
            vt_s[s, :, 0:WINDOW] = cv_ref[s].T.astype(bf16)
            vt_s[s, :, BAND:2 * LANES] = jnp.zeros((KV_W, 2 * LANES - BAND), bf16)
            nk_ref[s, 0:WINDOW - L, :] = ck[L:WINDOW, :]
            nv_ref[s, 0:WINDOW - L, :] = cv_ref[s, L:WINDOW, :]
        kvf = _mm(h_s[...], win_ref, O_K, 2 * KV_W)
        for s in range(nb):
            nk_ref[s, WINDOW - L:WINDOW, :] = kvf[s * L:(s + 1) * L, 0:KV_W]
            nv_ref[s, WINDOW - L:WINDOW, :] = kvf[s * L:(s + 1) * L, KV_W:2 * KV_W]

    for c in range(Q_W // COL_BLK):
        qf = _mm(h_s[...], win_ref, c * COL_BLK, COL_BLK)
        q_s[:, _cols(c)] = (qf * (HEAD_DIM ** -0.5)).astype(bf16)
    for c in range(KV_HEADS * LANES // COL_BLK):
        kdc = _mm(h_s[...], wkd_ref, c * COL_BLK, COL_BLK)
        for kk in range(COL_BLK // LANES):
            k = c * (COL_BLK // LANES) + kk
            kdk = kdc[:, kk * LANES:(kk + 1) * LANES].astype(bf16)
            if is_prompt:
                kd_s[k, WINDOW:WINDOW + T, :] = kdk
            else:
                for s in range(nb):
                    kd_s[s, k, WINDOW:BAND, :] = kdk[s * L:(s + 1) * L]
    vtf = lax.dot_general(wvt_ref[...], h_s[...], _NT, preferred_element_type=f32).astype(bf16)
    if is_prompt:
        for g in range(n_grp):
            vt_s[1 + g] = vtf[:, g * LANES:(g + 1) * LANES]
    else:
        for s in range(nb):
            vt_s[s, :, WINDOW:BAND] = vtf[:, s * L:(s + 1) * L]

    lane_gq = lax.broadcasted_iota(jnp.int32, (1, GQ), 1)
    sink_rows = [_head_row(lambda h: sink_ref[h], k, lane_gq) for k in range(KV_HEADS)]

    def attn_block(c2):
        chunks = []
        if is_prompt:
            v_win = jnp.concatenate([vt_s[c2], vt_s[c2 + 1]], axis=1)
            for par in range(2):
                r = c2 * PAIR + par * CHUNK
                pos = t * T + r - WINDOW + lax.broadcasted_iota(jnp.int32, (BAND, 1), 0)
                chunks.append((q_s[r:r + CHUNK, :],
                               [kd_s[k, r:r + BAND, :] for k in range(KV_HEADS)],
                               v_win, par == 1, pos >= 0 if r < WINDOW else None))
        else:
            for par in range(2):
                s = 2 * c2 + par
                chunks.append((q_s[s * CHUNK:(s + 1) * CHUNK, :],
                               [kd_s[s, k] for k in range(KV_HEADS)],
                               vt_s[s], False, None))
        _attn_pair(chunks, bias_ref, sink_rows, st_s, pb_s, (c2 % 2) * 2 * KV_HEADS, attT_s, c2)
        att_s[c2 * PAIR:(c2 + 1) * PAIR, :] = attT_s[c2].T.astype(bf16)

    def ga_piece(c):
        ga_s[:, _cols(c)] = _sigmoid(_mm(h_s[...], win_ref, O_GA + c * COL_BLK, COL_BLK))

    _interleave([functools.partial(ga_piece, c) for c in range(N_CC)],
                [functools.partial(attn_block, c2) for c2 in range(n_grp)], blocks_first=True)

    for c in range(N_CC):
        a1c = ga_s[:, _cols(c)] * _mm(att_s[...], wao_ref, c * COL_BLK, COL_BLK)
        for s in range(nb):
            a1_ref[s, :, _cols(c)] = a1c[s * L:(s + 1) * L]

    if is_prompt:
        @pl.when(t == n_tiles - 1)
        def _():
            kvf = _mm(h_s[T - WINDOW:T, :], win_ref, O_K, 2 * KV_W)
            nk_ref[0] = kvf[:, 0:KV_W]
            nv_ref[0] = kvf[:, KV_W:2 * KV_W]

        kd_s[:, 0:WINDOW, :] = kd_s[:, T:T + WINDOW, :]
        vt_s[0] = vt_s[n_grp]


def _attn_call(x, mod, caches, weights, nb, L, is_prompt, name):
    B, S, _ = x.shape
    T = nb * L
    if is_prompt:
        n_tiles = S // L
        grid = (B, n_tiles)
        xmap = lambda b, t: (b, t, 0)
        smap = lambda b, t: (b, 0, 0)
        kd_shape = (KV_HEADS, WINDOW + T, LANES)
        vt_shape = (1 + T // PAIR, KV_W, LANES)
    else:
        n_tiles = B // nb
        grid = (n_tiles,)
        xmap = smap = lambda i: (i, 0, 0)
        kd_shape = (nb, KV_HEADS, BAND, LANES)
        vt_shape = (nb, KV_W, 2 * LANES)
    body = functools.partial(_attn_body, is_prompt=is_prompt, nb=nb, L=L, n_tiles=n_tiles)
    cache_specs = [pl.BlockSpec((nb, WINDOW, KV_W), smap)] * len(caches)
    weight_specs = [
        _const_spec((KV_HEADS, BAND, GQ)),
        pl.BlockSpec(memory_space=pltpu.SMEM),
        _const_spec((1, D_MODEL)),
        _const_spec((D_MODEL, IN_W)),
        _const_spec((D_MODEL, KV_HEADS * LANES)),
        _const_spec((KV_W, D_MODEL)),
        _const_spec((Q_W, D_MODEL)),
    ]
    return pl.pallas_call(
        body,
        grid=grid,
        in_specs=[pl.BlockSpec((nb, L, D_MODEL), xmap), pl.BlockSpec((nb, 6, D_MODEL), smap)]
        + cache_specs + weight_specs,
        out_specs=[pl.BlockSpec((nb, L, D_MODEL), xmap),
                   pl.BlockSpec((nb, WINDOW, KV_W), smap),
                   pl.BlockSpec((nb, WINDOW, KV_W), smap)],
        out_shape=[jax.ShapeDtypeStruct((B, S, D_MODEL), f32),
                   jax.ShapeDtypeStruct((B, WINDOW, KV_W), f32),
                   jax.ShapeDtypeStruct((B, WINDOW, KV_W), f32)],
        scratch_shapes=[
            pltpu.VMEM((T, D_MODEL), bf16),
            pltpu.VMEM((T, Q_W), bf16),
            pltpu.VMEM(kd_shape, bf16),
            pltpu.VMEM(vt_shape, bf16),
            pltpu.VMEM((4 * KV_HEADS, BAND, GQ), f32),
            pltpu.VMEM((4 * KV_HEADS, 2 * LANES, GQ), bf16),
            pltpu.VMEM((T // PAIR, Q_W, PAIR), f32),
            pltpu.VMEM((T, Q_W), bf16),
            pltpu.VMEM((T, D_MODEL), f32),
        ],
        compiler_params=pltpu.CompilerParams(
            dimension_semantics=("arbitrary",) * len(grid), vmem_limit_bytes=VMEM_LIMIT),
        name=name,
    )(x, mod, *caches, *weights)


def _conv_block(zb_ref, s, j, i, dww_ref, cb_ref, out_row0):
    r0 = i * 32
    accs = [jnp.zeros((8, LANES), f32) for _ in range(4)]
    for k in range(CONV_W):
        wk = dww_ref[k:k + 1, j * LANES:(j + 1) * LANES]
        for a in range(4):
            start = r0 + (a // 2) * 16 + (a % 2) + (HIST - (CONV_W - 1)) + k
            accs[a] = accs[a] + zb_ref[s, j, pl.ds(start, 8, stride=2), :] * wk
    for a in range(4):
        start = out_row0 + r0 + (a // 2) * 16 + (a % 2)
        cb_ref[j, pl.ds(start, 8, stride=2), :] = accs[a]


def _mix_body(*refs, is_prompt, nb, L, n_tiles, tiles_per_seq):
    if is_prompt:
        (xa_ref, xb_ref, a1_ref, mod_ref, g1_ref, win_ref, dww_ref, dwb_ref,
         lng_ref, lnb_ref, wco_ref, wo_ref, g2_ref, fg_ref, wup_ref, wdn_ref,
         y_ref, nc_ref,
         h_s, zb_s, cb_s, cbf_s, gb_s, acc_s, x1_s, h2_s, act_s, d_s) = refs
        cc_ref = None
    else:
        (xa_ref, xb_ref, a1_ref, mod_ref, cc_ref, g1_ref, win_ref, dww_ref, dwb_ref,
         lng_ref, lnb_ref, wco_ref, wo_ref, g2_ref, fg_ref, wup_ref, wdn_ref,
         y_ref, nc_ref,
         h_s, zb_s, cb_s, cbf_s, gb_s, acc_s, x1_s, h2_s, act_s, d_s) = refs
    T = nb * L
    step = pl.program_id(0)
    t_in_seq = jnp.minimum(step, n_tiles - 1) % tiles_per_seq
    grp_a, grp_b, grp_c = [jnp.clip(step - lag, 0, n_tiles - 1) // tiles_per_seq for lag in range(3)]

    @pl.when(step == 0)
    def _():
        x1_s[...] = jnp.zeros((T, D_MODEL), f32)
        h_s[...] = jnp.zeros((T, D_MODEL), bf16)
        cb_s[...] = jnp.zeros((N_SLABS, T, LANES), f32)

    if is_prompt:
        @pl.when(t_in_seq == 0)
        def _():
            zb_s[0, :, 0:HIST, :] = jnp.zeros((N_SLABS, HIST, LANES), f32)

    def norm2_block(i):
        s = i * ROW_BLK // L
        rows = slice(i * ROW_BLK, (i + 1) * ROW_BLK)
        hh = _rms(x1_s[rows, :], g2_ref[...]) * (1.0 + mod_ref[grp_c, s, 4:5, :]) + mod_ref[grp_c, s, 3:4, :]
        h2_s[rows, :] = hh.astype(bf16)

    def norm1_block(i):
        s, r = divmod(i * ROW_BLK, L)
        rows = slice(i * ROW_BLK, (i + 1) * ROW_BLK)
        hh = _rms(xa_ref[0, s, r:r + ROW_BLK, :], g1_ref[...]) * (1.0 + mod_ref[grp_a, s, 1:2, :]) + mod_ref[grp_a, s, 0:1, :]
        h_s[rows, :] = hh.astype(bf16)

    def glu_piece(c):
        hv = h_s[...]
        za = _mm(hv, win_ref, O_GLU + c * COL_BLK, COL_BLK)
        zg = _mm(hv, win_ref, O_GLU + D_MODEL + c * COL_BLK, COL_BLK)
        zc = za * _sigmoid(zg)
        for jj in range(COL_BLK // LANES):
            j = c * (COL_BLK // LANES) + jj
            z = zc[:, jj * LANES:(jj + 1) * LANES]
            for s in range(nb):
                zb_s[s, j, HIST:HIST + L, :] = z[s * L:(s + 1) * L]
                if not is_prompt:
                    nc_ref[0, s, :, j * LANES:(j + 1) * LANES] = z[s * L + L - (CONV_W - 1):(s + 1) * L]

    def gb_piece(c):
        gb_s[:, _cols(c)] = _sigmoid(_mm(h_s[...], win_ref, O_GB + c * COL_BLK, COL_BLK))

    def up_piece(c):
        h2v = h2_s[...]
        gate = _mm(h2v, wup_ref, c * COL_BLK, COL_BLK)
        up = _mm(h2v, wup_ref, D_FF + c * COL_BLK, COL_BLK)
        act_s[c % 2] = (gate * _sigmoid(gate) * up).astype(bf16)

    def down_piece(c):
        d_s[...] += jnp.dot(act_s[c % 2], wdn_ref[c * COL_BLK:(c + 1) * COL_BLK, :], preferred_element_type=f32)

    def ln_block(i):
        rows = slice(i * ROW_BLK, (i + 1) * ROW_BLK)
        ys = [cb_s[j, rows, :] + dwb_ref[:, j * LANES:(j + 1) * LANES] for j in range(N_SLABS)]
        tot = ys[0]
        for j in range(1, N_SLABS):
            tot = tot + ys[j]
        mu = jnp.sum(tot, axis=-1, keepdims=True) * (1.0 / D_MODEL)
        ds = [y - mu for y in ys]
        sq = ds[0] * ds[0]
        for j in range(1, N_SLABS):
            sq = sq + ds[j] * ds[j]
        var = jnp.sum(sq, axis=-1, keepdims=True) * (1.0 / D_MODEL)
        rstd = lax.rsqrt(var + EPS)
        for j in range(N_SLABS):
            y = ds[j] * rstd * lng_ref[:, j * LANES:(j + 1) * LANES] + lnb_ref[:, j * LANES:(j + 1) * LANES]
            cbf_s[rows, j * LANES:(j + 1) * LANES] = (y * _sigmoid(y)).astype(bf16)

    def yb_piece(c):
        yb = _mm(cbf_s[...], wco_ref, c * COL_BLK, COL_BLK)
        for s in range(nb):
            rows = slice(s * L, (s + 1) * L)
            acc_s[rows, _cols(c)] = a1_ref[0, s, :, _cols(c)] + gb_s[rows, _cols(c)] * yb[rows]

    def final_block(i):
        s, r = divmod(i * ROW_BLK, L)
        rows = slice(i * ROW_BLK, (i + 1) * ROW_BLK)
        x2 = x1_s[rows, :] + mod_ref[grp_c, s, 5:6, :] * d_s[rows, :]
        y_ref[0, s, r:r + ROW_BLK, :] = _rms(x2, fg_ref[...])

    def out_piece(c):
        m = _mm(acc_s[...].astype(bf16), wo_ref, c * COL_BLK, COL_BLK)
        for s in range(nb):
            rows = slice(s * L, (s + 1) * L)
            x1_s[rows, _cols(c)] = xb_ref[0, s, :, _cols(c)] + mod_ref[grp_b, s, 2:3, _cols(c)] * m[rows]

    n_blk = T // ROW_BLK
    _interleave([functools.partial(gb_piece, c) for c in range(N_CC)],
                [functools.partial(ln_block, i) for i in range(n_blk)])
    for i in range(n_blk):
        norm1_block(i)
    if not is_prompt:
        for s in range(nb):
            for j in range(N_SLABS):
                zb_s[s, j, HIST - (CONV_W - 1):HIST, :] = cc_ref[0, s, :, j * LANES:(j + 1) * LANES]
    _interleave([functools.partial(glu_piece, c) for c in range(N_CC)],
                [functools.partial(norm2_block, i) for i in range(n_blk)])
    d_s[...] = jnp.zeros((T, D_MODEL), f32)
    up_piece(0)
    pieces = []
    for c in range(N_FF):
        if c + 1 < N_FF:
            pieces.append(functools.partial(up_piece, c + 1))
        pieces.append(functools.partial(down_piece, c))
    pieces += [functools.partial(yb_piece, c) for c in range(N_CC)]
    conv_blocks = [functools.partial(_conv_block, zb_s, s, j, i, dww_ref, cb_s, s * L)
                   for j in range(N_SLABS) for s in range(nb) for i in range(L // 32)]
    _interleave(pieces, conv_blocks)
    for i in range(n_blk):
        final_block(i)
    for c in range(N_CC):
        out_piece(c)

    if is_prompt:
        @pl.when(t_in_seq == tiles_per_seq - 1)
        def _():
            for j in range(N_SLABS):
                nc_ref[0, 0, :, j * LANES:(j + 1) * LANES] = zb_s[0, j, HIST + L - (CONV_W - 1):HIST + L, :]

        zb_s[0, :, 0:HIST, :] = zb_s[0, :, L:L + HIST, :]


def _mix_call(x, a1, mod, cc, weights, nb, L, is_prompt, name):
    n_tiles = x.shape[0]
    tiles_per_seq = n_tiles // mod.shape[0]
    T = nb * L

    def lagged(lag, per_group):
        def index_map(i):
            t = jnp.clip(i - lag, 0, n_tiles - 1)
            return (t // per_group, 0, 0, 0)
        return index_map

    body = functools.partial(_mix_body, is_prompt=is_prompt, nb=nb, L=L, n_tiles=n_tiles,
                             tiles_per_seq=tiles_per_seq)
    row = _const_spec((1, D_MODEL))
    tile_blk = (1, nb, L, D_MODEL)
    in_specs = [pl.BlockSpec(tile_blk, lagged(0, 1)),
                pl.BlockSpec(tile_blk, lagged(1, 1)),
                pl.BlockSpec(tile_blk, lagged(1, 1)),
                _const_spec(mod.shape)]
    args = [x, x, a1, mod]
    if not is_prompt:
        in_specs.append(pl.BlockSpec((1, nb, CONV_W - 1, D_MODEL), lagged(0, 1)))
        args.append(cc)
    in_specs += [
        row,
        _const_spec((D_MODEL, IN_W)),
        _const_spec((CONV_W, D_MODEL)),
        row, row, row,
        _const_spec((D_MODEL, D_MODEL)),
        _const_spec((D_MODEL, D_MODEL)),
        row, row,
        _const_spec((D_MODEL, 2 * D_FF)),
        _const_spec((D_FF, D_MODEL)),
    ]
    n_groups = mod.shape[0]
    return pl.pallas_call(
        body,
        grid=(n_tiles + 2,),
        in_specs=in_specs,
        out_specs=[pl.BlockSpec(tile_blk, lagged(2, 1)),
                   pl.BlockSpec((1, nb, CONV_W - 1, D_MODEL), lagged(0, tiles_per_seq))],
        out_shape=[jax.ShapeDtypeStruct(x.shape, f32),
                   jax.ShapeDtypeStruct((n_groups, nb, CONV_W - 1, D_MODEL), f32)],
        scratch_shapes=[
            pltpu.VMEM((T, D_MODEL), bf16),
            pltpu.VMEM((nb, N_SLABS, HIST + L, LANES), f32),
            pltpu.VMEM((N_SLABS, T, LANES), f32),
            pltpu.VMEM((T, D_MODEL), bf16),
            pltpu.VMEM((T, D_MODEL), f32),
            pltpu.VMEM((T, D_MODEL), f32),
            pltpu.VMEM((T, D_MODEL), f32),
            pltpu.VMEM((T, D_MODEL), bf16),
            pltpu.VMEM((2, T, COL_BLK), bf16),
            pltpu.VMEM((T, D_MODEL), f32),
        ],
        compiler_params=pltpu.CompilerParams(
            dimension_semantics=("arbitrary",), vmem_limit_bytes=VMEM_LIMIT),
        name=name,
    )(*args, *weights)


def kernel(x_prompt, x_sample, cache_k, cache_v, cache_conv, c_prompt, c_sample, rel_table, w_ada, b_ada, norm1_g, norm2_g, w_in, sink, w_attn_out, dw_w, dw_b, conv_ln_g, conv_ln_b, w_conv_out, w_out, w_ffn_up, w_ffn_down, final_g):
    assert w_ada.shape[0] == 1, "single-layer kernel"
    B, S, _ = x_prompt.shape
    DB, DS, _ = x_sample.shape
    nb_attn = ATTN_SAMPLE_TOKENS // DS
    nb_mix = MIX_TOKENS // DS

    bias = _bias_table(rel_table)
    n_seq = B + DB
    pad = (-n_seq) % 8
    c_all = jnp.concatenate([c_prompt, c_sample, jnp.zeros((pad, D_MODEL), f32)], axis=0)
    mod = _modulation(c_all, w_ada[0], b_ada).reshape(n_seq + pad, 6, D_MODEL)
    mod_p, mod_s = mod[:B], mod[B:B + DB]

    row = lambda v: v.reshape(1, D_MODEL)
    w = w_in[0]
    w_k = w[:, O_K:O_V].astype(bf16).reshape(D_MODEL, KV_HEADS, 1, HEAD_DIM)
    w_kd = jnp.broadcast_to(w_k, (D_MODEL, KV_HEADS, 2, HEAD_DIM)).reshape(D_MODEL, KV_HEADS * LANES)
    w_in_b = w.astype(bf16)
    attn_w = (bias, sink[0], row(norm1_g[0]), w_in_b, w_kd, w[:, O_V:O_GLU].T.astype(bf16),
              w_attn_out[0].astype(bf16))
    mix_w = (row(norm1_g[0]), w_in_b, dw_w[0], row(dw_b[0]),
             row(conv_ln_g[0]), row(conv_ln_b[0]), w_conv_out[0].astype(bf16), w_out[0].astype(bf16),
             row(norm2_g[0]), row(final_g), w_ffn_up[0].astype(bf16), w_ffn_down[0].astype(bf16))

    a1p, nkp, nvp = _attn_call(x_prompt, mod_p, (), attn_w, 1, ATTN_TOKENS, True, "attn_prompt")
    ck = cache_k[0].reshape(DB, WINDOW, KV_W)
    cv = cache_v[0].reshape(DB, WINDOW, KV_W)
    a1s, nks, nvs = _attn_call(x_sample, mod_s, (ck, cv), attn_w, nb_attn, DS, False, "attn_sample")

    tiles = lambda a, n, l: a.reshape(-1, n, l, D_MODEL)
    y_p, ncp = _mix_call(tiles(x_prompt, 1, MIX_TOKENS), tiles(a1p, 1, MIX_TOKENS), mod_p[:, None], None,
                         mix_w, 1, MIX_TOKENS, True, "mix_prompt")
    y_s, ncs = _mix_call(tiles(x_sample, nb_mix, DS), tiles(a1s, nb_mix, DS), tiles(mod_s, nb_mix, 6),
                         tiles(cache_conv[0], nb_mix, CONV_W - 1), mix_w, nb_mix, DS, False, "mix_sample")

    kv5 = lambda a: a.reshape(1, a.shape[0], WINDOW, KV_HEADS, HEAD_DIM)
    return (y_p.reshape(B, S, D_MODEL), y_s.reshape(DB, DS, D_MODEL), kv5(nkp), kv5(nvp),
            ncp.reshape(1, B, CONV_W - 1, D_MODEL), kv5(nks), kv5(nvs), ncs.reshape(1, DB, CONV_W - 1, D_MODEL))
```

```python
import functools
import math

import jax
import jax.numpy as jnp
from jax import lax
from jax.experimental import pallas as pl
from jax.experimental.pallas import tpu as pltpu

f32 = jnp.float32
bf16 = jnp.bfloat16

D_MODEL = 1024
N_HEADS = 16
KV_HEADS = 4
HEAD_DIM = 64
GROUP = N_HEADS // KV_HEADS
WINDOW = 128
CHUNK = 64
BAND = WINDOW + CHUNK
CONV_W = 31
D_FF = 2816
NUM_BUCKETS = 32
MAX_DISTANCE = 128
EPS = 1e-6
NEG = -1e30
Q_W = N_HEADS * HEAD_DIM
KV_W = KV_HEADS * HEAD_DIM
O_K = Q_W
O_V = O_K + KV_W
O_GLU = O_V + KV_W
O_GA = O_GLU + 2 * D_MODEL
O_GB = O_GA + D_MODEL
IN_W = O_GB + D_MODEL

LANES = 128
N_SLABS = D_MODEL // LANES
PAIR = 2 * CHUNK
GQ = GROUP * CHUNK
HIST = 32
ROW_BLK = 32
ATTN_DEPTH = 1
COL_BLK = 256
N_CC = D_MODEL // COL_BLK
N_FF = D_FF // COL_BLK
VMEM_LIMIT = 56 * 1024 * 1024
ATTN_TOKENS = 1024
ATTN_SAMPLE_TOKENS = 512
MIX_TOKENS = 256

_NT = (((1,), (1,)), ((), ()))


def _sigmoid(x):
    return 1.0 / (1.0 + jnp.exp(-x))


def _rms(x, g):
    ms = jnp.mean(x * x, axis=-1, keepdims=True)
    return x * lax.rsqrt(ms + EPS) * g


def _mm(a, w_ref, c0, width):
    return jnp.dot(a, w_ref[:, c0:c0 + width], preferred_element_type=f32)


def _cols(c):
    return slice(c * COL_BLK, (c + 1) * COL_BLK)


def _head_row(vals, k, lane):
    row = jnp.full((1, GQ), vals(GROUP * k + GROUP - 1), f32)
    for g in range(GROUP - 2, -1, -1):
        row = jnp.where(lane < CHUNK * (g + 1), vals(GROUP * k + g), row)
    return row


def _interleave(pieces, blocks, blocks_first=False):
    order = [((n + 0.5) / len(pieces), int(blocks_first), fn) for n, fn in enumerate(pieces)]
    order += [((n + 0.5) / len(blocks), int(not blocks_first), fn) for n, fn in enumerate(blocks)]
    for _, _, fn in sorted(order, key=lambda item: item[:2]):
        fn()


def _const_spec(shape):
    nd = len(shape)
    return pl.BlockSpec(shape, lambda *_: (0,) * nd, pipeline_mode=pl.Buffered(1))


def _rel_bucket(rel):
    nb = NUM_BUCKETS // 2
    max_exact = nb // 2
    ret = (rel > 0).astype(jnp.int32) * nb
    n = jnp.abs(rel)
    nf = jnp.maximum(n, 1).astype(f32)
    large = max_exact + (jnp.log(nf / max_exact) / math.log(MAX_DISTANCE / max_exact)
                         * (nb - max_exact)).astype(jnp.int32)
    large = jnp.minimum(large, nb - 1)
    return ret + jnp.where(n < max_exact, n, large)


def _bias_body(idx_ref, tab_ref, o_ref):
    idx = idx_ref[...]
    lane = lax.broadcasted_iota(jnp.int32, (1, GQ), 1)
    for k in range(KV_HEADS):
        acc = jnp.zeros((BAND, GQ), f32)
        for b in range(NUM_BUCKETS):
            acc = jnp.where(idx == b, _head_row(lambda h: tab_ref[b, h], k, lane), acc)
        o_ref[k] = acc


def _bias_table(rel_table):
    kj = jnp.arange(BAND, dtype=jnp.int32)
    rel = kj[None, :] - WINDOW - jnp.arange(CHUNK, dtype=jnp.int32)[:, None]
    idx_t = jnp.tile(_rel_bucket(rel).T, (1, GROUP))
    return pl.pallas_call(
        _bias_body,
        in_specs=[pl.BlockSpec(memory_space=pltpu.VMEM), pl.BlockSpec(memory_space=pltpu.SMEM)],
        out_specs=pl.BlockSpec(memory_space=pltpu.VMEM),
        out_shape=jax.ShapeDtypeStruct((KV_HEADS, BAND, GQ), f32),
        name="rel_bias",
    )(idx_t, rel_table)


def _mod_body(c_ref, w_ref, b_ref, o_ref):
    c = c_ref[...]
    s = c * _sigmoid(c)
    o_ref[...] = jnp.dot(s.astype(bf16), w_ref[...].astype(bf16), preferred_element_type=f32) + b_ref[...]


def _modulation(c_all, w_ada, b_ada):
    rows = c_all.shape[0]
    blk = 2048
    return pl.pallas_call(
        _mod_body,
        grid=(6 * D_MODEL // blk,),
        in_specs=[pl.BlockSpec((rows, D_MODEL), lambda j: (0, 0)),
                  pl.BlockSpec((D_MODEL, blk), lambda j: (0, j)),
                  pl.BlockSpec((1, blk), lambda j: (0, j))],
        out_specs=pl.BlockSpec((rows, blk), lambda j: (0, j)),
        out_shape=jax.ShapeDtypeStruct((rows, 6 * D_MODEL), f32),
        name="adaln_mod",
    )(c_all, w_ada, b_ada)


def _attn_pair(chunks, bias_ref, sink_rows, st_s, pb_s, slot0, attT_ref, grp):
    lane_lo = lax.broadcasted_iota(jnp.int32, (CHUNK, LANES), 1) < HEAD_DIM
    n_pad = 2 * LANES - BAND
    rden = {}
    for par, (q_c, kd_bands, v_win, pad_top, valid) in enumerate(chunks):
        for k in range(KV_HEADS):
            blocks = []
            for p in range(2):
                qc = q_c[:, (2 * k + p) * LANES:(2 * k + p + 1) * LANES]
                blocks.append(jnp.where(lane_lo, qc, jnp.zeros_like(qc)))
                blocks.append(jnp.where(lane_lo, jnp.zeros_like(qc), qc))
            qm = jnp.concatenate(blocks, axis=0)
            st_s[slot0 + par * KV_HEADS + k] = lax.dot_general(kd_bands[k], qm, _NT, preferred_element_type=f32)
    for par, (q_c, kd_bands, v_win, pad_top, valid) in enumerate(chunks):
        for k in range(KV_HEADS):
            i = slot0 + par * KV_HEADS + k
            st = st_s[i] + bias_ref[k]
            if valid is not None:
                st = jnp.where(valid, st, NEG)
            m = jnp.maximum(jnp.max(st, axis=0, keepdims=True), sink_rows[k])
            p_ = jnp.exp(st - m)
            den = jnp.sum(p_, axis=0, keepdims=True) + jnp.exp(sink_rows[k] - m)
            r0 = n_pad if pad_top else 0
            pb_s[i, r0:r0 + BAND, :] = p_.astype(bf16)
            rden[i] = 1.0 / den
            z0 = 0 if pad_top else BAND
            pb_s[i, z0:z0 + n_pad, :] = jnp.zeros((n_pad, GQ), bf16)
    for par, (q_c, kd_bands, v_win, pad_top, valid) in enumerate(chunks):
        for k in range(KV_HEADS):
            i = slot0 + par * KV_HEADS + k
            ot = jnp.dot(v_win[k * HEAD_DIM:(k + 1) * HEAD_DIM, :], pb_s[i], preferred_element_type=f32) * rden[i]
            for g in range(GROUP):
                h = GROUP * k + g
                attT_ref[grp, h * HEAD_DIM:(h + 1) * HEAD_DIM, par * CHUNK:(par + 1) * CHUNK] = (
                    ot[:, g * CHUNK:(g + 1) * CHUNK])


def _attn_body(*refs, is_prompt, nb, L, n_tiles):
    if is_prompt:
        (x_ref, mod_ref, bias_ref, sink_ref, g1_ref, win_ref, wkd_ref, wvt_ref, wao_ref,
         a1_ref, nk_ref, nv_ref,
         h_s, q_s, kd_s, vt_s, st_s, pb_s, attT_s, att_s, ga_s) = refs
        ck_ref = cv_ref = None
    else:
        (x_ref, mod_ref, ck_ref, cv_ref, bias_ref, sink_ref, g1_ref, win_ref, wkd_ref, wvt_ref, wao_ref,
         a1_ref, nk_ref, nv_ref,
         h_s, q_s, kd_s, vt_s, st_s, pb_s, attT_s, att_s, ga_s) = refs
    T = nb * L
    n_grp = T // PAIR

    if is_prompt:
        t = pl.program_id(1)

        @pl.when(t == 0)
        def _():
            kd_s[:, 0:WINDOW, :] = jnp.zeros((KV_HEADS, WINDOW, LANES), bf16)
            vt_s[0] = jnp.zeros((KV_W, LANES), bf16)

    for s in range(nb):
        for i in range(L // ROW_BLK):
            rows = slice(i * ROW_BLK, (i + 1) * ROW_BLK)
            hh = _rms(x_ref[s, rows, :], g1_ref[...]) * (1.0 + mod_ref[s, 1:2, :]) + mod_ref[s, 0:1, :]
            h_s[s * L + i * ROW_BLK:s * L + (i + 1) * ROW_BLK, :] = hh.astype(bf16)

    if not is_prompt:
        for s in range(nb):
            ck = ck_ref[s]
            for k in range(KV_HEADS):
                ckk = ck[:, k * HEAD_DIM:(k + 1) * HEAD_DIM].astype(bf16)
                kd_s[s, k, 0:WINDOW, :] = jnp.concatenate([ckk, ckk], axis=1)
            vt_s[s, :, 0:WINDOW] = cv_ref[s].T.astype(bf16)
            vt_s[s, :, BAND:2 * LANES] = jnp.zeros((KV_W, 2 * LANES - BAND), bf16)
            nk_ref[s, 0:WINDOW - L, :] = ck[L:WINDOW, :]
            nv_ref[s, 0:WINDOW - L, :] = cv_ref[s, L:WINDOW, :]
        kvf = _mm(h_s[...], win_ref, O_K, 2 * KV_W)
        for s in range(nb):
            nk_ref[s, WINDOW - L:WINDOW, :] = kvf[s * L:(s + 1) * L, 0:KV_W]
            nv_ref[s, WINDOW - L:WINDOW, :] = kvf[s * L:(s + 1) * L, KV_W:2 * KV_W]

    for c in range(Q_W // COL_BLK):
        qf = _mm(h_s[...], win_ref, c * COL_BLK, COL_BLK)
        q_s[:, _cols(c)] = (qf * (HEAD_DIM ** -0.5)).astype(bf16)
    for c in range(KV_HEADS * LANES // COL_BLK):
        kdc = _mm(h_s[...], wkd_ref, c * COL_BLK, COL_BLK)
        for kk in range(COL_BLK // LANES):
            k = c * (COL_BLK // LANES) + kk
            kdk = kdc[:, kk * LANES:(kk + 1) * LANES].astype(bf16)
            if is_prompt:
                kd_s[k, WINDOW:WINDOW + T, :] = kdk
            else:
                for s in range(nb):
                    kd_s[s, k, WINDOW:BAND, :] = kdk[s * L:(s + 1) * L]
    vtf = lax.dot_general(wvt_ref[...], h_s[...], _NT, preferred_element_type=f32).astype(bf16)
    if is_prompt:
        for g in range(n_grp):
            vt_s[1 + g] = vtf[:, g * LANES:(g + 1) * LANES]
    else:
        for s in range(nb):
            vt_s[s, :, WINDOW:BAND] = vtf[:, s * L:(s + 1) * L]

    lane_gq = lax.broadcasted_iota(jnp.int32, (1, GQ), 1)
    sink_rows = [_head_row(lambda h: sink_ref[h], k, lane_gq) for k in range(KV_HEADS)]

    def attn_block(c2):
        chunks = []
        if is_prompt:
            v_win = jnp.concatenate([vt_s[c2], vt_s[c2 + 1]], axis=1)
            for par in range(2):
                r = c2 * PAIR + par * CHUNK
                pos = t * T + r - WINDOW + lax.broadcasted_iota(jnp.int32, (BAND, 1), 0)
                chunks.append((q_s[r:r + CHUNK, :],
                               [kd_s[k, r:r + BAND, :] for k in range(KV_HEADS)],
                               v_win, par == 1, pos >= 0 if r < WINDOW else None))
        else:
            for par in range(2):
                s = 2 * c2 + par
                chunks.append((q_s[s * CHUNK:(s + 1) * CHUNK, :],
                               [kd_s[s, k] for k in range(KV_HEADS)],
                               vt_s[s], False, None))
        _attn_pair(chunks, bias_ref, sink_rows, st_s, pb_s, (c2 % ATTN_DEPTH) * 2 * KV_HEADS, attT_s, c2)
        att_s[c2 * PAIR:(c2 + 1) * PAIR, :] = attT_s[c2].T.astype(bf16)

    def ga_piece(c):
        ga_s[:, _cols(c)] = _sigmoid(_mm(h_s[...], win_ref, O_GA + c * COL_BLK, COL_BLK))

    _interleave([functools.partial(ga_piece, c) for c in range(N_CC)],
                [functools.partial(attn_block, c2) for c2 in range(n_grp)], blocks_first=True)

    for c in range(N_CC):
        a1c = ga_s[:, _cols(c)] * _mm(att_s[...], wao_ref, c * COL_BLK, COL_BLK)
        for s in range(nb):
            a1_ref[s, :, _cols(c)] = a1c[s * L:(s + 1) * L]

    if is_prompt:
        @pl.when(t == n_tiles - 1)
        def _():
            kvf = _mm(h_s[T - WINDOW:T, :], win_ref, O_K, 2 * KV_W)
            nk_ref[0] = kvf[:, 0:KV_W]
            nv_ref[0] = kvf[:, KV_W:2 * KV_W]

        kd_s[:, 0:WINDOW, :] = kd_s[:, T:T + WINDOW, :]
        vt_s[0] = vt_s[n_grp]


def _attn_call(x, mod, caches, weights, nb, L, is_prompt, name):
    B, S, _ = x.shape
    T = nb * L
    if is_prompt:
        n_tiles = S // L
        grid = (B, n_tiles)
        xmap = lambda b, t: (b, t, 0)
        smap = lambda b, t: (b, 0, 0)
        kd_shape = (KV_HEADS, WINDOW + T, LANES)
        vt_shape = (1 + T // PAIR, KV_W, LANES)
    else:
        n_tiles = B // nb
        grid = (n_tiles,)
        xmap = smap = lambda i: (i, 0, 0)
        kd_shape = (nb, KV_HEADS, BAND, LANES)
        vt_shape = (nb, KV_W, 2 * LANES)
    body = functools.partial(_attn_body, is_prompt=is_prompt, nb=nb, L=L, n_tiles=n_tiles)
    cache_specs = [pl.BlockSpec((nb, WINDOW, KV_W), smap)] * len(caches)
    weight_specs = [
        _const_spec((KV_HEADS, BAND, GQ)),
        pl.BlockSpec(memory_space=pltpu.SMEM),
        _const_spec((1, D_MODEL)),
        _const_spec((D_MODEL, IN_W)),
        _const_spec((D_MODEL, KV_HEADS * LANES)),
        _const_spec((KV_W, D_MODEL)),
        _const_spec((Q_W, D_MODEL)),
    ]
    return pl.pallas_call(
        body,
        grid=grid,
        in_specs=[pl.BlockSpec((nb, L, D_MODEL), xmap), pl.BlockSpec((nb, 6, D_MODEL), smap)]
        + cache_specs + weight_specs,
        out_specs=[pl.BlockSpec((nb, L, D_MODEL), xmap),
                   pl.BlockSpec((nb, WINDOW, KV_W), smap),
                   pl.BlockSpec((nb, WINDOW, KV_W), smap)],
        out_shape=[jax.ShapeDtypeStruct((B, S, D_MODEL), f32),
                   jax.ShapeDtypeStruct((B, WINDOW, KV_W), f32),
                   jax.ShapeDtypeStruct((B, WINDOW, KV_W), f32)],
        scratch_shapes=[
            pltpu.VMEM((T, D_MODEL), bf16),
            pltpu.VMEM((T, Q_W), bf16),
            pltpu.VMEM(kd_shape, bf16),
            pltpu.VMEM(vt_shape, bf16),
            pltpu.VMEM((ATTN_DEPTH * 2 * KV_HEADS, BAND, GQ), f32),
            pltpu.VMEM((ATTN_DEPTH * 2 * KV_HEADS, 2 * LANES, GQ), bf16),
            pltpu.VMEM((T // PAIR, Q_W, PAIR), f32),
            pltpu.VMEM((T, Q_W), bf16),
            pltpu.VMEM((T, D_MODEL), f32),
        ],
        compiler_params=pltpu.CompilerParams(
            dimension_semantics=("arbitrary",) * len(grid), vmem_limit_bytes=VMEM_LIMIT),
        name=name,
    )(x, mod, *caches, *weights)


def _conv_block(zb_ref, s, j, i, dww_ref, cb_ref, out_row0):
    r0 = i * 32
    accs = [jnp.zeros((8, LANES), f32) for _ in range(4)]
    for k in range(CONV_W):
        wk = dww_ref[k:k + 1, j * LANES:(j + 1) * LANES]
        for a in range(4):
            start = r0 + (a // 2) * 16 + (a % 2) + (HIST - (CONV_W - 1)) + k
            accs[a] = accs[a] + zb_ref[s, j, pl.ds(start, 8, stride=2), :] * wk
    for a in range(4):
        start = out_row0 + r0 + (a // 2) * 16 + (a % 2)
        cb_ref[j, pl.ds(start, 8, stride=2), :] = accs[a]


def _mix_body(*refs, is_prompt, nb, L, n_tiles, tiles_per_seq):
    if is_prompt:
        (xa_ref, xb_ref, a1_ref, mod_ref, g1_ref, win_ref, dww_ref, dwb_ref,
         lng_ref, lnb_ref, wco_ref, wo_ref, g2_ref, fg_ref, wup_ref, wdn_ref,
         y_ref, nc_ref,
         h_s, zb_s, cb_s, cbf_s, gb_s, acc_s, x1_s, h2_s, act_s, d_s) = refs
        cc_ref = None
    else:
        (xa_ref, xb_ref, a1_ref, mod_ref, cc_ref, g1_ref, win_ref, dww_ref, dwb_ref,
         lng_ref, lnb_ref, wco_ref, wo_ref, g2_ref, fg_ref, wup_ref, wdn_ref,
         y_ref, nc_ref,
         h_s, zb_s, cb_s, cbf_s, gb_s, acc_s, x1_s, h2_s, act_s, d_s) = refs
    T = nb * L
    step = pl.program_id(0)
    t_in_seq = jnp.minimum(step, n_tiles - 1) % tiles_per_seq
    grp_a, grp_b, grp_c = [jnp.clip(step - lag, 0, n_tiles - 1) // tiles_per_seq for lag in range(3)]

    @pl.when(step == 0)
    def _():
        x1_s[...] = jnp.zeros((T, D_MODEL), f32)
        h_s[...] = jnp.zeros((T, D_MODEL), bf16)
        cb_s[...] = jnp.zeros((N_SLABS, T, LANES), f32)

    if is_prompt:
        @pl.when(t_in_seq == 0)
        def _():
            zb_s[0, :, 0:HIST, :] = jnp.zeros((N_SLABS, HIST, LANES), f32)

    def norm2_block(i):
        s = i * ROW_BLK // L
        rows = slice(i * ROW_BLK, (i + 1) * ROW_BLK)
        hh = _rms(x1_s[rows, :], g2_ref[...]) * (1.0 + mod_ref[grp_c, s, 4:5, :]) + mod_ref[grp_c, s, 3:4, :]
        h2_s[rows, :] = hh.astype(bf16)

    def norm1_block(i):
        s, r = divmod(i * ROW_BLK, L)
        rows = slice(i * ROW_BLK, (i + 1) * ROW_BLK)
        hh = _rms(xa_ref[0, s, r:r + ROW_BLK, :], g1_ref[...]) * (1.0 + mod_ref[grp_a, s, 1:2, :]) + mod_ref[grp_a, s, 0:1, :]
        h_s[rows, :] = hh.astype(bf16)

    def glu_piece(c):
        hv = h_s[...]
        za = _mm(hv, win_ref, O_GLU + c * COL_BLK, COL_BLK)
        zg = _mm(hv, win_ref, O_GLU + D_MODEL + c * COL_BLK, COL_BLK)
        zc = za * _sigmoid(zg)
        for jj in range(COL_BLK // LANES):
            j = c * (COL_BLK // LANES) + jj
            z = zc[:, jj * LANES:(jj + 1) * LANES]
            for s in range(nb):
                zb_s[s, j, HIST:HIST + L, :] = z[s * L:(s + 1) * L]
                if not is_prompt:
                    nc_ref[0, s, :, j * LANES:(j + 1) * LANES] = z[s * L + L - (CONV_W - 1):(s + 1) * L]

    def gb_piece(c):
        gb_s[:, _cols(c)] = _sigmoid(_mm(h_s[...], win_ref, O_GB + c * COL_BLK, COL_BLK))

    def up_piece(c):
        h2v = h2_s[...]
        gate = _mm(h2v, wup_ref, c * COL_BLK, COL_BLK)
        up = _mm(h2v, wup_ref, D_FF + c * COL_BLK, COL_BLK)
        act_s[c % 2] = (gate * _sigmoid(gate) * up).astype(bf16)

    def down_piece(c):
        d_s[...] += jnp.dot(act_s[c % 2], wdn_ref[c * COL_BLK:(c + 1) * COL_BLK, :], preferred_element_type=f32)

    def ln_block(i):
        rows = slice(i * ROW_BLK, (i + 1) * ROW_BLK)
        ys = [cb_s[j, rows, :] + dwb_ref[:, j * LANES:(j + 1) * LANES] for j in range(N_SLABS)]
        tot = ys[0]
        for j in range(1, N_SLABS):
            tot = tot + ys[j]
        mu = jnp.sum(tot, axis=-1, keepdims=True) * (1.0 / D_MODEL)
        ds = [y - mu for y in ys]
        sq = ds[0] * ds[0]
        for j in range(1, N_SLABS):
            sq = sq + ds[j] * ds[j]
        var = jnp.sum(sq, axis=-1, keepdims=True) * (1.0 / D_MODEL)
        rstd = lax.rsqrt(var + EPS)
        for j in range(N_SLABS):
            y = ds[j] * rstd * lng_ref[:, j * LANES:(j + 1) * LANES] + lnb_ref[:, j * LANES:(j + 1) * LANES]
            cbf_s[rows, j * LANES:(j + 1) * LANES] = (y * _sigmoid(y)).astype(bf16)

    def yb_piece(c):
        yb = _mm(cbf_s[...], wco_ref, c * COL_BLK, COL_BLK)
        for s in range(nb):
            rows = slice(s * L, (s + 1) * L)
            acc_s[rows, _cols(c)] = a1_ref[0, s, :, _cols(c)] + gb_s[rows, _cols(c)] * yb[rows]

    def final_block(i):
        s, r = divmod(i * ROW_BLK, L)
        rows = slice(i * ROW_BLK, (i + 1) * ROW_BLK)
        x2 = x1_s[rows, :] + mod_ref[grp_c, s, 5:6, :] * d_s[rows, :]
        y_ref[0, s, r:r + ROW_BLK, :] = _rms(x2, fg_ref[...])

    def out_piece(c):
        m = _mm(acc_s[...].astype(bf16), wo_ref, c * COL_BLK, COL_BLK)
        for s in range(nb):
            rows = slice(s * L, (s + 1) * L)
            x1_s[rows, _cols(c)] = xb_ref[0, s, :, _cols(c)] + mod_ref[grp_b, s, 2:3, _cols(c)] * m[rows]

    n_blk = T // ROW_BLK
    _interleave([functools.partial(gb_piece, c) for c in range(N_CC)],
                [functools.partial(ln_block, i) for i in range(n_blk)])
    for i in range(n_blk):
        norm1_block(i)
    if not is_prompt:
        for s in range(nb):
            for j in range(N_SLABS):
                zb_s[s, j, HIST - (CONV_W - 1):HIST, :] = cc_ref[0, s, :, j * LANES:(j + 1) * LANES]
    _interleave([functools.partial(glu_piece, c) for c in range(N_CC)],
                [functools.partial(norm2_block, i) for i in range(n_blk)])
    d_s[...] = jnp.zeros((T, D_MODEL), f32)
    up_piece(0)
    pieces = []
    for c in range(N_FF):
        if c + 1 < N_FF:
            pieces.append(functools.partial(up_piece, c + 1))
        pieces.append(functools.partial(down_piece, c))
    pieces += [functools.partial(yb_piece, c) for c in range(N_CC)]
    conv_blocks = [functools.partial(_conv_block, zb_s, s, j, i, dww_ref, cb_s, s * L)
                   for j in range(N_SLABS) for s in range(nb) for i in range(L // 32)]
    _interleave(pieces, conv_blocks)
    for i in range(n_blk):
        final_block(i)
    for c in range(N_CC):
        out_piece(c)

    if is_prompt:
        @pl.when(t_in_seq == tiles_per_seq - 1)
        def _():
            for j in range(N_SLABS):
                nc_ref[0, 0, :, j * LANES:(j + 1) * LANES] = zb_s[0, j, HIST + L - (CONV_W - 1):HIST + L, :]

        zb_s[0, :, 0:HIST, :] = zb_s[0, :, L:L + HIST, :]


def _mix_call(x, a1, mod, cc, weights, nb, L, is_prompt, name):
    n_tiles = x.shape[0]
    tiles_per_seq = n_tiles // mod.shape[0]
    T = nb * L

    def lagged(lag, per_group):
        def index_map(i):
            t = jnp.clip(i - lag, 0, n_tiles - 1)
            return (t // per_group, 0, 0, 0)
        return index_map

    body = functools.partial(_mix_body, is_prompt=is_prompt, nb=nb, L=L, n_tiles=n_tiles,
                             tiles_per_seq=tiles_per_seq)
    row = _const_spec((1, D_MODEL))
    tile_blk = (1, nb, L, D_MODEL)
    in_specs = [pl.BlockSpec(tile_blk, lagged(0, 1)),
                pl.BlockSpec(tile_blk, lagged(1, 1)),
                pl.BlockSpec(tile_blk, lagged(1, 1)),
                _const_spec(mod.shape)]
    args = [x, x, a1, mod]
    if not is_prompt:
        in_specs.append(pl.BlockSpec((1, nb, CONV_W - 1, D_MODEL), lagged(0, 1)))
        args.append(cc)
    in_specs += [
        row,
        _const_spec((D_MODEL, IN_W)),
        _const_spec((CONV_W, D_MODEL)),
        row, row, row,
        _const_spec((D_MODEL, D_MODEL)),
        _const_spec((D_MODEL, D_MODEL)),
        row, row,
        _const_spec((D_MODEL, 2 * D_FF)),
        _const_spec((D_FF, D_MODEL)),
    ]
    n_groups = mod.shape[0]
    return pl.pallas_call(
        body,
        grid=(n_tiles + 2,),
        in_specs=in_specs,
        out_specs=[pl.BlockSpec(tile_blk, lagged(2, 1)),
                   pl.BlockSpec((1, nb, CONV_W - 1, D_MODEL), lagged(0, tiles_per_seq))],
        out_shape=[jax.ShapeDtypeStruct(x.shape, f32),
                   jax.ShapeDtypeStruct((n_groups, nb, CONV_W - 1, D_MODEL), f32)],
        scratch_shapes=[
            pltpu.VMEM((T, D_MODEL), bf16),
            pltpu.VMEM((nb, N_SLABS, HIST + L, LANES), f32),
            pltpu.VMEM((N_SLABS, T, LANES), f32),
            pltpu.VMEM((T, D_MODEL), bf16),
            pltpu.VMEM((T, D_MODEL), f32),
            pltpu.VMEM((T, D_MODEL), f32),
            pltpu.VMEM((T, D_MODEL), f32),
            pltpu.VMEM((T, D_MODEL), bf16),
            pltpu.VMEM((2, T, COL_BLK), bf16),
            pltpu.VMEM((T, D_MODEL), f32),
        ],
        compiler_params=pltpu.CompilerParams(
            dimension_semantics=("arbitrary",), vmem_limit_bytes=VMEM_LIMIT),
        name=name,
    )(*args, *weights)


def kernel(x_prompt, x_sample, cache_k, cache_v, cache_conv, c_prompt, c_sample, rel_table, w_ada, b_ada, norm1_g, norm2_g, w_in, sink, w_attn_out, dw_w, dw_b, conv_ln_g, conv_ln_b, w_conv_out, w_out, w_ffn_up, w_ffn_down, final_g):
    assert w_ada.shape[0] == 1, "single-layer kernel"
    B, S, _ = x_prompt.shape
    DB, DS, _ = x_sample.shape
    nb_attn = ATTN_SAMPLE_TOKENS // DS
    nb_mix = MIX_TOKENS // DS

    bias = _bias_table(rel_table)
    n_seq = B + DB
    pad = (-n_seq) % 8
    c_all = jnp.concatenate([c_prompt, c_sample, jnp.zeros((pad, D_MODEL), f32)], axis=0)
    mod = _modulation(c_all, w_ada[0], b_ada).reshape(n_seq + pad, 6, D_MODEL)
    mod_p, mod_s = mod[:B], mod[B:B + DB]

    row = lambda v: v.reshape(1, D_MODEL)
    w = w_in[0]
    w_k = w[:, O_K:O_V].astype(bf16).reshape(D_MODEL, KV_HEADS, 1, HEAD_DIM)
    w_kd = jnp.broadcast_to(w_k, (D_MODEL, KV_HEADS, 2, HEAD_DIM)).reshape(D_MODEL, KV_HEADS * LANES)
    w_in_b = w.astype(bf16)
    attn_w = (bias, sink[0], row(norm1_g[0]), w_in_b, w_kd, w[:, O_V:O_GLU].T.astype(bf16),
              w_attn_out[0].astype(bf16))
    mix_w = (row(norm1_g[0]), w_in_b, dw_w[0], row(dw_b[0]),
             row(conv_ln_g[0]), row(conv_ln_b[0]), w_conv_out[0].astype(bf16), w_out[0].astype(bf16),
             row(norm2_g[0]), row(final_g), w_ffn_up[0].astype(bf16), w_ffn_down[0].astype(bf16))

    a1p, nkp, nvp = _attn_call(x_prompt, mod_p, (), attn_w, 1, ATTN_TOKENS, True, "attn_prompt")
    ck = cache_k[0].reshape(DB, WINDOW, KV_W)
    cv = cache_v[0].reshape(DB, WINDOW, KV_W)
    a1s, nks, nvs = _attn_call(x_sample, mod_s, (ck, cv), attn_w, nb_attn, DS, False, "attn_sample")

    tiles = lambda a, n, l: a.reshape(-1, n, l, D_MODEL)
    y_p, ncp = _mix_call(tiles(x_prompt, 1, MIX_TOKENS), tiles(a1p, 1, MIX_TOKENS), mod_p[:, None], None,
                         mix_w, 1, MIX_TOKENS, True, "mix_prompt")
    y_s, ncs = _mix_call(tiles(x_sample, nb_mix, DS), tiles(a1s, nb_mix, DS), tiles(mod_s, nb_mix, 6),
                         tiles(cache_conv[0], nb_mix, CONV_W - 1), mix_w, nb_mix, DS, False, "mix_sample")

    kv5 = lambda a: a.reshape(1, a.shape[0], WINDOW, KV_HEADS, HEAD_DIM)
    return (y_p.reshape(B, S, D_MODEL), y_s.reshape(DB, DS, D_MODEL), kv5(nkp), kv5(nvp),
            ncp.reshape(1, B, CONV_W - 1, D_MODEL), kv5(nks), kv5(nvs), ncs.reshape(1, DB, CONV_W - 1, D_MODEL))
```
